```python
import jax, jax.numpy as jnp
from jax import lax
import numpy as np

D_MODEL = 2048
BATCH = 4
SEQ = 2048
DEPTH = 2
DEC_BATCH = 128
DEC_SEQ = 8
PAST_LEN = 16384
PAGE_SIZE = 128

N_GROUPS = 4
GROUP_W = D_MODEL // N_GROUPS
MIX_W = N_GROUPS * GROUP_W
RW_HD = 64
RW_H = GROUP_W // RW_HD
RW_LORA_W = 32
RW_LORA_A = 32
RW_LORA_G = 96
RW_GN_EPS = 64e-5
ML_H = 4
ML_HD = GROUP_W // ML_H
CONV_W = 4
RT_H = 4
RT_HD = GROUP_W // RT_H
GL_H = 4
GL_DK = GROUP_W // (2 * GL_H)
GL_DV = GROUP_W // GL_H
GL_LORA = 16
GL_TAU = 16.0
CHUNK = 64
FFN_DENSE = 256 * ((8 * D_MODEL // 3 + 255) // 256)
N_EXPERTS = 8
TOP_K = 2
FFN_EXPERT = FFN_DENSE // 2
N_DENSE = (DEPTH + 1) // 2
N_MOE = DEPTH // 2
ALPHA = (2 * DEPTH) ** 0.25
BETA = (8 * DEPTH) ** -0.25
LN_EPS = 1e-5
GN_EPS = 1e-5

RW_SPLITS = (
    ('rw_r', GROUP_W), ('rw_k', GROUP_W), ('rw_v', GROUP_W),
    ('rw_wlo', RW_LORA_W), ('rw_alo', RW_LORA_A), ('rw_glo', RW_LORA_G),
)
IN_SPLITS = RW_SPLITS + (
    ('ml_q', GROUP_W), ('ml_k', GROUP_W), ('ml_v', GROUP_W), ('ml_o', GROUP_W),
    ('ml_i', ML_H), ('ml_f', ML_H),
    ('rt_q', GROUP_W), ('rt_k', GROUP_W), ('rt_v', GROUP_W), ('rt_g', GROUP_W),
    ('gl_q', GL_H * GL_DK), ('gl_k', GL_H * GL_DK), ('gl_v', GROUP_W),
    ('gl_alo', GL_LORA), ('gl_g', GROUP_W),
)
RW_COLS = 3 * GROUP_W + RW_LORA_W + RW_LORA_A + RW_LORA_G
N_IN_COLS = sum(w for _, w in IN_SPLITS)
VALUE_COLS = ('rw_v', 'ml_v', 'rt_v', 'gl_v')

kernel_name = 'hybrid_rwkv7_mlstm_retnet_gla_deepnorm_step'


def _split_cols(a, splits):
    out, off = {}, 0
    for name, width in splits:
        out[name] = a[..., off:off + width]
        off += width
    return out


def _chunk_len(T):
    return CHUNK if T % CHUNK == 0 else T


def _to_chunks(t, L):
    B, T = t.shape[:2]
    t = t.reshape((B, T // L, L) + t.shape[2:])
    return jnp.moveaxis(jnp.moveaxis(t, 1, 0), 2, 3)


def _from_chunks(t):
    t = jnp.moveaxis(jnp.moveaxis(t, 3, 2), 0, 1)
    return t.reshape((t.shape[0], t.shape[1] * t.shape[2]) + t.shape[3:])


def _head_norm(y, eps, rms=False):
    y = y.astype(jnp.float32)
    if not rms:
        y = y - jnp.mean(y, axis=-1, keepdims=True)
    return y * lax.rsqrt(jnp.mean(y * y, axis=-1, keepdims=True) + eps)


def _layer_norm(x, w, b):
    xf = x.astype(jnp.float32)
    mu = jnp.mean(xf, axis=-1, keepdims=True)
    var = jnp.mean(jnp.square(xf - mu), axis=-1, keepdims=True)
    return ((xf - mu) * lax.rsqrt(var + LN_EPS) * w + b).astype(x.dtype)


def _rwkv7(p_rw, shift_prev, S0, mu, w0, w2, a0, a2, g2, k_k, k_a, r_k, ln_w, ln_b):
    B, T, _ = p_rw.shape
    f32 = jnp.float32
    p_prev = jnp.concatenate([shift_prev[:, None, :].astype(p_rw.dtype), p_rw[:, :-1]], axis=1)
    pm = p_rw + (p_prev - p_rw) * mu
    c = _split_cols(pm, RW_SPLITS)
    logw = -jax.nn.softplus(-(w0 + jnp.tanh(c['rw_wlo']) @ w2).astype(f32)) - 0.5
    decay = jnp.exp(-jnp.exp(logw))
    a = jax.nn.sigmoid((a0 + c['rw_alo'] @ a2).astype(f32))
    g = jax.nn.sigmoid(c['rw_glo']) @ g2

    def heads(t):
        return t.astype(f32).reshape(B, T, RW_H, RW_HD)

    kk = heads(c['rw_k'] * k_k)
    kk = kk / jnp.maximum(jnp.sqrt(jnp.sum(kk * kk, axis=-1, keepdims=True)), 1e-12)
    k = c['rw_k'].astype(f32) * (1.0 + (a - 1.0) * k_a)
    r_h, w_h, k_h, v_h, a_h = heads(c['rw_r']), heads(decay), heads(k), heads(c['rw_v']), heads(a)

    def step(S, inp):
        r_t, w_t, k_t, v_t, kk_t, b_t = inp
        sa = jnp.einsum('bhvk,bhk->bhv', S, kk_t)
        S = S * w_t[:, :, None, :] - sa[..., None] * b_t[:, :, None, :] + v_t[..., None] * k_t[:, :, None, :]
        return S, jnp.einsum('bhvk,bhk->bhv', S, r_t)

    seq = tuple(jnp.moveaxis(t, 1, 0) for t in (r_h, w_h, k_h, v_h, kk, kk * a_h))
    S, y = lax.scan(step, S0.astype(f32), seq)
    y = jnp.moveaxis(y, 0, 1)
    y = _head_norm(y, RW_GN_EPS) * ln_w.reshape(RW_H, RW_HD) + ln_b.reshape(RW_H, RW_HD)
    y = y + jnp.sum(r_h * k_h * r_k, axis=-1, keepdims=True) * v_h
    out = y.reshape(B, T, GROUP_W) * g.astype(f32)
    return out.astype(p_rw.dtype), p_rw[:, -1], S.astype(S0.dtype)


def _mlstm(qk_pre, conv_prev, v, o_pre, i_pre, f_pre, C0, n0, m0, conv_w, conv_b, b_i, b_f, gn_w):
    B, T, _ = qk_pre.shape
    f32 = jnp.float32
    full = jnp.concatenate([conv_prev.astype(qk_pre.dtype), qk_pre], axis=1)
    conv = conv_b + full[:, 0:T] * conv_w[0]
    for j in range(1, CONV_W):
        conv = conv + full[:, j:j + T] * conv_w[j]
    conv = jax.nn.silu(conv.astype(f32))
    qh = conv[..., :GROUP_W].reshape(B, T, ML_H, ML_HD)
    kh = conv[..., GROUP_W:].reshape(B, T, ML_H, ML_HD) * ML_HD ** -0.5
    vh = v.astype(f32).reshape(B, T, ML_H, ML_HD)
    ig = (i_pre + b_i).astype(f32)
    lf = jax.nn.log_sigmoid((f_pre + b_f).astype(f32))
    L = _chunk_len(T)
    causal = jnp.tril(jnp.ones((L, L), dtype=bool))

    def step(carry, inp):
        C, n, m = carry
        q_c, k_c, v_c, i_c, lf_c = inp
        b = jnp.cumsum(lf_c, axis=-1)
        dlog = jnp.where(causal, b[..., :, None] - b[..., None, :] + i_c[..., None, :], -jnp.inf)
        inter = b + m[..., None]
        m_t = jnp.maximum(inter, jnp.max(dlog, axis=-1))
        dmat = jnp.exp(dlog - m_t[..., None])
        s_in = jnp.exp(inter - m_t)
        A = jnp.einsum('bhld,bhsd->bhls', q_c, k_c) * dmat
        num = jnp.einsum('bhls,bhsv->bhlv', A, v_c) + s_in[..., None] * jnp.einsum('bhvd,bhld->bhlv', C, q_c)
        den = jnp.sum(A, axis=-1) + s_in * jnp.einsum('bhd,bhld->bhl', n, q_c)
        h = num / jnp.maximum(jnp.abs(den), jnp.exp(-m_t))[..., None]
        m_new = m_t[..., -1]
        carry_dec = jnp.exp(b[..., -1] + m - m_new)
        wk = jnp.exp(b[..., -1:] - b + i_c - m_new[..., None])[..., None] * k_c
        C = carry_dec[..., None, None] * C + jnp.einsum('bhlv,bhld->bhvd', v_c, wk)
        n = carry_dec[..., None] * n + jnp.sum(wk, axis=2)
        return (C, n, m_new), h

    seq = tuple(_to_chunks(t, L) for t in (qh, kh, vh, ig, lf))
    (C, n, m), h = lax.scan(step, (C0.astype(f32), n0.astype(f32), m0.astype(f32)), seq)
    h = _head_norm(_from_chunks(h), GN_EPS) * gn_w.reshape(ML_H, ML_HD)
    out = jax.nn.sigmoid(o_pre.astype(f32)) * h.reshape(B, T, GROUP_W)
    return out.astype(qk_pre.dtype), full[:, T:], C.astype(C0.dtype), n.astype(n0.dtype), m.astype(m0.dtype)


def _retention(q, k, v, g, S0, gn_w, pos0):
    B, T, _ = q.shape
    f32 = jnp.float32
    pos = pos0 + jnp.arange(T, dtype=f32)
    inv = 1.0 / (10000.0 ** jnp.linspace(0.0, 1.0, RT_HD // 2, dtype=f32))
    ang = pos[:, None] * inv[None, :]
    cos, sin = jnp.cos(ang)[:, None, :], jnp.sin(ang)[:, None, :]

    def rot(t):
        t = t.astype(f32).reshape(B, T, RT_H, RT_HD)
        t1, t2 = t[..., :RT_HD // 2], t[..., RT_HD // 2:]
        return jnp.concatenate([t1 * cos - t2 * sin, t1 * sin + t2 * cos], axis=-1)

    qh = rot(q)
    kh = rot(k) * RT_HD ** -0.5
    vh = v.astype(f32).reshape(B, T, RT_H, RT_HD)
    lg = jnp.log1p(-jnp.exp2(-5.0 - jnp.arange(RT_H, dtype=f32)))
    L = _chunk_len(T)
    idx = jnp.arange(L, dtype=f32)
    rel = idx[:, None] - idx[None, :]
    dmat = jnp.where(rel >= 0, jnp.exp(jnp.maximum(rel, 0.0) * lg[:, None, None]), 0.0)
    q_dec = jnp.exp((idx + 1.0) * lg[:, None])
    k_dec = jnp.exp((L - 1.0 - idx) * lg[:, None])
    c_dec = jnp.exp(L * lg)

    def step(S, inp):
        q_c, k_c, v_c = inp
        A = jnp.einsum('bhjd,bhsd->bhjs', q_c, k_c) * dmat
        o = jnp.einsum('bhjs,bhsv->bhjv', A, v_c) + jnp.einsum('bhjd,bhdv->bhjv', q_c * q_dec[..., None], S)
        S = c_dec[:, None, None] * S + jnp.einsum('bhsd,bhsv->bhdv', k_c * k_dec[..., None], v_c)
        return S, o

    S, o = lax.scan(step, S0.astype(f32), (_to_chunks(qh, L), _to_chunks(kh, L), _to_chunks(vh, L)))
    o = _head_norm(_from_chunks(o), GN_EPS) * gn_w.reshape(RT_H, RT_HD)
    out = jax.nn.silu(g.astype(f32)) * o.reshape(B, T, GROUP_W)
    return out.astype(q.dtype), S.astype(S0.dtype)


def _gla(q, k, v, alo, g, S0, a2, ab, gn_w):
    B, T, _ = q.shape
    f32 = jnp.float32
    la = jax.nn.log_sigmoid((alo @ a2 + ab).astype(f32)) / GL_TAU
    qh = q.astype(f32).reshape(B, T, GL_H, GL_DK) * GL_DK ** -0.5
    kh = k.astype(f32).reshape(B, T, GL_H, GL_DK)
    vh = v.astype(f32).reshape(B, T, GL_H, GL_DV)
    lah = la.reshape(B, T, GL_H, GL_DK)
    L = _chunk_len(T)
    causal = jnp.tril(jnp.ones((L, L), dtype=bool))[:, :, None]

    def step(S, inp):
        q_c, k_c, v_c, la_c = inp
        b = jnp.cumsum(la_c, axis=2)
        dec = jnp.exp(jnp.where(causal, b[:, :, :, None, :] - b[:, :, None, :, :], -jnp.inf))
        A = jnp.einsum('bhjsd,bhsd->bhjs', q_c[:, :, :, None, :] * dec, k_c)
        o = jnp.einsum('bhjs,bhsv->bhjv', A, v_c) + jnp.einsum('bhjd,bhdv->bhjv', q_c * jnp.exp(b), S)
        bL = b[:, :, -1:, :]
        S = jnp.exp(bL[:, :, 0, :])[..., None] * S + jnp.einsum('bhsd,bhsv->bhdv', k_c * jnp.exp(bL - b), v_c)
        return S, o

    seq = tuple(_to_chunks(t, L) for t in (qh, kh, vh, lah))
    S, o = lax.scan(step, S0.astype(f32), seq)
    o = _head_norm(_from_chunks(o), GN_EPS, rms=True) * gn_w.reshape(GL_H, GL_DV)
    out = jax.nn.silu(g.astype(f32)) * o.reshape(B, T, GROUP_W)
    return out.astype(q.dtype), S.astype(S0.dtype)


def _swiglu(x, w1, w3, w2):
    return (jax.nn.silu(x @ w1) * (x @ w3)) @ w2


def _moe(x, router, w1, w3, w2):
    logits = (x @ router).astype(jnp.float32)
    top_v, top_i = lax.top_k(logits, TOP_K)
    gates = jax.nn.softmax(top_v, axis=-1)
    comb = jnp.einsum('btk,btke->bte', gates, jax.nn.one_hot(top_i, N_EXPERTS, dtype=jnp.float32)).astype(x.dtype)
    out = jnp.zeros_like(x)
    for e in range(N_EXPERTS):
        out = out + comb[..., e:e + 1] * _swiglu(x, w1[e], w3[e], w2[e])
    return out


def _mixer_block(x, st, prm, l, pos0):
    rw_shift, rw_wkv, ml_conv, ml_C, ml_n, ml_m, rt_S, gl_S = st
    P = x @ prm['w_in'][l]
    c = _split_cols(P, IN_SPLITS)
    rw_out, rw_shift, rw_wkv = _rwkv7(
        P[..., :RW_COLS], rw_shift, rw_wkv, prm['rw_mu'][l], prm['rw_w0'][l], prm['rw_w2'][l],
        prm['rw_a0'][l], prm['rw_a2'][l], prm['rw_g2'][l], prm['rw_kk'][l], prm['rw_ka'][l],
        prm['rw_rk'][l], prm['rw_ln_w'][l], prm['rw_ln_b'][l])
    ml_out, ml_conv, ml_C, ml_n, ml_m = _mlstm(
        P[..., RW_COLS:RW_COLS + 2 * GROUP_W], ml_conv, c['ml_v'], c['ml_o'], c['ml_i'], c['ml_f'],
        ml_C, ml_n, ml_m, prm['ml_conv_w'][l], prm['ml_conv_b'][l], prm['ml_bi'][l], prm['ml_bf'][l],
        prm['ml_gn_w'][l])
    rt_out, rt_S = _retention(c['rt_q'], c['rt_k'], c['rt_v'], c['rt_g'], rt_S, prm['rt_gn_w'][l], pos0)
    gl_out, gl_S = _gla(c['gl_q'], c['gl_k'], c['gl_v'], c['gl_alo'], c['gl_g'], gl_S,
                        prm['gl_a2'][l], prm['gl_ab'][l], prm['gl_gn_w'][l])
    mix = jnp.concatenate([rw_out, ml_out, rt_out, gl_out], axis=-1) @ prm['w_out'][l]
    return mix, (rw_shift, rw_wkv, ml_conv, ml_C, ml_n, ml_m, rt_S, gl_S)


def _trunk(x, states, prm, pos0):
    outs = [[] for _ in states]
    for l in range(DEPTH):
        mix, new = _mixer_block(x, tuple(s[l] for s in states), prm, l, pos0)
        x = _layer_norm(ALPHA * x + mix, prm['ln1_w'][l], prm['ln1_b'][l])
        if l % 2 == 0:
            f = _swiglu(x, prm['ffn_w1'][l // 2], prm['ffn_w3'][l // 2], prm['ffn_w2'][l // 2])
        else:
            f = _moe(x, prm['moe_router'][l // 2], prm['moe_w1'][l // 2], prm['moe_w3'][l // 2], prm['moe_w2'][l // 2])
        x = _layer_norm(ALPHA * x + f, prm['ln2_w'][l], prm['ln2_b'][l])
        for o, s in zip(outs, new):
            o.append(s)
    return x, tuple(jnp.stack(o) for o in outs)


def setup_inputs(seed: int = 0) -> dict:
    key = jax.random.key(seed)
    keys = iter(jax.random.split(key, 64))
    f32 = jnp.float32

    def nrm(shape, scale=1.0):
        return jax.random.normal(next(keys), shape, f32) * scale

    def gain(shape):
        return 1.0 + nrm(shape, 0.1)

    col_scale = jnp.concatenate([jnp.full((w,), BETA if n in VALUE_COLS else 1.0, f32) for n, w in IN_SPLITS])
    return {
        'x_prompt': nrm((BATCH, SEQ, D_MODEL)),
        'x_sample': nrm((DEC_BATCH, DEC_SEQ, D_MODEL)),
        'state_rw_shift': nrm((DEPTH, DEC_BATCH, RW_COLS)),
        'state_rw_wkv': nrm((DEPTH, DEC_BATCH, RW_H, RW_HD, RW_HD), 0.3),
        'state_ml_conv': nrm((DEPTH, DEC_BATCH, CONV_W - 1, 2 * GROUP_W)),
        'state_ml_C': nrm((DEPTH, DEC_BATCH, ML_H, ML_HD, ML_HD), 0.3),
        'state_ml_n': nrm((DEPTH, DEC_BATCH, ML_H, ML_HD), 0.3),
        'state_ml_m': nrm((DEPTH, DEC_BATCH, ML_H)),
        'state_rt_S': nrm((DEPTH, DEC_BATCH, RT_H, RT_HD, RT_HD)),
        'state_gl_S': nrm((DEPTH, DEC_BATCH, GL_H, GL_DK, GL_DV), 0.5),
        'w_in': nrm((DEPTH, D_MODEL, N_IN_COLS), D_MODEL ** -0.5) * col_scale,
        'rw_mu': jax.random.uniform(next(keys), (DEPTH, RW_COLS), f32),
        'rw_w0': jnp.linspace(-6.0, -1.0, GROUP_W, dtype=f32) + nrm((DEPTH, GROUP_W), 0.1),
        'rw_w2': nrm((DEPTH, RW_LORA_W, GROUP_W), 0.1),
        'rw_a0': nrm((DEPTH, GROUP_W), 0.1),
        'rw_a2': nrm((DEPTH, RW_LORA_A, GROUP_W), 0.1),
        'rw_g2': nrm((DEPTH, RW_LORA_G, GROUP_W), RW_LORA_G ** -0.5),
        'rw_kk': 0.85 + nrm((DEPTH, GROUP_W), 0.05),
        'rw_ka': gain((DEPTH, GROUP_W)),
        'rw_rk': nrm((DEPTH, RW_H, RW_HD), 0.1),
        'rw_ln_w': gain((DEPTH, GROUP_W)),
        'rw_ln_b': nrm((DEPTH, GROUP_W), 0.02),
        'ml_conv_w': nrm((DEPTH, CONV_W, 2 * GROUP_W), CONV_W ** -0.5),
        'ml_conv_b': nrm((DEPTH, 2 * GROUP_W), 0.02),
        'ml_bi': nrm((DEPTH, ML_H), 0.1),
        'ml_bf': jnp.linspace(3.0, 6.0, ML_H, dtype=f32) + nrm((DEPTH, ML_H), 0.1),
        'ml_gn_w': gain((DEPTH, GROUP_W)),
        'rt_gn_w': gain((DEPTH, GROUP_W)),
        'gl_a2': nrm((DEPTH, GL_LORA, GL_H * GL_DK), GL_LORA ** -0.5),
        'gl_ab': nrm((DEPTH, GL_H * GL_DK), 0.1),
        'gl_gn_w': gain((DEPTH, GROUP_W)),
        'w_out': nrm((DEPTH, MIX_W, D_MODEL), BETA * MIX_W ** -0.5),
        'ln1_w': gain((DEPTH, D_MODEL)),
        'ln1_b': nrm((DEPTH, D_MODEL), 0.02),
        'ln2_w': gain((DEPTH, D_MODEL)),
        'ln2_b': nrm((DEPTH, D_MODEL), 0.02),
        'ffn_w1': nrm((N_DENSE, D_MODEL, FFN_DENSE), BETA * D_MODEL ** -0.5),
        'ffn_w3': nrm((N_DENSE, D_MODEL, FFN_DENSE), BETA * D_MODEL ** -0.5),
        'ffn_w2': nrm((N_DENSE, FFN_DENSE, D_MODEL), BETA * FFN_DENSE ** -0.5),
        'moe_router': nrm((N_MOE, D_MODEL, N_EXPERTS), D_MODEL ** -0.5),
        'moe_w1': nrm((N_MOE, N_EXPERTS, D_MODEL, FFN_EXPERT), BETA * D_MODEL ** -0.5),
        'moe_w3': nrm((N_MOE, N_EXPERTS, D_MODEL, FFN_EXPERT), BETA * D_MODEL ** -0.5),
        'moe_w2': nrm((N_MOE, N_EXPERTS, FFN_EXPERT, D_MODEL), BETA * FFN_EXPERT ** -0.5),
    }


def reference(x_prompt, x_sample, state_rw_shift, state_rw_wkv, state_ml_conv, state_ml_C, state_ml_n,
              state_ml_m, state_rt_S, state_gl_S, w_in, rw_mu, rw_w0, rw_w2, rw_a0, rw_a2, rw_g2, rw_kk,
              rw_ka, rw_rk, rw_ln_w, rw_ln_b, ml_conv_w, ml_conv_b, ml_bi, ml_bf, ml_gn_w, rt_gn_w, gl_a2,
              gl_ab, gl_gn_w, w_out, ln1_w, ln1_b, ln2_w, ln2_b, ffn_w1, ffn_w3, ffn_w2, moe_router,
              moe_w1, moe_w3, moe_w2):
    prm = dict(w_in=w_in, rw_mu=rw_mu, rw_w0=rw_w0, rw_w2=rw_w2, rw_a0=rw_a0, rw_a2=rw_a2, rw_g2=rw_g2,
               rw_kk=rw_kk, rw_ka=rw_ka, rw_rk=rw_rk, rw_ln_w=rw_ln_w, rw_ln_b=rw_ln_b,
               ml_conv_w=ml_conv_w, ml_conv_b=ml_conv_b, ml_bi=ml_bi, ml_bf=ml_bf, ml_gn_w=ml_gn_w,
               rt_gn_w=rt_gn_w, gl_a2=gl_a2, gl_ab=gl_ab, gl_gn_w=gl_gn_w, w_out=w_out,
               ln1_w=ln1_w, ln1_b=ln1_b, ln2_w=ln2_w, ln2_b=ln2_b, ffn_w1=ffn_w1, ffn_w3=ffn_w3,
               ffn_w2=ffn_w2, moe_router=moe_router, moe_w1=moe_w1, moe_w3=moe_w3, moe_w2=moe_w2)
    b_p = x_prompt.shape[0]

    def zeros_like_state(s):
        return jnp.zeros((s.shape[0], b_p) + s.shape[2:], s.dtype)

    sample_states = (state_rw_shift, state_rw_wkv, state_ml_conv, state_ml_C, state_ml_n, state_ml_m,
                     state_rt_S, state_gl_S)
    prompt_states = tuple(zeros_like_state(s) for s in sample_states)
    y_prompt, ps = _trunk(x_prompt, prompt_states, prm, 0)
    y_sample, ss = _trunk(x_sample, sample_states, prm, PAST_LEN)
    p_rw_shift, p_rw_wkv, p_ml_conv, p_ml_C, p_ml_n, p_ml_m, p_rt_S, p_gl_S = ps
    s_rw_shift, s_rw_wkv, s_ml_conv, s_ml_C, s_ml_n, s_ml_m, s_rt_S, s_gl_S = ss
    return (y_prompt, y_sample, p_rw_shift, s_rw_shift, p_rw_wkv, s_rw_wkv, p_ml_conv, s_ml_conv,
            p_ml_C, s_ml_C, p_ml_n, s_ml_n, p_ml_m, s_ml_m, p_rt_S, s_rt_S, p_gl_S, s_gl_S)
```

```python
import functools
import math

import jax
import jax.numpy as jnp
from jax import lax
from jax.experimental import pallas as pl
from jax.experimental.pallas import tpu as pltpu

F32 = jnp.float32
BF16 = jnp.bfloat16

D_MODEL = 2048
GROUP_W = 512
RW_COLS = 1696
RW_H, RW_HD = 8, 64
HEADS, HEAD_D = 4, 128
GL_DK = 64
CHUNK = 64
ALPHA = 4.0 ** 0.25
LN_EPS = 1e-5
GN_EPS = 1e-5
RW_GN_EPS = 64e-5
GL_TAU = 16.0
PAST_LEN = 16384
N_EXPERTS = 8

LANES = 128
SUBLANES = 8
VMEM_LIMIT = 48 * 1024 * 1024

N_PACKED = 15 * GROUP_W
SM_MLI, SM_MLF, SM_GLA = 160, 164, 168


def _nn(a, b):
    return jnp.dot(a, b, preferred_element_type=F32)


def _nt(a, b):
    return lax.dot_general(a, b, (((1,), (1,)), ((), ())), preferred_element_type=F32)


def _tn(a, b):
    return lax.dot_general(a, b, (((0,), (0,)), ((), ())), preferred_element_type=F32)


def _bf(x):
    return x.astype(BF16)


def _split3(x):
    hi = x.astype(BF16)
    r1 = x - hi.astype(F32)
    mid = r1.astype(BF16)
    lo = (r1 - mid.astype(F32)).astype(BF16)
    return hi, mid, lo


def _dot3(x, m, kind="nn"):
    f = {"nn": _nn, "nt": _nt, "tn": _tn}[kind]
    hi, mid, lo = _split3(x)
    return f(hi, m) + f(mid, m) + f(lo, m)


def _dot3r(m, x, kind="nn"):
    f = {"nn": _nn, "nt": _nt, "tn": _tn}[kind]
    hi, mid, lo = _split3(x)
    return f(m, hi) + f(m, mid) + f(m, lo)


def _sigmoid(x):
    return 1.0 / (1.0 + jnp.exp(-x))


def _softplus(x):
    return jnp.maximum(x, 0.0) + jnp.log1p(jnp.exp(-jnp.abs(x)))


def _log_sigmoid(x):
    return -_softplus(-x)


def _silu(x):
    return x * _sigmoid(x)


def _iota(shape, axis):
    return lax.broadcasted_iota(jnp.int32, shape, axis)


def _shift_rows(x, prev8, j):
    rows = x.shape[0]
    xr = pltpu.roll(x, j, 0)
    pr = pltpu.roll(prev8, j, 0)
    first = jnp.where(_iota((SUBLANES, x.shape[1]), 0) < j, pr, xr[0:SUBLANES])
    if rows == SUBLANES:
        return first
    return jnp.concatenate([first, xr[SUBLANES:]], axis=0)


def _pick(n, cands):
    return next(c for c in cands if n % c == 0)


def _cparams(n_axes):
    return pltpu.CompilerParams(dimension_semantics=("arbitrary",) * n_axes, vmem_limit_bytes=VMEM_LIMIT)


def _mm_kernel(n_in, *refs):
    xs, ws, o_ref = refs[:n_in], refs[n_in:2 * n_in], refs[2 * n_in]
    acc = _nn(xs[0][...], ws[0][...])
    for x_ref, w_ref in zip(xs[1:], ws[1:]):
        acc = acc + _nn(x_ref[...], w_ref[...])
    o_ref[...] = acc


def _mm(xs, ws, tm, tn, name):
    M, N = xs[0].shape[0], ws[0].shape[1]
    n_in = len(xs)
    in_specs = ([pl.BlockSpec((tm, x.shape[1]), lambda j, i: (i, 0)) for x in xs]
                + [pl.BlockSpec((w.shape[0], tn), lambda j, i: (0, j)) for w in ws])
    return pl.pallas_call(
        functools.partial(_mm_kernel, n_in),
        grid=(N // tn, M // tm),
        in_specs=in_specs,
        out_specs=pl.BlockSpec((tm, tn), lambda j, i: (i, j)),
        out_shape=jax.ShapeDtypeStruct((M, N), F32),
        compiler_params=_cparams(2),
        name=name,
    )(*xs, *ws)


def _add_ln_kernel(n_y, gated, x_ref, *refs):
    ys = refs[:n_y]
    gs = refs[n_y:2 * n_y] if gated else ()
    w_ref, b_ref, o_ref, ob_ref = refs[-4:]
    z = ALPHA * x_ref[...]
    for i, y_ref in enumerate(ys):
        y = y_ref[...]
        if gated:
            y = gs[i][...] * y
        z = z + y
    mu = jnp.mean(z, axis=-1, keepdims=True)
    zc = z - mu
    var = jnp.mean(zc * zc, axis=-1, keepdims=True)
    out = zc * lax.rsqrt(var + LN_EPS) * w_ref[...] + b_ref[...]
    o_ref[...] = out
    ob_ref[...] = out.astype(BF16)


def _add_ln(x, ys, gates, w, b):
    M = x.shape[0]
    tm = _pick(M, (256, 128))
    gated = gates is not None
    row = pl.BlockSpec((tm, D_MODEL), lambda i: (i, 0))
    in_specs = [row] + [row] * len(ys)
    args = [x] + list(ys)
    if gated:
        in_specs += [pl.BlockSpec((tm, 1), lambda i: (i, 0))] * len(ys)
        args += list(gates)
    in_specs += [pl.BlockSpec((1, D_MODEL), lambda i: (0, 0))] * 2
    args += [w.reshape(1, D_MODEL), b.reshape(1, D_MODEL)]
    return pl.pallas_call(
        functools.partial(_add_ln_kernel, len(ys), gated),
        grid=(M // tm,),
        in_specs=in_specs,
        out_specs=[row, row],
        out_shape=[jax.ShapeDtypeStruct((M, D_MODEL), F32), jax.ShapeDtypeStruct((M, D_MODEL), BF16)],
        compiler_params=_cparams(1),
        name="add_ln",
    )(*args)


def _ffn_kernel(te_ref, x_ref, w1_ref, w3_ref, w2_ref, o_ref):
    f = pl.program_id(1)
    x = x_ref[...]
    h = _silu(_nn(x, w1_ref[...])) * _nn(x, w3_ref[...])
    part = _nn(h.astype(BF16), w2_ref[...])

    @pl.when(f == 0)
    def _():
        o_ref[...] = part

    @pl.when(f > 0)
    def _():
        o_ref[...] += part


def _ffn(xs, tile_expert, w1, w3, w2, tm, tf):
    R = xs.shape[0]
    F = w1.shape[2]
    grid_spec = pltpu.PrefetchScalarGridSpec(
        num_scalar_prefetch=1,
        grid=(R // tm, F // tf),
        in_specs=[
            pl.BlockSpec((tm, D_MODEL), lambda i, f, te: (i, 0)),
            pl.BlockSpec((None, D_MODEL, tf), lambda i, f, te: (te[i], 0, f)),
            pl.BlockSpec((None, D_MODEL, tf), lambda i, f, te: (te[i], 0, f)),
            pl.BlockSpec((None, tf, D_MODEL), lambda i, f, te: (te[i], f, 0)),
        ],
        out_specs=pl.BlockSpec((tm, D_MODEL), lambda i, f, te: (i, 0)),
    )
    return pl.pallas_call(
        _ffn_kernel,
        grid_spec=grid_spec,
        out_shape=jax.ShapeDtypeStruct((R, D_MODEL), F32),
        compiler_params=_cparams(2),
        name="ffn",
    )(tile_expert, xs, w1, w3, w2)


def _router_kernel(x_ref, r_ref, g_ref, i_ref):
    xh, xm, xl = _split3(x_ref[...])
    rh, rm, rl = _split3(r_ref[...])
    logits = (_nn(xh, rh) + _nn(xh, rm) + _nn(xm, rh)) + (_nn(xh, rl) + _nn(xl, rh) + _nn(xm, rm))
    shape = logits.shape
    lane = _iota(shape, 1).astype(F32)
    neg = jnp.float32(-jnp.inf)
    l1 = jnp.where(lane < N_EXPERTS, logits, neg)
    m1 = jnp.max(l1, axis=1, keepdims=True)
    i1 = jnp.min(jnp.where(l1 == m1, lane, float(LANES)), axis=1, keepdims=True)
    l2 = jnp.where(lane == i1, neg, l1)
    m2 = jnp.max(l2, axis=1, keepdims=True)
    i2 = jnp.min(jnp.where(l2 == m2, lane, float(LANES)), axis=1, keepdims=True)
    e = jnp.exp(m2 - m1)
    g1 = 1.0 / (1.0 + e)
    g2 = e / (1.0 + e)
    g_ref[...] = jnp.where(lane == 0.0, g1, jnp.where(lane == 1.0, g2, 0.0))
    i_ref[...] = jnp.where(lane == 0.0, i1, jnp.where(lane == 1.0, i2, 0.0)).astype(jnp.int32)


def _router(x, router_pad):
    M = x.shape[0]
    tm = _pick(M, (512, 256, 128))
    return pl.pallas_call(
        _router_kernel,
        grid=(M // tm,),
        in_specs=[pl.BlockSpec((tm, D_MODEL), lambda i: (i, 0)),
                  pl.BlockSpec((D_MODEL, LANES), lambda i: (0, 0))],
        out_specs=[pl.BlockSpec((tm, LANES), lambda i: (i, 0))] * 2,
        out_shape=[jax.ShapeDtypeStruct((M, LANES), F32), jax.ShapeDtypeStruct((M, LANES), jnp.int32)],
        compiler_params=_cparams(1),
        name="router",
    )(x, router_pad)


def _rw_prep(pm, prm):
    (w0, a0, w2p, a2p, g2p, k_k, k_a, _, _, _, g64, _) = prm
    r = pm[:, 0:512]
    k = pm[:, 512:1024]
    v = pm[:, 1024:1536]
    sm = pm[:, 1536:1792]
    logw = -_softplus(-(w0 + _nn(_bf(jnp.tanh(sm)), w2p))) - 0.5
    w = jnp.exp(-jnp.exp(logw))
    a = _sigmoid(a0 + _nn(_bf(sm), a2p))
    g = _nn(_bf(_sigmoid(sm)), g2p)
    kk = k * k_k
    ss = jnp.concatenate([_dot3((kk * kk)[:, LANES * p:LANES * (p + 1)], g64) for p in range(4)], axis=1)
    kk = kk / jnp.maximum(jnp.sqrt(ss), 1e-12)
    kmod = k * (1.0 + (a - 1.0) * k_a)
    return r, w, kmod, kk, kk * a, v, g


def _rw_post(y, r, kmod, v, g, prm):
    (_, _, _, _, _, _, _, r_k, ln_w, ln_b, g64, g64m) = prm

    def per_head(x, m):
        return jnp.concatenate([_dot3(x[:, LANES * p:LANES * (p + 1)], m) for p in range(4)], axis=1)

    yc = y - per_head(y, g64m)
    var = per_head(yc * yc, g64m)
    yn = yc * lax.rsqrt(var + RW_GN_EPS) * ln_w + ln_b
    bonus = per_head(r * kmod * r_k, g64) * v
    return (yn + bonus) * g


def _rw_tiles(slab, row0, refs):
    return [[q[slab, pl.ds(row0, SUBLANES), LANES * p:LANES * (p + 1)] for q in refs[:5]] for p in range(4)]


def _rw_step(S_ref, sidx, slab, tiles, j, col, refs):
    vt_s, yt_s = refs[5:]
    lo = _iota((1, LANES), 1) < RW_HD
    onehot = _iota((1, LANES), 1) == col

    def halves(x):
        s_lo = jnp.sum(jnp.where(lo, x, 0.0), axis=1, keepdims=True)
        s_hi = jnp.sum(jnp.where(lo, 0.0, x), axis=1, keepdims=True)
        return s_lo, s_hi

    for p in range(4):
        r, w, k, kk, b = (q[j:j + 1, :] for q in tiles[p])
        S = S_ref[sidx, p]
        sa_lo, sa_hi = halves(S * kk)
        sa = jnp.where(lo, sa_lo, sa_hi)
        vsel = jnp.where(onehot, vt_s[slab, p], 0.0)
        v_lo = jnp.sum(vsel[0:RW_HD], axis=1, keepdims=True)
        v_hi = jnp.sum(vsel[RW_HD:], axis=1, keepdims=True)
        vc = jnp.where(lo, v_lo, v_hi)
        S = S * w - sa * b + vc * k
        S_ref[sidx, p] = S
        y_lo, y_hi = halves(S * r)
        yt_s[slab, p, 0:RW_HD, :] = jnp.where(onehot, y_lo, yt_s[slab, p, 0:RW_HD, :])
        yt_s[slab, p, RW_HD:, :] = jnp.where(onehot, y_hi, yt_s[slab, p, RW_HD:, :])


def _rw_store_prep(slab, pm, prm, scr):
    r_s, w_s, k_s, kk_s, b_s, g_s, v_s, vt_s, yt_s = scr
    r, w, kmod, kk, bb, v, g = _rw_prep(pm, prm)
    r_s[slab], w_s[slab], k_s[slab], kk_s[slab], b_s[slab] = r, w, kmod, kk, bb
    g_s[slab], v_s[slab] = g, v
    for p in range(4):
        vt_s[slab, p] = v[:, LANES * p:LANES * (p + 1)].T
    yt_s[slab] = jnp.zeros(yt_s.shape[1:], F32)


def _rw_finish(slab, prm, scr):
    r_s, w_s, k_s, kk_s, b_s, g_s, v_s, vt_s, yt_s = scr
    y = jnp.concatenate([yt_s[slab, p].T for p in range(4)], axis=1)
    return _rw_post(y, r_s[slab], k_s[slab], v_s[slab], g_s[slab], prm)


def _rw_params(refs):
    return tuple(r[...] for r in refs)


N_RW_PRM = 13


def _rwkv_prompt_kernel(nseq, tb_rows, *refs):
    p_refs = refs[:nseq]
    sh0_ref, s0_ref = refs[nseq:nseq + 2]
    pos = nseq + 2
    prm_refs = refs[pos:pos + N_RW_PRM]
    o_ref, sh_out_ref, s_out_ref, S_scr, prev_scr = refs[pos + N_RW_PRM:pos + N_RW_PRM + 5]
    scr = refs[pos + N_RW_PRM + 5:]
    tb = pl.program_id(0)

    @pl.when(tb == 0)
    def _():
        S_scr[...] = s0_ref[...]
        prev_scr[...] = sh0_ref[...]

    mu = prm_refs[0][...]
    prm = _rw_params(prm_refs[1:])
    for s in range(nseq):
        p = p_refs[s][...]
        pp = _shift_rows(p, prev_scr[s], 1)
        prev_scr[s] = p[tb_rows - SUBLANES:tb_rows]
        _rw_store_prep(s, p + (pp - p) * mu, prm, scr)

    step_refs = scr[:5] + scr[7:]

    def group(gi, c):
        row0 = pl.multiple_of(gi * SUBLANES, SUBLANES)
        tiles = [_rw_tiles(s, row0, step_refs) for s in range(nseq)]
        for j in range(SUBLANES):
            for s in range(nseq):
                _rw_step(S_scr, s, s, tiles[s], j, row0 + j, step_refs)
        return c

    lax.fori_loop(0, tb_rows // SUBLANES, group, 0)
    for s in range(nseq):
        o_ref[s] = _rw_finish(s, prm, scr).astype(BF16)
    sh_out_ref[...] = prev_scr[...]
    s_out_ref[...] = S_scr[...]


def _rwkv_sample_kernel(nsq, T, p_ref, sh0_ref, s0_ref, *refs):
    prm_refs = refs[:N_RW_PRM]
    o_ref, sh_out_ref, s_out_ref = refs[N_RW_PRM:N_RW_PRM + 3]
    scr = refs[N_RW_PRM + 3:]
    mu = prm_refs[0][...]
    prm = _rw_params(prm_refs[1:])
    p = p_ref[...]
    p3 = p.reshape(nsq, T, p.shape[1])
    first = _iota(p3.shape, 1) == 0
    pp = jnp.where(first, sh0_ref[...], pltpu.roll(p3, 1, 1)).reshape(p.shape)
    sh_out_ref[...] = p3[:, T - 1:T, :]
    _rw_store_prep(0, p + (pp - p) * mu, prm, scr)
    s_out_ref[...] = s0_ref[...]
    step_refs = scr[:5] + scr[7:]

    def seq(b, c):
        row0 = pl.multiple_of(b * T, SUBLANES)
        tiles = _rw_tiles(0, row0, step_refs)
        for t in range(T):
            _rw_step(s_out_ref, b, 0, tiles, t, row0 + t, step_refs)
        return c

    lax.fori_loop(0, nsq, seq, 0)
    o_ref[...] = _rw_finish(0, prm, scr).astype(BF16)


def _rw_scratch(nslab, rows):
    big = pltpu.VMEM((nslab, rows, GROUP_W), F32)
    return [big] * 7 + [pltpu.VMEM((nslab, 4, LANES, rows), F32)] * 2


def _rw_prm_specs(prm_list, nidx):
    zero = (lambda *a: (0, 0))
    return [pl.BlockSpec(x.shape, zero) for x in prm_list]


def _rwkv_prompt(P, row0, nseq, T, sh0, s0, prm_list, tb_rows=LANES):
    nblk = T // tb_rows
    in_specs = [pl.BlockSpec((tb_rows, 4 * GROUP_W), functools.partial(lambda s, t: ((row0 + s * T) // tb_rows + t, 0), s))
                for s in range(nseq)]
    in_specs += [pl.BlockSpec(sh0.shape, lambda t: (0, 0, 0)), pl.BlockSpec(s0.shape, lambda t: (0, 0, 0, 0))]
    in_specs += _rw_prm_specs(prm_list, 1)
    out_specs = [pl.BlockSpec((nseq, tb_rows, GROUP_W), lambda t: (0, t, 0)),
                 pl.BlockSpec(sh0.shape, lambda t: (0, 0, 0)), pl.BlockSpec(s0.shape, lambda t: (0, 0, 0, 0))]
    out_shape = [jax.ShapeDtypeStruct((nseq, T, GROUP_W), BF16),
                 jax.ShapeDtypeStruct(sh0.shape, F32), jax.ShapeDtypeStruct(s0.shape, F32)]
    scratch = [pltpu.VMEM(s0.shape, F32), pltpu.VMEM(sh0.shape, F32)] + _rw_scratch(nseq, tb_rows)
    outs = pl.pallas_call(
        functools.partial(_rwkv_prompt_kernel, nseq, tb_rows),
        grid=(nblk,),
        in_specs=in_specs,
        out_specs=out_specs,
        out_shape=out_shape,
        scratch_shapes=scratch,
        compiler_params=_cparams(1),
        name="rwkv_prompt",
    )(*([P] * nseq), sh0, s0, *prm_list)
    return outs[0].reshape(nseq * T, GROUP_W), outs[1], outs[2]


def _rwkv_sample(P, row0, B, T, sh0, s0, prm_list, nsq=16):
    rows = nsq * T
    in_specs = [pl.BlockSpec((rows, 4 * GROUP_W), lambda i: (row0 // rows + i, 0)),
                pl.BlockSpec((nsq, 1, 4 * GROUP_W), lambda i: (i, 0, 0)),
                pl.BlockSpec((nsq, 4, RW_HD, LANES), lambda i: (i, 0, 0, 0))]
    in_specs += _rw_prm_specs(prm_list, 1)
    out_specs = [pl.BlockSpec((rows, GROUP_W), lambda i: (i, 0)),
                 pl.BlockSpec((nsq, 1, 4 * GROUP_W), lambda i: (i, 0, 0)),
                 pl.BlockSpec((nsq, 4, RW_HD, LANES), lambda i: (i, 0, 0, 0))]
    out_shape = [jax.ShapeDtypeStruct((B * T, GROUP_W), BF16),
                 jax.ShapeDtypeStruct(sh0.shape, F32), jax.ShapeDtypeStruct(s0.shape, F32)]
    return pl.pallas_call(
        functools.partial(_rwkv_sample_kernel, nsq, T),
        grid=(B // nsq,),
        in_specs=in_specs,
        out_specs=out_specs,
        out_shape=out_shape,
        scratch_shapes=_rw_scratch(1, rows),
        compiler_params=_cparams(1),
        name="rwkv_sample",
    )(P, sh0, s0, *prm_list)


def _causal(L):
    return _iota((L, L), 0) >= _iota((L, L), 1)


def _ret_unit(L, get, put, S_ref, sidx, consts):
    cosv, sinv, dm_ref, qd_ref, kd_ref, gn, c_dec = consts
    for h in range(HEADS):
        q = get(h * HEAD_D, HEAD_D)
        k = get(GROUP_W + h * HEAD_D, HEAD_D)
        v = get(2 * GROUP_W + h * HEAD_D, HEAD_D)
        g = get(3 * GROUP_W + h * HEAD_D, HEAD_D)
        qr = q * cosv + pltpu.roll(q, HEAD_D // 2, 1) * sinv
        kr = (k * cosv + pltpu.roll(k, HEAD_D // 2, 1) * sinv) * (HEAD_D ** -0.5)
        S = S_ref[sidx, h]
        vb = _bf(v)
        A = _nt(_bf(qr), _bf(kr)) * dm_ref[h]
        o = _nn(_bf(A), vb) + _nn(_bf(qr * qd_ref[h]), _bf(S))
        S_ref[sidx, h] = c_dec[h] * S + _tn(_bf(kr * kd_ref[h]), vb)
        oc = o - jnp.mean(o, axis=1, keepdims=True)
        on = oc * lax.rsqrt(jnp.mean(oc * oc, axis=1, keepdims=True) + GN_EPS) * gn[:, h * HEAD_D:(h + 1) * HEAD_D]
        put(h * HEAD_D, _bf(_silu(g) * on))


def _ml_unit(L, get, put, get_sm, st_refs, sidx, consts):
    C_ref, n_ref, m_ref, cv_ref = st_refs
    ltri, ones_l, conv_w, conv_b, bias_sm, gn = consts
    x = get(0, 2 * GROUP_W)
    prev8 = cv_ref[sidx]
    conv = (conv_b + _shift_rows(x, prev8, 3) * conv_w[0:1] + _shift_rows(x, prev8, 2) * conv_w[1:2]
            + _shift_rows(x, prev8, 1) * conv_w[2:3] + x * conv_w[3:4])
    cv_ref[sidx] = x[L - SUBLANES:L]
    conv = _silu(conv)
    xs = get_sm(LANES, LANES) + bias_sm
    lf_all = _log_sigmoid(xs)
    bc_all = _dot3r(ltri, lf_all)
    causal = _causal(L)
    lane0 = _iota((L, LANES), 1) == 0
    for h in range(HEADS):
        q = conv[:, h * HEAD_D:(h + 1) * HEAD_D]
        k = conv[:, GROUP_W + h * HEAD_D:GROUP_W + (h + 1) * HEAD_D] * (HEAD_D ** -0.5)
        v = get(2 * GROUP_W + h * HEAD_D, HEAD_D)
        o_pre = get(3 * GROUP_W + h * HEAD_D, HEAD_D)
        ig = xs[:, SM_MLI - LANES + h:SM_MLI - LANES + h + 1]
        bc = bc_all[:, SM_MLF - LANES + h:SM_MLF - LANES + h + 1]
        m_prev = m_ref[sidx, h][:, 0:1]
        row = _dot3r(ones_l, jnp.where(lane0, ig - bc, 0.0), "nt")
        dlog = jnp.where(causal, bc + row, -jnp.inf)
        inter = bc + m_prev
        m_t = jnp.maximum(inter, jnp.max(dlog, axis=1, keepdims=True))
        dmat = jnp.exp(dlog - m_t)
        s_in = jnp.exp(inter - m_t)
        C = C_ref[sidx, h]
        n = n_ref[sidx, h]
        qb, vb = _bf(q), _bf(v)
        A = _nt(qb, _bf(k)) * dmat
        num = _nn(_bf(A), vb) + s_in * _nt(qb, _bf(C))
        den = jnp.sum(A, axis=1, keepdims=True) + s_in * jnp.sum(q * n, axis=1, keepdims=True)
        hh = num / jnp.maximum(jnp.abs(den), jnp.exp(-m_t))
        m_new = m_t[L - 1:L, :]
        b_last = bc[L - 1:L, :]
        carry = jnp.exp(b_last + m_prev - m_new)
        wk = jnp.exp(b_last - bc + ig - m_new) * k
        C_ref[sidx, h] = carry * C + _tn(vb, _bf(wk))
        n_ref[sidx, h] = carry * n + jnp.sum(wk, axis=0, keepdims=True)
        m_ref[sidx, h] = jnp.broadcast_to(m_new, (1, LANES))
        hc = hh - jnp.mean(hh, axis=1, keepdims=True)
        hn = hc * lax.rsqrt(jnp.mean(hc * hc, axis=1, keepdims=True) + GN_EPS) * gn[:, h * HEAD_D:(h + 1) * HEAD_D]
        put(h * HEAD_D, _bf(_sigmoid(o_pre) * hn))


def _gl_unit(L, get, put, get_sm, S_ref, sidx, consts):
    ltri, a2p, ab, gn = consts
    la = _log_sigmoid(_nn(_bf(get_sm(LANES, LANES)), a2p) + ab) / GL_TAU
    bc = _dot3r(ltri, la)
    mid = L // 2 - 1
    b_mid = bc[mid:mid + 1, :]
    b_last = bc[L - 1:L, :]
    qall = get(0, GL_DK * HEADS)
    kall = get(GL_DK * HEADS, GL_DK * HEADS)
    q_in = _bf(qall * (GL_DK ** -0.5) * jnp.exp(jnp.minimum(bc - b_mid, 80.0)))
    k_in = _bf(kall * jnp.exp(jnp.minimum(b_mid - bc, 80.0)))
    q_st = _bf(qall * (GL_DK ** -0.5) * jnp.exp(bc))
    k_st = _bf(kall * jnp.exp(b_last - bc))
    e_last = jnp.exp(b_last)
    causal = _causal(L)
    eye = _iota((GL_DK, GL_DK), 0) == _iota((GL_DK, GL_DK), 1)
    for h in range(HEADS):
        ks = slice(h * GL_DK, (h + 1) * GL_DK)
        v = get(GROUP_W + h * HEAD_D, HEAD_D)
        g = get(2 * GROUP_W + h * HEAD_D, HEAD_D)
        vb = _bf(v)
        S = S_ref[sidx, h]
        A = jnp.where(causal, _nt(q_in[:, ks], k_in[:, ks]), 0.0)
        o = _nn(_bf(A), vb) + _nn(q_st[:, ks], _bf(S))
        e_col = jnp.sum(jnp.where(eye, e_last[:, ks], 0.0), axis=1, keepdims=True)
        S_ref[sidx, h] = e_col * S + _tn(k_st[:, ks], vb)
        on = o * lax.rsqrt(jnp.mean(o * o, axis=1, keepdims=True) + GN_EPS) * gn[:, h * HEAD_D:(h + 1) * HEAD_D]
        put(h * HEAD_D, _bf(_silu(g) * on))


def _chunk_kernel(kind, nslab, nunit, L, n_state, n_const, has_sm, *refs):
    pos = 0
    p_refs = refs[pos:pos + nslab]; pos += nslab
    sm_refs = refs[pos:pos + (nslab if has_sm else 0)]; pos += (nslab if has_sm else 0)
    st_in = refs[pos:pos + n_state]; pos += n_state
    c_refs = refs[pos:pos + n_const]; pos += n_const
    o_ref = refs[pos]; pos += 1
    st_out = refs[pos:pos + n_state]; pos += n_state
    c = pl.program_id(1)

    @pl.when(c == 0)
    def _():
        for a, b in zip(st_in, st_out):
            b[...] = a[...]

    def run(slab, unit, sidx):
        r0 = unit * L if isinstance(unit, int) else pl.multiple_of(unit * L, SUBLANES)

        def get(col, width):
            return p_refs[slab][pl.ds(r0, L), col:col + width]

        def get_sm(col, width):
            return sm_refs[slab][pl.ds(r0, L), col:col + width]

        def put(col, val):
            o_ref[slab, pl.ds(r0, L), col:col + val.shape[1]] = val

        if kind == "ret":
            cosv, sinv = c_refs[0][...], c_refs[1][...]
            consts = (cosv, sinv, c_refs[2], c_refs[3], c_refs[4], c_refs[5][...], _RT_CDEC[L])
            _ret_unit(L, get, put, st_out[0], sidx, consts)
        elif kind == "ml":
            consts = tuple(r[...] for r in c_refs)
            _ml_unit(L, get, put, get_sm, st_out, sidx, consts)
        else:
            consts = tuple(r[...] for r in c_refs)
            _gl_unit(L, get, put, get_sm, st_out[0], sidx, consts)

    if nunit == 1:
        for s in range(nslab):
            run(s, 0, s)
    else:
        def body(u, carry):
            run(0, u, u)
            return carry
        lax.fori_loop(0, nunit, body, 0)


def _rt_log_gamma():
    return [math.log1p(-(2.0 ** (-5.0 - h))) for h in range(HEADS)]


_RT_CDEC = {L: [math.exp(L * lg) for lg in _rt_log_gamma()] for L in (8, CHUNK)}


def _chunk_call(kind, P, col_blk, col_w, sm, row0, nseq, T, states, consts, const_chunked, name):
    has_sm = sm
    if T % CHUNK == 0:
        L, nslab, nunit = CHUNK, nseq, 1
        grid = (1, T // L)
        rows = L
        def pmap(s, i, c):
            return ((row0 + s * T) // L + c, col_blk)
        def smap(s, i, c):
            return ((row0 + s * T) // L + c, 3)
        def omap(i, c):
            return (0, c, 0)
        sblk = nseq
    else:
        L, nslab, nunit = T, 1, 16
        grid = (nseq // nunit, 1)
        rows = L * nunit
        def pmap(s, i, c):
            return (row0 // rows + i, col_blk)
        def smap(s, i, c):
            return (row0 // rows + i, 3)
        def omap(i, c):
            return (0, i, 0)
        sblk = nunit
    in_specs = [pl.BlockSpec((rows, col_w), functools.partial(pmap, s)) for s in range(nslab)]
    args = [P] * nslab
    if has_sm:
        in_specs += [pl.BlockSpec((rows, GROUP_W), functools.partial(smap, s)) for s in range(nslab)]
        args += [P] * nslab
    st_specs = [pl.BlockSpec((sblk,) + st.shape[1:], lambda i, c, nd=st.ndim: (i,) + (0,) * (nd - 1)) for st in states]
    in_specs += st_specs
    args += list(states)
    for cst, chunked in zip(consts, const_chunked):
        if chunked:
            in_specs.append(pl.BlockSpec((None,) + cst.shape[1:], lambda i, c: (c, 0, 0)))
        else:
            in_specs.append(pl.BlockSpec(cst.shape, lambda i, c, nd=cst.ndim: (0,) * nd))
        args.append(cst)
    out_specs = [pl.BlockSpec((nslab, rows, GROUP_W), omap)] + st_specs
    out_shape = ([jax.ShapeDtypeStruct((nslab, nseq * T // nslab, GROUP_W), BF16)]
                 + [jax.ShapeDtypeStruct(st.shape, F32) for st in states])
    outs = pl.pallas_call(
        functools.partial(_chunk_kernel, kind, nslab, nunit, L, len(states), len(consts), has_sm),
        grid=grid,
        in_specs=in_specs,
        out_specs=out_specs,
        out_shape=out_shape,
        compiler_params=_cparams(2),
        name=name,
    )(*args)
    return outs[0].reshape(nseq * T, GROUP_W), outs[1:]


def _pack_w_in(w_in):
    z = jnp.zeros(w_in.shape[:2] + (GROUP_W - 184,), w_in.dtype)
    parts = [w_in[..., 0:1696], w_in[..., 3744:3752], w_in[..., 6824:6840], z,
             w_in[..., 1696:3744], w_in[..., 3752:5800], w_in[..., 5800:6824], w_in[..., 6840:7352]]
    return jnp.concatenate(parts, axis=-1).astype(BF16)


def _row(v):
    return v.reshape(1, -1).astype(F32)


def _rows_at(mat, r0, rows_total):
    z = jnp.zeros((rows_total, mat.shape[1]), F32).at[r0:r0 + mat.shape[0]].set(mat)
    return z.astype(BF16)


def _pack_rw_state(s):
    b = s.shape[0]
    return s.reshape(b, 4, 2, RW_HD, RW_HD).transpose(0, 1, 3, 2, 4).reshape(b, 4, RW_HD, LANES)


def _unpack_rw_state(s):
    b = s.shape[0]
    return s.reshape(b, 4, RW_HD, 2, RW_HD).transpose(0, 1, 3, 2, 4).reshape(b, RW_H, RW_HD, RW_HD)


def _ret_consts(pos0, T, L, gn):
    pos = pos0 + jnp.arange(T, dtype=F32)
    inv = 1.0 / (10000.0 ** jnp.linspace(0.0, 1.0, HEAD_D // 2, dtype=F32))
    ang = pos[:, None] * inv[None, :]
    cos, sin = jnp.cos(ang), jnp.sin(ang)
    cos2 = jnp.concatenate([cos, cos], axis=1).reshape(T // L, L, HEAD_D)
    sin2 = jnp.concatenate([-sin, sin], axis=1).reshape(T // L, L, HEAD_D)
    lg = jnp.log1p(-jnp.exp2(-5.0 - jnp.arange(HEADS, dtype=F32)))
    idx = jnp.arange(L, dtype=F32)
    rel = idx[:, None] - idx[None, :]
    dmat = jnp.where(rel >= 0, jnp.exp(jnp.maximum(rel, 0.0) * lg[:, None, None]), 0.0)
    q_dec = jnp.broadcast_to(jnp.exp((idx + 1.0) * lg[:, None])[..., None], (HEADS, L, HEAD_D))
    k_dec = jnp.broadcast_to(jnp.exp((L - 1.0 - idx) * lg[:, None])[..., None], (HEADS, L, HEAD_D))
    return [cos2, sin2, dmat, q_dec, k_dec, _row(gn)], [True, True, False, False, False, False]


def _tri(L):
    return jnp.tril(jnp.ones((L, L), F32)).astype(BF16)


def _moe_plan(ids, tm, n_tiles):
    e = ids.reshape(-1)
    n = e.shape[0]
    order = jnp.argsort(e, stable=True)
    sorted_e = e[order]
    counts = jnp.zeros((N_EXPERTS,), jnp.int32).at[e].add(1)
    padded = ((counts + tm - 1) // tm) * tm
    ends = jnp.cumsum(padded)
    starts = ends - padded
    raw_starts = jnp.cumsum(counts) - counts
    dest_sorted = starts[sorted_e] + jnp.arange(n, dtype=jnp.int32) - raw_starts[sorted_e]
    row_src = jnp.zeros((n_tiles * tm,), jnp.int32).at[dest_sorted].set(order // 2)
    dest = jnp.zeros((n,), jnp.int32).at[order].set(dest_sorted)
    tile_expert = jnp.minimum(jnp.searchsorted(ends, jnp.arange(n_tiles, dtype=jnp.int32) * tm, side="right"),
                              N_EXPERTS - 1).astype(jnp.int32)
    return row_src, dest.reshape(-1, 2), tile_expert


def kernel(x_prompt, x_sample, state_rw_shift, state_rw_wkv, state_ml_conv, state_ml_C, state_ml_n, state_ml_m, state_rt_S, state_gl_S, w_in, rw_mu, rw_w0, rw_w2, rw_a0, rw_a2, rw_g2, rw_kk, rw_ka, rw_rk, rw_ln_w, rw_ln_b, ml_conv_w, ml_conv_b, ml_bi, ml_bf, ml_gn_w, rt_gn_w, gl_a2, gl_ab, gl_gn_w, w_out, ln1_w, ln1_b, ln2_w, ln2_b, ffn_w1, ffn_w3, ffn_w2, moe_router, moe_w1, moe_w3, moe_w2):
    Bp, Tp = x_prompt.shape[:2]
    Bs, Ts = x_sample.shape[:2]
    Mp, Ms = Bp * Tp, Bs * Ts
    M = Mp + Ms
    depth = w_in.shape[0]
    x = jnp.concatenate([x_prompt.reshape(Mp, D_MODEL), x_sample.reshape(Ms, D_MODEL)], axis=0)
    xb = x.astype(BF16)
    w_in_p = _pack_w_in(w_in)
    w_out_b = w_out.astype(BF16)
    blockdiag = (jnp.arange(LANES)[:, None] // RW_HD) == (jnp.arange(LANES)[None, :] // RW_HD)
    g64 = blockdiag.astype(BF16)
    g64m = (blockdiag.astype(F32) / RW_HD).astype(BF16)
    pad_sh = 4 * GROUP_W - RW_COLS
    outs = {k: [[], []] for k in ("sh", "wkv", "conv", "C", "n", "m", "rt", "gl")}

    for l in range(depth):
        tm_mm = _pick(M, (1024, 512, 256, 128))
        P = _mm([xb], [w_in_p[l]], tm_mm, GROUP_W, "mm_in")

        prm = [_row(jnp.pad(rw_mu[l], (0, pad_sh))), _row(rw_w0[l]), _row(rw_a0[l]),
               _rows_at(rw_w2[l], 0, 2 * LANES), _rows_at(rw_a2[l], 32, 2 * LANES), _rows_at(rw_g2[l], 64, 2 * LANES),
               _row(rw_kk[l]), _row(rw_ka[l]), _row(rw_rk[l]), _row(rw_ln_w[l]), _row(rw_ln_b[l]), g64, g64m]
        o_rw_p, sh_p, wkv_p = _rwkv_prompt(P, 0, Bp, Tp, jnp.zeros((Bp, SUBLANES, 4 * GROUP_W), F32),
                                           jnp.zeros((Bp, 4, RW_HD, LANES), F32), prm)
        sh_in = jnp.pad(state_rw_shift[l], ((0, 0), (0, pad_sh)))[:, None, :]
        o_rw_s, sh_s, wkv_s = _rwkv_sample(P, Mp, Bs, Ts, sh_in, _pack_rw_state(state_rw_wkv[l]), prm)
        outs["sh"][0].append(sh_p[:, SUBLANES - 1, :RW_COLS])
        outs["sh"][1].append(sh_s[:, 0, :RW_COLS])
        outs["wkv"][0].append(_unpack_rw_state(wkv_p))
        outs["wkv"][1].append(_unpack_rw_state(wkv_s))

        bias_sm = jnp.zeros((LANES,), F32).at[SM_MLI - LANES:SM_MLI - LANES + HEADS].set(ml_bi[l])
        bias_sm = bias_sm.at[SM_MLF - LANES:SM_MLF - LANES + HEADS].set(ml_bf[l])

        def ml_consts(L):
            return [_tri(L), jnp.ones((L, LANES), BF16), ml_conv_w[l], _row(ml_conv_b[l]), _row(bias_sm), _row(ml_gn_w[l])]

        def ml_states(C, n, m, conv):
            b = C.shape[0]
            return [C, n[:, :, None, :], jnp.broadcast_to(m[:, :, None, None], (b, HEADS, 1, LANES)),
                    jnp.pad(conv, ((0, 0), (SUBLANES - conv.shape[1], 0), (0, 0)))]

        zp = lambda *s: jnp.zeros((Bp,) + s, F32)
        o_ml_p, st_p = _chunk_call("ml", P, 1, 4 * GROUP_W, True, 0, Bp, Tp,
                                   ml_states(zp(HEADS, HEAD_D, HEAD_D), zp(HEADS, HEAD_D), zp(HEADS), zp(3, 2 * GROUP_W)),
                                   ml_consts(CHUNK), [False] * 6, "mlstm_prompt")
        o_ml_s, st_s = _chunk_call("ml", P, 1, 4 * GROUP_W, True, Mp, Bs, Ts,
                                   ml_states(state_ml_C[l], state_ml_n[l], state_ml_m[l], state_ml_conv[l]),
                                   ml_consts(Ts), [False] * 6, "mlstm_sample")
        for g, st in enumerate((st_p, st_s)):
            outs["C"][g].append(st[0])
            outs["n"][g].append(st[1][:, :, 0, :])
            outs["m"][g].append(st[2][:, :, 0, 0])
            outs["conv"][g].append(st[3][:, SUBLANES - 3:, :])

        c_p, ch_p = _ret_consts(0.0, Tp, CHUNK, rt_gn_w[l])
        c_s, ch_s = _ret_consts(float(PAST_LEN), Ts, Ts, rt_gn_w[l])
        o_rt_p, st_p = _chunk_call("ret", P, 2, 4 * GROUP_W, False, 0, Bp, Tp, [zp(HEADS, HEAD_D, HEAD_D)], c_p, ch_p, "ret_prompt")
        o_rt_s, st_s = _chunk_call("ret", P, 2, 4 * GROUP_W, False, Mp, Bs, Ts, [state_rt_S[l]], c_s, ch_s, "ret_sample")
        outs["rt"][0].append(st_p[0])
        outs["rt"][1].append(st_s[0])

        def gl_consts(L):
            return [_tri(L), _rows_at(gl_a2[l], SM_GLA - LANES, LANES), _row(gl_ab[l]), _row(gl_gn_w[l])]

        o_gl_p, st_p = _chunk_call("gl", P, 4, 3 * GROUP_W, True, 0, Bp, Tp, [zp(HEADS, GL_DK, HEAD_D)],
                                   gl_consts(CHUNK), [False] * 4, "gla_prompt")
        o_gl_s, st_s = _chunk_call("gl", P, 4, 3 * GROUP_W, True, Mp, Bs, Ts, [state_gl_S[l]],
                                   gl_consts(Ts), [False] * 4, "gla_sample")
        outs["gl"][0].append(st_p[0])
        outs["gl"][1].append(st_s[0])

        mixed = [jnp.concatenate([a, b], axis=0) for a, b in
                 ((o_rw_p, o_rw_s), (o_ml_p, o_ml_s), (o_rt_p, o_rt_s), (o_gl_p, o_gl_s))]
        mix = _mm(mixed, [w_out_b[l, g * GROUP_W:(g + 1) * GROUP_W] for g in range(4)], tm_mm, GROUP_W, "mm_out")
        x, xb = _add_ln(x, [mix], None, ln1_w[l], ln1_b[l])

        if l % 2 == 0:
            tm = _pick(M, (512, 256, 128))
            y = _ffn(xb, jnp.zeros((M // tm,), jnp.int32), ffn_w1[l // 2][None].astype(BF16),
                     ffn_w3[l // 2][None].astype(BF16), ffn_w2[l // 2][None].astype(BF16), tm, 512)
            x, xb = _add_ln(x, [y], None, ln2_w[l], ln2_b[l])
        else:
            tm = _pick(M, (512, 256, 128))
            n_tiles = 2 * M // tm + N_EXPERTS
            gates, ids = _router(x, jnp.pad(moe_router[l // 2], ((0, 0), (0, LANES - N_EXPERTS))))
            row_src, dest, tile_expert = _moe_plan(ids[:, :2], tm, n_tiles)
            xs = jnp.take(xb, row_src, axis=0)
            ysorted = _ffn(xs, tile_expert, moe_w1[l // 2].astype(BF16), moe_w3[l // 2].astype(BF16),
                           moe_w2[l // 2].astype(BF16), tm, 256)
            ys = [jnp.take(ysorted, dest[:, k], axis=0) for k in range(2)]
            x, xb = _add_ln(x, ys, [gates[:, 0:1], gates[:, 1:2]], ln2_w[l], ln2_b[l])

    def st(key, g):
        return jnp.stack(outs[key][g])

    y_prompt = x[:Mp].reshape(Bp, Tp, D_MODEL)
    y_sample = x[Mp:].reshape(Bs, Ts, D_MODEL)
    res = [y_prompt, y_sample]
    for key in ("sh", "wkv", "conv", "C", "n", "m", "rt", "gl"):
        res += [st(key, 0), st(key, 1)]
    return tuple(res)
```

```python
import functools
import math

import jax
import jax.numpy as jnp
from jax import lax
from jax.experimental import pallas as pl
from jax.experimental.pallas import tpu as pltpu

F32 = jnp.float32
BF16 = jnp.bfloat16

D_MODEL = 2048
GROUP_W = 512
RW_COLS = 1696
RW_H, RW_HD = 8, 64
HEADS, HEAD_D = 4, 128
GL_DK = 64
CHUNK = 64
ALPHA = 4.0 ** 0.25
LN_EPS = 1e-5
GN_EPS = 1e-5
RW_GN_EPS = 64e-5
GL_TAU = 16.0
PAST_LEN = 16384
N_EXPERTS = 8

LANES = 128
SUBLANES = 8
VMEM_LIMIT = 48 * 1024 * 1024

N_PACKED = 15 * GROUP_W
SM_MLI, SM_MLF, SM_GLA = 160, 164, 168


def _nn(a, b):
    return jnp.dot(a, b, preferred_element_type=F32)


def _nt(a, b):
    return lax.dot_general(a, b, (((1,), (1,)), ((), ())), preferred_element_type=F32)


def _tn(a, b):
    return lax.dot_general(a, b, (((0,), (0,)), ((), ())), preferred_element_type=F32)


def _bf(x):
    return x.astype(BF16)


def _split3(x):
    hi = x.astype(BF16)
    r1 = x - hi.astype(F32)
    mid = r1.astype(BF16)
    lo = (r1 - mid.astype(F32)).astype(BF16)
    return hi, mid, lo


def _dot3(x, m, kind="nn"):
    f = {"nn": _nn, "nt": _nt, "tn": _tn}[kind]
    hi, mid, lo = _split3(x)
    return f(hi, m) + f(mid, m) + f(lo, m)


def _dot3r(m, x, kind="nn"):
    f = {"nn": _nn, "nt": _nt, "tn": _tn}[kind]
    hi, mid, lo = _split3(x)
    return f(m, hi) + f(m, mid) + f(m, lo)


def _sigmoid(x):
    return 1.0 / (1.0 + jnp.exp(-x))


def _softplus(x):
    return jnp.maximum(x, 0.0) + jnp.log1p(jnp.exp(-jnp.abs(x)))


def _log_sigmoid(x):
    return -_softplus(-x)


def _silu(x):
    return x * _sigmoid(x)


def _iota(shape, axis):
    return lax.broadcasted_iota(jnp.int32, shape, axis)


def _shift_rows(x, prev8, j):
    rows = x.shape[0]
    xr = pltpu.roll(x, j, 0)
    pr = pltpu.roll(prev8, j, 0)
    first = jnp.where(_iota((SUBLANES, x.shape[1]), 0) < j, pr, xr[0:SUBLANES])
    if rows == SUBLANES:
        return first
    return jnp.concatenate([first, xr[SUBLANES:]], axis=0)


def _pick(n, cands):
    return next(c for c in cands if n % c == 0)


def _cparams(n_axes):
    return pltpu.CompilerParams(dimension_semantics=("arbitrary",) * n_axes, vmem_limit_bytes=VMEM_LIMIT)


def _mm_kernel(n_in, *refs):
    xs, ws, o_ref = refs[:n_in], refs[n_in:2 * n_in], refs[2 * n_in]
    acc = _nn(xs[0][...], ws[0][...])
    for x_ref, w_ref in zip(xs[1:], ws[1:]):
        acc = acc + _nn(x_ref[...], w_ref[...])
    o_ref[...] = acc


def _mm(xs, ws, tm, tn, name):
    M, N = xs[0].shape[0], ws[0].shape[1]
    n_in = len(xs)
    in_specs = ([pl.BlockSpec((tm, x.shape[1]), lambda j, i: (i, 0)) for x in xs]
                + [pl.BlockSpec((w.shape[0], tn), lambda j, i: (0, j)) for w in ws])
    return pl.pallas_call(
        functools.partial(_mm_kernel, n_in),
        grid=(N // tn, M // tm),
        in_specs=in_specs,
        out_specs=pl.BlockSpec((tm, tn), lambda j, i: (i, j)),
        out_shape=jax.ShapeDtypeStruct((M, N), F32),
        compiler_params=_cparams(2),
        name=name,
    )(*xs, *ws)


def _add_ln_kernel(n_y, gated, x_ref, *refs):
    ys = refs[:n_y]
    gs = refs[n_y:2 * n_y] if gated else ()
    w_ref, b_ref, o_ref, ob_ref = refs[-4:]
    z = ALPHA * x_ref[...]
    for i, y_ref in enumerate(ys):
        y = y_ref[...]
        if gated:
            y = gs[i][...] * y
        z = z + y
    mu = jnp.mean(z, axis=-1, keepdims=True)
    zc = z - mu
    var = jnp.mean(zc * zc, axis=-1, keepdims=True)
    out = zc * lax.rsqrt(var + LN_EPS) * w_ref[...] + b_ref[...]
    o_ref[...] = out
    ob_ref[...] = out.astype(BF16)


def _add_ln(x, ys, gates, w, b):
    M = x.shape[0]
    tm = _pick(M, (256, 128))
    gated = gates is not None
    row = pl.BlockSpec((tm, D_MODEL), lambda i: (i, 0))
    in_specs = [row] + [row] * len(ys)
    args = [x] + list(ys)
    if gated:
        in_specs += [pl.BlockSpec((tm, 1), lambda i: (i, 0))] * len(ys)
        args += list(gates)
    in_specs += [pl.BlockSpec((1, D_MODEL), lambda i: (0, 0))] * 2
    args += [w.reshape(1, D_MODEL), b.reshape(1, D_MODEL)]
    return pl.pallas_call(
        functools.partial(_add_ln_kernel, len(ys), gated),
        grid=(M // tm,),
        in_specs=in_specs,
        out_specs=[row, row],
        out_shape=[jax.ShapeDtypeStruct((M, D_MODEL), F32), jax.ShapeDtypeStruct((M, D_MODEL), BF16)],
        compiler_params=_cparams(1),
        name="add_ln",
    )(*args)


def _ffn_kernel(te_ref, x_ref, w1_ref, w3_ref, w2_ref, o_ref):
    f = pl.program_id(1)
    x = x_ref[...]
    h = _silu(_nn(x, w1_ref[...])) * _nn(x, w3_ref[...])
    part = _nn(h.astype(BF16), w2_ref[...])

    @pl.when(f == 0)
    def _():
        o_ref[...] = part

    @pl.when(f > 0)
    def _():
        o_ref[...] += part


def _ffn(xs, tile_expert, w1, w3, w2, tm, tf):
    R = xs.shape[0]
    F = w1.shape[2]
    grid_spec = pltpu.PrefetchScalarGridSpec(
        num_scalar_prefetch=1,
        grid=(R // tm, F // tf),
        in_specs=[
            pl.BlockSpec((tm, D_MODEL), lambda i, f, te: (i, 0)),
            pl.BlockSpec((None, D_MODEL, tf), lambda i, f, te: (te[i], 0, f)),
            pl.BlockSpec((None, D_MODEL, tf), lambda i, f, te: (te[i], 0, f)),
            pl.BlockSpec((None, tf, D_MODEL), lambda i, f, te: (te[i], f, 0)),
        ],
        out_specs=pl.BlockSpec((tm, D_MODEL), lambda i, f, te: (i, 0)),
    )
    return pl.pallas_call(
        _ffn_kernel,
        grid_spec=grid_spec,
        out_shape=jax.ShapeDtypeStruct((R, D_MODEL), F32),
        compiler_params=_cparams(2),
        name="ffn",
    )(tile_expert, xs, w1, w3, w2)


def _router_kernel(x_ref, r_ref, g_ref, i_ref):
    xh, xm, xl = _split3(x_ref[...])
    rh, rm, rl = _split3(r_ref[...])
    logits = (_nn(xh, rh) + _nn(xh, rm) + _nn(xm, rh)) + (_nn(xh, rl) + _nn(xl, rh) + _nn(xm, rm))
    shape = logits.shape
    lane = _iota(shape, 1).astype(F32)
    neg = jnp.float32(-jnp.inf)
    l1 = jnp.where(lane < N_EXPERTS, logits, neg)
    m1 = jnp.max(l1, axis=1, keepdims=True)
    i1 = jnp.min(jnp.where(l1 == m1, lane, float(LANES)), axis=1, keepdims=True)
    l2 = jnp.where(lane == i1, neg, l1)
    m2 = jnp.max(l2, axis=1, keepdims=True)
    i2 = jnp.min(jnp.where(l2 == m2, lane, float(LANES)), axis=1, keepdims=True)
    e = jnp.exp(m2 - m1)
    g1 = 1.0 / (1.0 + e)
    g2 = e / (1.0 + e)
    g_ref[...] = jnp.where(lane == 0.0, g1, jnp.where(lane == 1.0, g2, 0.0))
    i_ref[...] = jnp.where(lane == 0.0, i1, jnp.where(lane == 1.0, i2, 0.0)).astype(jnp.int32)


def _router(x, router_pad):
    M = x.shape[0]
    tm = _pick(M, (512, 256, 128))
    return pl.pallas_call(
        _router_kernel,
        grid=(M // tm,),
        in_specs=[pl.BlockSpec((tm, D_MODEL), lambda i: (i, 0)),
                  pl.BlockSpec((D_MODEL, LANES), lambda i: (0, 0))],
        out_specs=[pl.BlockSpec((tm, LANES), lambda i: (i, 0))] * 2,
        out_shape=[jax.ShapeDtypeStruct((M, LANES), F32), jax.ShapeDtypeStruct((M, LANES), jnp.int32)],
        compiler_params=_cparams(1),
        name="router",
    )(x, router_pad)


def _rw_prep(pm, prm):
    (w0, a0, w2p, a2p, g2p, k_k, k_a, _, _, _, g64, _) = prm
    r = pm[:, 0:512]
    k = pm[:, 512:1024]
    v = pm[:, 1024:1536]
    sm = pm[:, 1536:1792]
    logw = -_softplus(-(w0 + _nn(_bf(jnp.tanh(sm)), w2p))) - 0.5
    lw = -jnp.exp(logw)
    a = _sigmoid(a0 + _nn(_bf(sm), a2p))
    g = _nn(_bf(_sigmoid(sm)), g2p)
    kk = k * k_k
    ss = jnp.concatenate([_dot3((kk * kk)[:, LANES * p:LANES * (p + 1)], g64) for p in range(4)], axis=1)
    kk = kk / jnp.maximum(jnp.sqrt(ss), 1e-12)
    kmod = k * (1.0 + (a - 1.0) * k_a)
    return r, lw, kmod, kk, kk * a, v, g


def _rw_post(y, r, kmod, v, g, prm):
    (_, _, _, _, _, _, _, r_k, ln_w, ln_b, g64, g64m) = prm

    def per_head(x, m):
        return jnp.concatenate([_dot3(x[:, LANES * p:LANES * (p + 1)], m) for p in range(4)], axis=1)

    yc = y - per_head(y, g64m)
    var = per_head(yc * yc, g64m)
    yn = yc * lax.rsqrt(var + RW_GN_EPS) * ln_w + ln_b
    bonus = per_head(r * kmod * r_k, g64) * v
    return (yn + bonus) * g


def _rw_chunks(L, seqs, scr, s_ref):
    q_s, rr_s, kt_s, bt_s, kp_s, bp_s, v_s, wl_s, y_s = scr
    units = [(si, r0, h) for (si, r0) in seqs for h in range(RW_H)]

    def fetch(ref, r0, h):
        return ref[pl.ds(r0, L), RW_HD * h:RW_HD * (h + 1)]

    row = _iota((2 * L, 2 * L), 0)
    col = _iota((2 * L, 2 * L), 1)
    colm = jnp.where(col >= L, col - L, col)
    mask4 = colm <= jnp.where(row < L, row - 1, row - L)
    qf = [fetch(q_s, r0, h) for (_, r0, h) in units]
    rf = [fetch(rr_s, r0, h) for (_, r0, h) in units]
    vb = [_bf(fetch(v_s, r0, h)) for (_, r0, h) in units]
    g4 = [jnp.where(mask4, _nt(_bf(jnp.concatenate([qf[i], rf[i]], axis=0)),
                               _bf(jnp.concatenate([fetch(kt_s, r0, h), fetch(bt_s, r0, h)], axis=0))), 0.0)
          for i, (_, r0, h) in enumerate(units)]
    av = [_nn(_bf(g4[i][:, :L]), vb[i]) for i in range(len(units))]
    x = [jnp.concatenate([qf[i], av[i][:L]], axis=1) for i in range(len(units))]
    pw = [g4[i][:L, L:] for i in range(len(units))]
    x = [x[i] - _nn(_bf(pw[i]), _bf(x[i])) for i in range(len(units))]
    for _ in range(int(math.log2(L)) - 1):
        pw = [_nn(_bf(m), _bf(m)) for m in pw]
        x = [x[i] + _nn(_bf(pw[i]), _bf(x[i])) for i in range(len(units))]
    ry = [jnp.concatenate([rf[i], av[i][L:]], axis=1) - _nn(_bf(g4[i][L:, L:]), _bf(x[i])) for i in range(len(units))]
    gh = [_tn(_bf(x[i]), _bf(fetch(bp_s, r0, h))) for i, (_, r0, h) in enumerate(units)]
    hk = [_tn(vb[i], _bf(fetch(kp_s, r0, h))) for i, (_, r0, h) in enumerate(units)]
    ys = []
    for i, (si, r0, h) in enumerate(units):
        S = s_ref[si, h]
        Sb = _bf(S)
        ys.append(_nt(_bf(ry[i][:, :RW_HD]), Sb) + ry[i][:, RW_HD:])
        w_last = fetch(wl_s, r0, h)[0:1, :]
        s_ref[si, h] = S * w_last - _nn(Sb, _bf(gh[i][:RW_HD])) + hk[i] - gh[i][RW_HD:]
    for i in range(0, len(units), 2):
        _, r0, h = units[i]
        y_s[pl.ds(r0, L), RW_HD * h:RW_HD * (h + 2)] = jnp.concatenate([ys[i], ys[i + 1]], axis=1)


N_RW_PRM = 13


def _rwkv_kernel(is_prompt, nsq, L, grp, *refs):
    n_p = nsq if is_prompt else 1
    p_refs = refs[:n_p]
    sh0_ref, s0_ref = refs[n_p:n_p + 2]
    pos = n_p + 2
    prm_refs = refs[pos:pos + N_RW_PRM]
    tri_ref, ones_ref, o_ref, sh_out_ref, s_out_ref = refs[pos + N_RW_PRM:pos + N_RW_PRM + 5]
    scr = refs[pos + N_RW_PRM + 5:]
    g_s, r_s, k_s = scr[9:12]
    mu = prm_refs[0][...]
    prm = tuple(r[...] for r in prm_refs[1:])
    if is_prompt:
        prev_scr = scr[12]

        @pl.when(pl.program_id(0) == 0)
        def _():
            s_out_ref[...] = s0_ref[...]
            prev_scr[...] = sh0_ref[...]

        ps, pps = [], []
        for s in range(nsq):
            ps.append(p_refs[s][...])
            pps.append(_shift_rows(ps[s], prev_scr[s], 1))
            prev_scr[s] = ps[s][L - SUBLANES:L]
        p = jnp.concatenate(ps, axis=0)
        pp = jnp.concatenate(pps, axis=0)
        sh_out_ref[...] = prev_scr[...]
    else:
        p = p_refs[0][...]
        p3 = p.reshape(nsq, L, p.shape[1])
        pp = jnp.where(_iota(p3.shape, 1) == 0, sh0_ref[...], pltpu.roll(p3, 1, 1)).reshape(p.shape)
        sh_out_ref[...] = p3[:, L - 1:L, :]
        s_out_ref[...] = s0_ref[...]

    r, lw, kmod, kk, bb, v, g = _rw_prep(p + (pp - p) * mu, prm)
    cum = _dot3r(tri_ref[...], lw)
    tot = _dot3r(ones_ref[...], lw)
    e_neg = jnp.exp(-cum)
    e_rem = jnp.exp(tot - cum)
    q_s, rr_s, kt_s, bt_s, kp_s, bp_s, v_s, wl_s, y_s = scr[:9]
    q_s[...] = kk * jnp.exp(cum - lw)
    rr_s[...] = r * jnp.exp(cum)
    kt_s[...] = kmod * e_neg
    bt_s[...] = bb * e_neg
    kp_s[...] = kmod * e_rem
    bp_s[...] = bb * e_rem
    v_s[...] = v
    wl_s[...] = jnp.exp(tot)
    g_s[...], r_s[...], k_s[...] = g, r, kmod

    if is_prompt:
        _rw_chunks(L, [(s, s * L) for s in range(nsq)], scr[:9], s_out_ref)
    else:
        def body(i, c):
            _rw_chunks(L, [(i * grp + j, pl.multiple_of((i * grp + j) * L, SUBLANES)) for j in range(grp)],
                       scr[:9], s_out_ref)
            return c
        lax.fori_loop(0, nsq // grp, body, 0)

    out = _rw_post(y_s[...], r_s[...], k_s[...], v_s[...], g_s[...], prm).astype(BF16)
    if is_prompt:
        for s in range(nsq):
            o_ref[s] = out[s * L:(s + 1) * L]
    else:
        o_ref[...] = out


def _rwkv(P, row0, B, T, sh0, s0, prm_list):
    is_prompt = T % CHUNK == 0
    wide = 4 * GROUP_W
    if is_prompt:
        L, nsq, grid = CHUNK, B, (T // CHUNK,)
        rows = nsq * L
        in_specs = [pl.BlockSpec((L, wide), functools.partial(lambda s, c: ((row0 + s * T) // L + c, 0), s))
                    for s in range(nsq)]
        in_specs += [pl.BlockSpec(sh0.shape, lambda c: (0, 0, 0)), pl.BlockSpec(s0.shape, lambda c: (0, 0, 0, 0))]
        out_specs = [pl.BlockSpec((nsq, L, GROUP_W), lambda c: (0, c, 0)),
                     pl.BlockSpec(sh0.shape, lambda c: (0, 0, 0)), pl.BlockSpec(s0.shape, lambda c: (0, 0, 0, 0))]
        out_shape = [jax.ShapeDtypeStruct((nsq, T, GROUP_W), BF16)]
        args = [P] * nsq
    else:
        L, nsq, grid = T, 16, (B // 16,)
        rows = nsq * L
        in_specs = [pl.BlockSpec((rows, wide), lambda i: (row0 // rows + i, 0)),
                    pl.BlockSpec((nsq, 1, wide), lambda i: (i, 0, 0)),
                    pl.BlockSpec((nsq, RW_H, RW_HD, RW_HD), lambda i: (i, 0, 0, 0))]
        out_specs = [pl.BlockSpec((rows, GROUP_W), lambda i: (i, 0)),
                     pl.BlockSpec((nsq, 1, wide), lambda i: (i, 0, 0)),
                     pl.BlockSpec((nsq, RW_H, RW_HD, RW_HD), lambda i: (i, 0, 0, 0))]
        out_shape = [jax.ShapeDtypeStruct((B * T, GROUP_W), BF16)]
        args = [P]
    seq_of_row = jnp.arange(rows) // L
    same = seq_of_row[:, None] == seq_of_row[None, :]
    tri = (same & (jnp.arange(rows)[:, None] >= jnp.arange(rows)[None, :])).astype(BF16)
    consts = list(prm_list) + [tri, same.astype(BF16)]
    in_specs += [pl.BlockSpec(x.shape, lambda *a: (0, 0)) for x in consts]
    out_shape += [jax.ShapeDtypeStruct(sh0.shape, F32), jax.ShapeDtypeStruct(s0.shape, F32)]
    scratch = [pltpu.VMEM((rows, GROUP_W), F32)] * 12
    if is_prompt:
        scratch.append(pltpu.VMEM(sh0.shape, F32))
    outs = pl.pallas_call(
        functools.partial(_rwkv_kernel, is_prompt, nsq, L, 2),
        grid=grid,
        in_specs=in_specs,
        out_specs=out_specs,
        out_shape=out_shape,
        scratch_shapes=scratch,
        compiler_params=_cparams(1),
        name="rwkv_prompt" if is_prompt else "rwkv_sample",
    )(*args, sh0, s0, *consts)
    return outs[0].reshape(B * T, GROUP_W), outs[1], outs[2]


def _causal(L):
    return _iota((L, L), 0) >= _iota((L, L), 1)


def _ret_unit(L, get, put, S_ref, sidx, consts):
    cosv, sinv, dm_ref, qd_ref, kd_ref, gn, c_dec = consts
    for h in range(HEADS):
        q = get(h * HEAD_D, HEAD_D)
        k = get(GROUP_W + h * HEAD_D, HEAD_D)
        v = get(2 * GROUP_W + h * HEAD_D, HEAD_D)
        g = get(3 * GROUP_W + h * HEAD_D, HEAD_D)
        qr = q * cosv + pltpu.roll(q, HEAD_D // 2, 1) * sinv
        kr = (k * cosv + pltpu.roll(k, HEAD_D // 2, 1) * sinv) * (HEAD_D ** -0.5)
        S = S_ref[sidx, h]
        vb = _bf(v)
        A = _nt(_bf(qr), _bf(kr)) * dm_ref[h]
        o = _nn(_bf(A), vb) + _nn(_bf(qr * qd_ref[h]), _bf(S))
        S_ref[sidx, h] = c_dec[h] * S + _tn(_bf(kr * kd_ref[h]), vb)
        oc = o - jnp.mean(o, axis=1, keepdims=True)
        on = oc * lax.rsqrt(jnp.mean(oc * oc, axis=1, keepdims=True) + GN_EPS) * gn[:, h * HEAD_D:(h + 1) * HEAD_D]
        put(h * HEAD_D, _bf(_silu(g) * on))


def _ml_unit(L, get, put, get_sm, st_refs, sidx, consts):
    C_ref, n_ref, m_ref, cv_ref = st_refs
    ltri, ones_l, conv_w, conv_b, bias_sm, gn = consts
    x = get(0, 2 * GROUP_W)
    prev8 = cv_ref[sidx]
    conv = (conv_b + _shift_rows(x, prev8, 3) * conv_w[0:1] + _shift_rows(x, prev8, 2) * conv_w[1:2]
            + _shift_rows(x, prev8, 1) * conv_w[2:3] + x * conv_w[3:4])
    cv_ref[sidx] = x[L - SUBLANES:L]
    conv = _silu(conv)
    xs = get_sm(LANES, LANES) + bias_sm
    lf_all = _log_sigmoid(xs)
    bc_all = _dot3r(ltri, lf_all)
    causal = _causal(L)
    lane0 = _iota((L, LANES), 1) == 0
    for h in range(HEADS):
        q = conv[:, h * HEAD_D:(h + 1) * HEAD_D]
        k = conv[:, GROUP_W + h * HEAD_D:GROUP_W + (h + 1) * HEAD_D] * (HEAD_D ** -0.5)
        v = get(2 * GROUP_W + h * HEAD_D, HEAD_D)
        o_pre = get(3 * GROUP_W + h * HEAD_D, HEAD_D)
        ig = xs[:, SM_MLI - LANES + h:SM_MLI - LANES + h + 1]
        bc = bc_all[:, SM_MLF - LANES + h:SM_MLF - LANES + h + 1]
        m_prev = m_ref[sidx, h][:, 0:1]
        row = _dot3r(ones_l, jnp.where(lane0, ig - bc, 0.0), "nt")
        dlog = jnp.where(causal, bc + row, -jnp.inf)
        inter = bc + m_prev
        m_t = jnp.maximum(inter, jnp.max(dlog, axis=1, keepdims=True))
        dmat = jnp.exp(dlog - m_t)
        s_in = jnp.exp(inter - m_t)
        C = C_ref[sidx, h]
        n = n_ref[sidx, h]
        qb, vb = _bf(q), _bf(v)
        A = _nt(qb, _bf(k)) * dmat
        num = _nn(_bf(A), vb) + s_in * _nt(qb, _bf(C))
        den = jnp.sum(A, axis=1, keepdims=True) + s_in * jnp.sum(q * n, axis=1, keepdims=True)
        hh = num / jnp.maximum(jnp.abs(den), jnp.exp(-m_t))
        m_new = m_t[L - 1:L, :]
        b_last = bc[L - 1:L, :]
        carry = jnp.exp(b_last + m_prev - m_new)
        wk = jnp.exp(b_last - bc + ig - m_new) * k
        C_ref[sidx, h] = carry * C + _tn(vb, _bf(wk))
        n_ref[sidx, h] = carry * n + jnp.sum(wk, axis=0, keepdims=True)
        m_ref[sidx, h] = jnp.broadcast_to(m_new, (1, LANES))
        hc = hh - jnp.mean(hh, axis=1, keepdims=True)
        hn = hc * lax.rsqrt(jnp.mean(hc * hc, axis=1, keepdims=True) + GN_EPS) * gn[:, h * HEAD_D:(h + 1) * HEAD_D]
        put(h * HEAD_D, _bf(_sigmoid(o_pre) * hn))


def _gl_unit(L, get, put, get_sm, S_ref, sidx, consts):
    ltri, a2p, ab, gn = consts
    la = _log_sigmoid(_nn(_bf(get_sm(LANES, LANES)), a2p) + ab) / GL_TAU
    bc = _dot3r(ltri, la)
    mid = L // 2 - 1
    b_mid = bc[mid:mid + 1, :]
    b_last = bc[L - 1:L, :]
    qall = get(0, GL_DK * HEADS)
    kall = get(GL_DK * HEADS, GL_DK * HEADS)
    q_in = _bf(qall * (GL_DK ** -0.5) * jnp.exp(jnp.minimum(bc - b_mid, 80.0)))
    k_in = _bf(kall * jnp.exp(jnp.minimum(b_mid - bc, 80.0)))
    q_st = _bf(qall * (GL_DK ** -0.5) * jnp.exp(bc))
    k_st = _bf(kall * jnp.exp(b_last - bc))
    e_last = jnp.exp(b_last)
    causal = _causal(L)
    eye = _iota((GL_DK, GL_DK), 0) == _iota((GL_DK, GL_DK), 1)
    for h in range(HEADS):
        ks = slice(h * GL_DK, (h + 1) * GL_DK)
        v = get(GROUP_W + h * HEAD_D, HEAD_D)
        g = get(2 * GROUP_W + h * HEAD_D, HEAD_D)
        vb = _bf(v)
        S = S_ref[sidx, h]
        A = jnp.where(causal, _nt(q_in[:, ks], k_in[:, ks]), 0.0)
        o = _nn(_bf(A), vb) + _nn(q_st[:, ks], _bf(S))
        e_col = jnp.sum(jnp.where(eye, e_last[:, ks], 0.0), axis=1, keepdims=True)
        S_ref[sidx, h] = e_col * S + _tn(k_st[:, ks], vb)
        on = o * lax.rsqrt(jnp.mean(o * o, axis=1, keepdims=True) + GN_EPS) * gn[:, h * HEAD_D:(h + 1) * HEAD_D]
        put(h * HEAD_D, _bf(_silu(g) * on))


def _chunk_kernel(kind, nslab, nunit, L, n_state, n_const, has_sm, *refs):
    pos = 0
    p_refs = refs[pos:pos + nslab]; pos += nslab
    sm_refs = refs[pos:pos + (nslab if has_sm else 0)]; pos += (nslab if has_sm else 0)
    st_in = refs[pos:pos + n_state]; pos += n_state
    c_refs = refs[pos:pos + n_const]; pos += n_const
    o_ref = refs[pos]; pos += 1
    st_out = refs[pos:pos + n_state]; pos += n_state
    c = pl.program_id(1)

    @pl.when(c == 0)
    def _():
        for a, b in zip(st_in, st_out):
            b[...] = a[...]

    def run(slab, unit, sidx):
        r0 = unit * L if isinstance(unit, int) else pl.multiple_of(unit * L, SUBLANES)

        def get(col, width):
            return p_refs[slab][pl.ds(r0, L), col:col + width]

        def get_sm(col, width):
            return sm_refs[slab][pl.ds(r0, L), col:col + width]

        def put(col, val):
            o_ref[slab, pl.ds(r0, L), col:col + val.shape[1]] = val

        if kind == "ret":
            cosv, sinv = c_refs[0][...], c_refs[1][...]
            consts = (cosv, sinv, c_refs[2], c_refs[3], c_refs[4], c_refs[5][...], _RT_CDEC[L])
            _ret_unit(L, get, put, st_out[0], sidx, consts)
        elif kind == "ml":
            consts = tuple(r[...] for r in c_refs)
            _ml_unit(L, get, put, get_sm, st_out, sidx, consts)
        else:
            consts = tuple(r[...] for r in c_refs)
            _gl_unit(L, get, put, get_sm, st_out[0], sidx, consts)

    if nunit == 1:
        for s in range(nslab):
            run(s, 0, s)
    else:
        def body(u, carry):
            run(0, u, u)
            return carry
        lax.fori_loop(0, nunit, body, 0)


def _rt_log_gamma():
    return [math.log1p(-(2.0 ** (-5.0 - h))) for h in range(HEADS)]


_RT_CDEC = {L: [math.exp(L * lg) for lg in _rt_log_gamma()] for L in (8, CHUNK)}


def _chunk_call(kind, P, col_blk, col_w, sm, row0, nseq, T, states, consts, const_chunked, name):
    has_sm = sm
    if T % CHUNK == 0:
        L, nslab, nunit = CHUNK, nseq, 1
        grid = (1, T // L)
        rows = L
        def pmap(s, i, c):
            return ((row0 + s * T) // L + c, col_blk)
        def smap(s, i, c):
            return ((row0 + s * T) // L + c, 3)
        def omap(i, c):
            return (0, c, 0)
        sblk = nseq
    else:
        L, nslab, nunit = T, 1, 16
        grid = (nseq // nunit, 1)
        rows = L * nunit
        def pmap(s, i, c):
            return (row0 // rows + i, col_blk)
        def smap(s, i, c):
            return (row0 // rows + i, 3)
        def omap(i, c):
            return (0, i, 0)
        sblk = nunit
    in_specs = [pl.BlockSpec((rows, col_w), functools.partial(pmap, s)) for s in range(nslab)]
    args = [P] * nslab
    if has_sm:
        in_specs += [pl.BlockSpec((rows, GROUP_W), functools.partial(smap, s)) for s in range(nslab)]
        args += [P] * nslab
    st_specs = [pl.BlockSpec((sblk,) + st.shape[1:], lambda i, c, nd=st.ndim: (i,) + (0,) * (nd - 1)) for st in states]
    in_specs += st_specs
    args += list(states)
    for cst, chunked in zip(consts, const_chunked):
        if chunked:
            in_specs.append(pl.BlockSpec((None,) + cst.shape[1:], lambda i, c: (c, 0, 0)))
        else:
            in_specs.append(pl.BlockSpec(cst.shape, lambda i, c, nd=cst.ndim: (0,) * nd))
        args.append(cst)
    out_specs = [pl.BlockSpec((nslab, rows, GROUP_W), omap)] + st_specs
    out_shape = ([jax.ShapeDtypeStruct((nslab, nseq * T // nslab, GROUP_W), BF16)]
                 + [jax.ShapeDtypeStruct(st.shape, F32) for st in states])
    outs = pl.pallas_call(
        functools.partial(_chunk_kernel, kind, nslab, nunit, L, len(states), len(consts), has_sm),
        grid=grid,
        in_specs=in_specs,
        out_specs=out_specs,
        out_shape=out_shape,
        compiler_params=_cparams(2),
        name=name,
    )(*args)
    return outs[0].reshape(nseq * T, GROUP_W), outs[1:]


def _pack_w_in(w_in):
    z = jnp.zeros(w_in.shape[:2] + (GROUP_W - 184,), w_in.dtype)
    parts = [w_in[..., 0:1696], w_in[..., 3744:3752], w_in[..., 6824:6840], z,
             w_in[..., 1696:3744], w_in[..., 3752:5800], w_in[..., 5800:6824], w_in[..., 6840:7352]]
    return jnp.concatenate(parts, axis=-1).astype(BF16)


def _row(v):
    return v.reshape(1, -1).astype(F32)


def _rows_at(mat, r0, rows_total):
    z = jnp.zeros((rows_total, mat.shape[1]), F32).at[r0:r0 + mat.shape[0]].set(mat)
    return z.astype(BF16)


def _ret_consts(pos0, T, L, gn):
    pos = pos0 + jnp.arange(T, dtype=F32)
    inv = 1.0 / (10000.0 ** jnp.linspace(0.0, 1.0, HEAD_D // 2, dtype=F32))
    ang = pos[:, None] * inv[None, :]
    cos, sin = jnp.cos(ang), jnp.sin(ang)
    cos2 = jnp.concatenate([cos, cos], axis=1).reshape(T // L, L, HEAD_D)
    sin2 = jnp.concatenate([-sin, sin], axis=1).reshape(T // L, L, HEAD_D)
    lg = jnp.log1p(-jnp.exp2(-5.0 - jnp.arange(HEADS, dtype=F32)))
    idx = jnp.arange(L, dtype=F32)
    rel = idx[:, None] - idx[None, :]
    dmat = jnp.where(rel >= 0, jnp.exp(jnp.maximum(rel, 0.0) * lg[:, None, None]), 0.0)
    q_dec = jnp.broadcast_to(jnp.exp((idx + 1.0) * lg[:, None])[..., None], (HEADS, L, HEAD_D))
    k_dec = jnp.broadcast_to(jnp.exp((L - 1.0 - idx) * lg[:, None])[..., None], (HEADS, L, HEAD_D))
    return [cos2, sin2, dmat, q_dec, k_dec, _row(gn)], [True, True, False, False, False, False]


def _tri(L):
    return jnp.tril(jnp.ones((L, L), F32)).astype(BF16)


def _moe_plan(ids, tm, n_tiles):
    e = ids.reshape(-1)
    n = e.shape[0]
    onehot = (e[:, None] == jnp.arange(N_EXPERTS, dtype=jnp.int32)[None, :]).astype(jnp.int32)
    rank = jnp.sum((jnp.cumsum(onehot, axis=0) - 1) * onehot, axis=1)
    counts = jnp.sum(onehot, axis=0)
    padded = ((counts + tm - 1) // tm) * tm
    ends = jnp.cumsum(padded)
    starts = ends - padded
    dest = jnp.sum(starts[None, :] * onehot, axis=1) + rank
    row_src = jnp.zeros((n_tiles * tm,), jnp.int32).at[dest].set(jnp.arange(n, dtype=jnp.int32) // 2)
    tile_start = jnp.arange(n_tiles, dtype=jnp.int32) * tm
    tile_expert = jnp.minimum(jnp.sum((tile_start[:, None] >= ends[None, :]).astype(jnp.int32), axis=1),
                              N_EXPERTS - 1)
    return row_src, dest.reshape(-1, 2), tile_expert


def kernel(x_prompt, x_sample, state_rw_shift, state_rw_wkv, state_ml_conv, state_ml_C, state_ml_n, state_ml_m, state_rt_S, state_gl_S, w_in, rw_mu, rw_w0, rw_w2, rw_a0, rw_a2, rw_g2, rw_kk, rw_ka, rw_rk, rw_ln_w, rw_ln_b, ml_conv_w, ml_conv_b, ml_bi, ml_bf, ml_gn_w, rt_gn_w, gl_a2, gl_ab, gl_gn_w, w_out, ln1_w, ln1_b, ln2_w, ln2_b, ffn_w1, ffn_w3, ffn_w2, moe_router, moe_w1, moe_w3, moe_w2):
    Bp, Tp = x_prompt.shape[:2]
    Bs, Ts = x_sample.shape[:2]
    Mp, Ms = Bp * Tp, Bs * Ts
    M = Mp + Ms
    depth = w_in.shape[0]
    x = jnp.concatenate([x_prompt.reshape(Mp, D_MODEL), x_sample.reshape(Ms, D_MODEL)], axis=0)
    xb = x.astype(BF16)
    w_in_p = _pack_w_in(w_in)
    w_out_b = w_out.astype(BF16)
    blockdiag = (jnp.arange(LANES)[:, None] // RW_HD) == (jnp.arange(LANES)[None, :] // RW_HD)
    g64 = blockdiag.astype(BF16)
    g64m = (blockdiag.astype(F32) / RW_HD).astype(BF16)
    pad_sh = 4 * GROUP_W - RW_COLS
    outs = {k: [[], []] for k in ("sh", "wkv", "conv", "C", "n", "m", "rt", "gl")}

    for l in range(depth):
        tm_mm = _pick(M, (1024, 512, 256, 128))
        P = _mm([xb], [w_in_p[l]], tm_mm, GROUP_W, "mm_in")

        prm = [_row(jnp.pad(rw_mu[l], (0, pad_sh))), _row(rw_w0[l]), _row(rw_a0[l]),
               _rows_at(rw_w2[l], 0, 2 * LANES), _rows_at(rw_a2[l], 32, 2 * LANES), _rows_at(rw_g2[l], 64, 2 * LANES),
               _row(rw_kk[l]), _row(rw_ka[l]), _row(rw_rk[l]), _row(rw_ln_w[l]), _row(rw_ln_b[l]), g64, g64m]
        o_rw_p, sh_p, wkv_p = _rwkv(P, 0, Bp, Tp, jnp.zeros((Bp, SUBLANES, 4 * GROUP_W), F32),
                                    jnp.zeros((Bp, RW_H, RW_HD, RW_HD), F32), prm)
        sh_in = jnp.pad(state_rw_shift[l], ((0, 0), (0, pad_sh)))[:, None, :]
        o_rw_s, sh_s, wkv_s = _rwkv(P, Mp, Bs, Ts, sh_in, state_rw_wkv[l], prm)
        outs["sh"][0].append(sh_p[:, SUBLANES - 1, :RW_COLS])
        outs["sh"][1].append(sh_s[:, 0, :RW_COLS])
        outs["wkv"][0].append(wkv_p)
        outs["wkv"][1].append(wkv_s)

        bias_sm = jnp.zeros((LANES,), F32).at[SM_MLI - LANES:SM_MLI - LANES + HEADS].set(ml_bi[l])
        bias_sm = bias_sm.at[SM_MLF - LANES:SM_MLF - LANES + HEADS].set(ml_bf[l])

        def ml_consts(L):
            return [_tri(L), jnp.ones((L, LANES), BF16), ml_conv_w[l], _row(ml_conv_b[l]), _row(bias_sm), _row(ml_gn_w[l])]

        def ml_states(C, n, m, conv):
            b = C.shape[0]
            return [C, n[:, :, None, :], jnp.broadcast_to(m[:, :, None, None], (b, HEADS, 1, LANES)),
                    jnp.pad(conv, ((0, 0), (SUBLANES - conv.shape[1], 0), (0, 0)))]

        zp = lambda *s: jnp.zeros((Bp,) + s, F32)
        o_ml_p, st_p = _chunk_call("ml", P, 1, 4 * GROUP_W, True, 0, Bp, Tp,
                                   ml_states(zp(HEADS, HEAD_D, HEAD_D), zp(HEADS, HEAD_D), zp(HEADS), zp(3, 2 * GROUP_W)),
                                   ml_consts(CHUNK), [False] * 6, "mlstm_prompt")
        o_ml_s, st_s = _chunk_call("ml", P, 1, 4 * GROUP_W, True, Mp, Bs, Ts,
                                   ml_states(state_ml_C[l], state_ml_n[l], state_ml_m[l], state_ml_conv[l]),
                                   ml_consts(Ts), [False] * 6, "mlstm_sample")
        for g, st in enumerate((st_p, st_s)):
            outs["C"][g].append(st[0])
            outs["n"][g].append(st[1][:, :, 0, :])
            outs["m"][g].append(st[2][:, :, 0, 0])
            outs["conv"][g].append(st[3][:, SUBLANES - 3:, :])

        c_p, ch_p = _ret_consts(0.0, Tp, CHUNK, rt_gn_w[l])
        c_s, ch_s = _ret_consts(float(PAST_LEN), Ts, Ts, rt_gn_w[l])
        o_rt_p, st_p = _chunk_call("ret", P, 2, 4 * GROUP_W, False, 0, Bp, Tp, [zp(HEADS, HEAD_D, HEAD_D)], c_p, ch_p, "ret_prompt")
        o_rt_s, st_s = _chunk_call("ret", P, 2, 4 * GROUP_W, False, Mp, Bs, Ts, [state_rt_S[l]], c_s, ch_s, "ret_sample")
        outs["rt"][0].append(st_p[0])
        outs["rt"][1].append(st_s[0])

        def gl_consts(L):
            return [_tri(L), _rows_at(gl_a2[l], SM_GLA - LANES, LANES), _row(gl_ab[l]), _row(gl_gn_w[l])]

        o_gl_p, st_p = _chunk_call("gl", P, 4, 3 * GROUP_W, True, 0, Bp, Tp, [zp(HEADS, GL_DK, HEAD_D)],
                                   gl_consts(CHUNK), [False] * 4, "gla_prompt")
        o_gl_s, st_s = _chunk_call("gl", P, 4, 3 * GROUP_W, True, Mp, Bs, Ts, [state_gl_S[l]],
                                   gl_consts(Ts), [False] * 4, "gla_sample")
        outs["gl"][0].append(st_p[0])
        outs["gl"][1].append(st_s[0])

        mixed = [jnp.concatenate([a, b], axis=0) for a, b in
                 ((o_rw_p, o_rw_s), (o_ml_p, o_ml_s), (o_rt_p, o_rt_s), (o_gl_p, o_gl_s))]
        mix = _mm(mixed, [w_out_b[l, g * GROUP_W:(g + 1) * GROUP_W] for g in range(4)], tm_mm, GROUP_W, "mm_out")
        x, xb = _add_ln(x, [mix], None, ln1_w[l], ln1_b[l])

        if l % 2 == 0:
            tm = _pick(M, (512, 256, 128))
            y = _ffn(xb, jnp.zeros((M // tm,), jnp.int32), ffn_w1[l // 2][None].astype(BF16),
                     ffn_w3[l // 2][None].astype(BF16), ffn_w2[l // 2][None].astype(BF16), tm, 512)
            x, xb = _add_ln(x, [y], None, ln2_w[l], ln2_b[l])
        else:
            tm = _pick(M, (512, 256, 128))
            n_tiles = 2 * M // tm + N_EXPERTS
            gates, ids = _router(x, jnp.pad(moe_router[l // 2], ((0, 0), (0, LANES - N_EXPERTS))))
            row_src, dest, tile_expert = _moe_plan(ids[:, :2], tm, n_tiles)
            xs = jnp.take(xb, row_src, axis=0)
            ysorted = _ffn(xs, tile_expert, moe_w1[l // 2].astype(BF16), moe_w3[l // 2].astype(BF16),
                           moe_w2[l // 2].astype(BF16), tm, 256)
            ys = [jnp.take(ysorted, dest[:, k], axis=0) for k in range(2)]
            x, xb = _add_ln(x, ys, [gates[:, 0:1], gates[:, 1:2]], ln2_w[l], ln2_b[l])

    def st(key, g):
        return jnp.stack(outs[key][g])

    y_prompt = x[:Mp].reshape(Bp, Tp, D_MODEL)
    y_sample = x[Mp:].reshape(Bs, Ts, D_MODEL)
    res = [y_prompt, y_sample]
    for key in ("sh", "wkv", "conv", "C", "n", "m", "rt", "gl"):
        res += [st(key, 0), st(key, 1)]
    return tuple(res)
```

```python
import functools
import math

import jax
import jax.numpy as jnp
from jax import lax
from jax.experimental import pallas as pl
from jax.experimental.pallas import tpu as pltpu

F32 = jnp.float32
BF16 = jnp.bfloat16

D_MODEL = 2048
GROUP_W = 512
RW_COLS = 1696
RW_H, RW_HD = 8, 64
HEADS, HEAD_D = 4, 128
GL_DK = 64
CHUNK = 64
ALPHA = 4.0 ** 0.25
LN_EPS = 1e-5
GN_EPS = 1e-5
RW_GN_EPS = 64e-5
GL_TAU = 16.0
PAST_LEN = 16384
N_EXPERTS = 8

LANES = 128
SUBLANES = 8
VMEM_LIMIT = 48 * 1024 * 1024

N_PACKED = 15 * GROUP_W
SM_MLI, SM_MLF, SM_GLA = 160, 164, 168


def _nn(a, b):
    return jnp.dot(a, b, preferred_element_type=F32)


def _nt(a, b):
    return lax.dot_general(a, b, (((1,), (1,)), ((), ())), preferred_element_type=F32)


def _tn(a, b):
    return lax.dot_general(a, b, (((0,), (0,)), ((), ())), preferred_element_type=F32)


def _bf(x):
    return x.astype(BF16)


def _split3(x):
    hi = x.astype(BF16)
    r1 = x - hi.astype(F32)
    mid = r1.astype(BF16)
    lo = (r1 - mid.astype(F32)).astype(BF16)
    return hi, mid, lo


def _dot3(x, m, kind="nn"):
    f = {"nn": _nn, "nt": _nt, "tn": _tn}[kind]
    hi, mid, lo = _split3(x)
    return f(hi, m) + f(mid, m) + f(lo, m)


def _dot3r(m, x, kind="nn"):
    f = {"nn": _nn, "nt": _nt, "tn": _tn}[kind]
    hi, mid, lo = _split3(x)
    return f(m, hi) + f(m, mid) + f(m, lo)


def _sigmoid(x):
    return 1.0 / (1.0 + jnp.exp(-x))


def _softplus(x):
    return jnp.maximum(x, 0.0) + jnp.log1p(jnp.exp(-jnp.abs(x)))


def _log_sigmoid(x):
    return -_softplus(-x)


def _silu(x):
    return x * _sigmoid(x)


def _iota(shape, axis):
    return lax.broadcasted_iota(jnp.int32, shape, axis)


def _shift_rows(x, prev8, j):
    rows = x.shape[0]
    xr = pltpu.roll(x, j, 0)
    pr = pltpu.roll(prev8, j, 0)
    first = jnp.where(_iota((SUBLANES, x.shape[1]), 0) < j, pr, xr[0:SUBLANES])
    if rows == SUBLANES:
        return first
    return jnp.concatenate([first, xr[SUBLANES:]], axis=0)


def _pick(n, cands):
    return next(c for c in cands if n % c == 0)


def _cparams(n_axes):
    return pltpu.CompilerParams(dimension_semantics=("arbitrary",) * n_axes, vmem_limit_bytes=VMEM_LIMIT)


def _mm_kernel(n_in, *refs):
    xs, ws, o_ref = refs[:n_in], refs[n_in:2 * n_in], refs[2 * n_in]
    acc = _nn(xs[0][...], ws[0][...])
    for x_ref, w_ref in zip(xs[1:], ws[1:]):
        acc = acc + _nn(x_ref[...], w_ref[...])
    o_ref[...] = acc


def _mm(xs, ws, tm, tn, name):
    M, N = xs[0].shape[0], ws[0].shape[1]
    n_in = len(xs)
    in_specs = ([pl.BlockSpec((tm, x.shape[1]), lambda j, i: (i, 0)) for x in xs]
                + [pl.BlockSpec((w.shape[0], tn), lambda j, i: (0, j)) for w in ws])
    return pl.pallas_call(
        functools.partial(_mm_kernel, n_in),
        grid=(N // tn, M // tm),
        in_specs=in_specs,
        out_specs=pl.BlockSpec((tm, tn), lambda j, i: (i, j)),
        out_shape=jax.ShapeDtypeStruct((M, N), F32),
        compiler_params=_cparams(2),
        name=name,
    )(*xs, *ws)


def _add_ln_kernel(n_y, gated, x_ref, *refs):
    ys = refs[:n_y]
    gs = refs[n_y:2 * n_y] if gated else ()
    w_ref, b_ref, o_ref, ob_ref = refs[-4:]
    z = ALPHA * x_ref[...]
    for i, y_ref in enumerate(ys):
        y = y_ref[...]
        if gated:
            y = gs[i][...] * y
        z = z + y
    mu = jnp.mean(z, axis=-1, keepdims=True)
    zc = z - mu
    var = jnp.mean(zc * zc, axis=-1, keepdims=True)
    out = zc * lax.rsqrt(var + LN_EPS) * w_ref[...] + b_ref[...]
    o_ref[...] = out
    ob_ref[...] = out.astype(BF16)


def _add_ln(x, ys, gates, w, b):
    M = x.shape[0]
    tm = _pick(M, (256, 128))
    gated = gates is not None
    row = pl.BlockSpec((tm, D_MODEL), lambda i: (i, 0))
    in_specs = [row] + [row] * len(ys)
    args = [x] + list(ys)
    if gated:
        in_specs += [pl.BlockSpec((tm, 1), lambda i: (i, 0))] * len(ys)
        args += list(gates)
    in_specs += [pl.BlockSpec((1, D_MODEL), lambda i: (0, 0))] * 2
    args += [w.reshape(1, D_MODEL), b.reshape(1, D_MODEL)]
    return pl.pallas_call(
        functools.partial(_add_ln_kernel, len(ys), gated),
        grid=(M // tm,),
        in_specs=in_specs,
        out_specs=[row, row],
        out_shape=[jax.ShapeDtypeStruct((M, D_MODEL), F32), jax.ShapeDtypeStruct((M, D_MODEL), BF16)],
        compiler_params=_cparams(1),
        name="add_ln",
    )(*args)


def _ffn_up_kernel(te_ref, nu_ref, x_ref, w1_ref, w3_ref, h_ref):
    @pl.when(pl.program_id(0) < nu_ref[0])
    def _():
        x = x_ref[...]
        h_ref[...] = (_silu(_nn(x, w1_ref[...])) * _nn(x, w3_ref[...])).astype(BF16)


def _ffn_down_kernel(te_ref, nu_ref, h_ref, w2_ref, o_ref):
    @pl.when(pl.program_id(0) < nu_ref[0])
    def _():
        o_ref[...] = _nn(h_ref[...], w2_ref[...])


FFN_TF = 1408
FFN_TN = 512


def _ffn(xs, tile_expert, n_used, w1, w3, w2, tm):
    R = xs.shape[0]
    F = w1.shape[2]
    nf, nn = F // FFN_TF, D_MODEL // FFN_TN

    def last(i, j, nu, n):
        return jnp.where(i < nu[0], j, n - 1)

    up_spec = pltpu.PrefetchScalarGridSpec(
        num_scalar_prefetch=2,
        grid=(R // tm, nf),
        in_specs=[
            pl.BlockSpec((tm, D_MODEL), lambda i, f, te, nu: (i, 0)),
            pl.BlockSpec((None, D_MODEL, FFN_TF), lambda i, f, te, nu: (te[i], 0, last(i, f, nu, nf))),
            pl.BlockSpec((None, D_MODEL, FFN_TF), lambda i, f, te, nu: (te[i], 0, last(i, f, nu, nf))),
        ],
        out_specs=pl.BlockSpec((tm, FFN_TF), lambda i, f, te, nu: (i, f)),
    )
    h = pl.pallas_call(
        _ffn_up_kernel,
        grid_spec=up_spec,
        out_shape=jax.ShapeDtypeStruct((R, F), BF16),
        compiler_params=_cparams(2),
        name="ffn_up",
    )(tile_expert, n_used, xs, w1, w3)
    down_spec = pltpu.PrefetchScalarGridSpec(
        num_scalar_prefetch=2,
        grid=(R // tm, nn),
        in_specs=[
            pl.BlockSpec((tm, F), lambda i, n, te, nu: (i, 0)),
            pl.BlockSpec((None, F, FFN_TN), lambda i, n, te, nu: (te[i], 0, last(i, n, nu, nn))),
        ],
        out_specs=pl.BlockSpec((tm, FFN_TN), lambda i, n, te, nu: (i, n)),
    )
    return pl.pallas_call(
        _ffn_down_kernel,
        grid_spec=down_spec,
        out_shape=jax.ShapeDtypeStruct((R, D_MODEL), F32),
        compiler_params=_cparams(2),
        name="ffn_down",
    )(tile_expert, n_used, h, w2)


def _router_kernel(x_ref, r_ref, g_ref, i_ref):
    xh, xm, xl = _split3(x_ref[...])
    rh, rm, rl = _split3(r_ref[...])
    logits = (_nn(xh, rh) + _nn(xh, rm) + _nn(xm, rh)) + (_nn(xh, rl) + _nn(xl, rh) + _nn(xm, rm))
    shape = logits.shape
    lane = _iota(shape, 1).astype(F32)
    neg = jnp.float32(-jnp.inf)
    l1 = jnp.where(lane < N_EXPERTS, logits, neg)
    m1 = jnp.max(l1, axis=1, keepdims=True)
    i1 = jnp.min(jnp.where(l1 == m1, lane, float(LANES)), axis=1, keepdims=True)
    l2 = jnp.where(lane == i1, neg, l1)
    m2 = jnp.max(l2, axis=1, keepdims=True)
    i2 = jnp.min(jnp.where(l2 == m2, lane, float(LANES)), axis=1, keepdims=True)
    e = jnp.exp(m2 - m1)
    g1 = 1.0 / (1.0 + e)
    g2 = e / (1.0 + e)
    g_ref[...] = jnp.where(lane == 0.0, g1, jnp.where(lane == 1.0, g2, 0.0))
    i_ref[...] = jnp.where(lane == 0.0, i1, jnp.where(lane == 1.0, i2, 0.0)).astype(jnp.int32)


def _router(x, router_pad):
    M = x.shape[0]
    tm = _pick(M, (512, 256, 128))
    return pl.pallas_call(
        _router_kernel,
        grid=(M // tm,),
        in_specs=[pl.BlockSpec((tm, D_MODEL), lambda i: (i, 0)),
                  pl.BlockSpec((D_MODEL, LANES), lambda i: (0, 0))],
        out_specs=[pl.BlockSpec((tm, LANES), lambda i: (i, 0))] * 2,
        out_shape=[jax.ShapeDtypeStruct((M, LANES), F32), jax.ShapeDtypeStruct((M, LANES), jnp.int32)],
        compiler_params=_cparams(1),
        name="router",
    )(x, router_pad)


def _rw_prep(pm, prm):
    (w0, a0, w2p, a2p, g2p, k_k, k_a, _, _, _, g64, _) = prm
    r = pm[:, 0:512]
    k = pm[:, 512:1024]
    v = pm[:, 1024:1536]
    sm = pm[:, 1536:1792]
    logw = -_softplus(-(w0 + _nn(_bf(jnp.tanh(sm)), w2p))) - 0.5
    lw = -jnp.exp(logw)
    a = _sigmoid(a0 + _nn(_bf(sm), a2p))
    g = _nn(_bf(_sigmoid(sm)), g2p)
    kk = k * k_k
    ss = jnp.concatenate([_dot3((kk * kk)[:, LANES * p:LANES * (p + 1)], g64) for p in range(4)], axis=1)
    kk = kk / jnp.maximum(jnp.sqrt(ss), 1e-12)
    kmod = k * (1.0 + (a - 1.0) * k_a)
    return r, lw, kmod, kk, kk * a, v, g


def _rw_post(y, r, kmod, v, g, prm):
    (_, _, _, _, _, _, _, r_k, ln_w, ln_b, g64, g64m) = prm

    def per_head(x, m):
        return jnp.concatenate([_dot3(x[:, LANES * p:LANES * (p + 1)], m) for p in range(4)], axis=1)

    yc = y - per_head(y, g64m)
    var = per_head(yc * yc, g64m)
    yn = yc * lax.rsqrt(var + RW_GN_EPS) * ln_w + ln_b
    bonus = per_head(r * kmod * r_k, g64) * v
    return (yn + bonus) * g


def _rw_chunks(L, seqs, scr, s_ref):
    q_s, rr_s, kt_s, bt_s, kp_s, bp_s, v_s, wl_s, y_s = scr
    units = [(si, r0, h) for (si, r0) in seqs for h in range(RW_H)]

    def fetch(ref, r0, h):
        return ref[pl.ds(r0, L), RW_HD * h:RW_HD * (h + 1)]

    row = _iota((2 * L, 2 * L), 0)
    col = _iota((2 * L, 2 * L), 1)
    colm = jnp.where(col >= L, col - L, col)
    mask4 = colm <= jnp.where(row < L, row - 1, row - L)
    qf = [fetch(q_s, r0, h) for (_, r0, h) in units]
    rf = [fetch(rr_s, r0, h) for (_, r0, h) in units]
    vb = [_bf(fetch(v_s, r0, h)) for (_, r0, h) in units]
    g4 = [jnp.where(mask4, _nt(_bf(jnp.concatenate([qf[i], rf[i]], axis=0)),
                               _bf(jnp.concatenate([fetch(kt_s, r0, h), fetch(bt_s, r0, h)], axis=0))), 0.0)
          for i, (_, r0, h) in enumerate(units)]
    av = [_nn(_bf(g4[i][:, :L]), vb[i]) for i in range(len(units))]
    x = [jnp.concatenate([qf[i], av[i][:L]], axis=1) for i in range(len(units))]
    pw = [g4[i][:L, L:] for i in range(len(units))]
    x = [x[i] - _nn(_bf(pw[i]), _bf(x[i])) for i in range(len(units))]
    for _ in range(int(math.log2(L)) - 1):
        pw = [_nn(_bf(m), _bf(m)) for m in pw]
        x = [x[i] + _nn(_bf(pw[i]), _bf(x[i])) for i in range(len(units))]
    ry = [jnp.concatenate([rf[i], av[i][L:]], axis=1) - _nn(_bf(g4[i][L:, L:]), _bf(x[i])) for i in range(len(units))]
    gh = [_tn(_bf(x[i]), _bf(fetch(bp_s, r0, h))) for i, (_, r0, h) in enumerate(units)]
    hk = [_tn(vb[i], _bf(fetch(kp_s, r0, h))) for i, (_, r0, h) in enumerate(units)]
    ys = []
    for i, (si, r0, h) in enumerate(units):
        S = s_ref[si, h]
        Sb = _bf(S)
        ys.append(_nt(_bf(ry[i][:, :RW_HD]), Sb) + ry[i][:, RW_HD:])
        w_last = fetch(wl_s, r0, h)[0:1, :]
        s_ref[si, h] = S * w_last - _nn(Sb, _bf(gh[i][:RW_HD])) + hk[i] - gh[i][RW_HD:]
    for i in range(0, len(units), 2):
        _, r0, h = units[i]
        y_s[pl.ds(r0, L), RW_HD * h:RW_HD * (h + 2)] = jnp.concatenate([ys[i], ys[i + 1]], axis=1)


N_RW_PRM = 13


def _rwkv_kernel(is_prompt, nsq, L, grp, *refs):
    n_p = nsq if is_prompt else 1
    p_refs = refs[:n_p]
    sh0_ref, s0_ref = refs[n_p:n_p + 2]
    pos = n_p + 2
    prm_refs = refs[pos:pos + N_RW_PRM]
    tri_ref, ones_ref, o_ref, sh_out_ref, s_out_ref = refs[pos + N_RW_PRM:pos + N_RW_PRM + 5]
    scr = refs[pos + N_RW_PRM + 5:]
    g_s, r_s, k_s = scr[9:12]
    mu = prm_refs[0][...]
    prm = tuple(r[...] for r in prm_refs[1:])
    if is_prompt:
        prev_scr = scr[12]

        @pl.when(pl.program_id(0) == 0)
        def _():
            s_out_ref[...] = s0_ref[...]
            prev_scr[...] = sh0_ref[...]

        ps, pps = [], []
        for s in range(nsq):
            ps.append(p_refs[s][...])
            pps.append(_shift_rows(ps[s], prev_scr[s], 1))
            prev_scr[s] = ps[s][L - SUBLANES:L]
        p = jnp.concatenate(ps, axis=0)
        pp = jnp.concatenate(pps, axis=0)
        sh_out_ref[...] = prev_scr[...]
    else:
        p = p_refs[0][...]
        p3 = p.reshape(nsq, L, p.shape[1])
        pp = jnp.where(_iota(p3.shape, 1) == 0, sh0_ref[...], pltpu.roll(p3, 1, 1)).reshape(p.shape)
        sh_out_ref[...] = p3[:, L - 1:L, :]
        s_out_ref[...] = s0_ref[...]

    r, lw, kmod, kk, bb, v, g = _rw_prep(p + (pp - p) * mu, prm)
    cum = _dot3r(tri_ref[...], lw)
    tot = _dot3r(ones_ref[...], lw)
    e_neg = jnp.exp(-cum)
    e_rem = jnp.exp(tot - cum)
    q_s, rr_s, kt_s, bt_s, kp_s, bp_s, v_s, wl_s, y_s = scr[:9]
    q_s[...] = kk * jnp.exp(cum - lw)
    rr_s[...] = r * jnp.exp(cum)
    kt_s[...] = kmod * e_neg
    bt_s[...] = bb * e_neg
    kp_s[...] = kmod * e_rem
    bp_s[...] = bb * e_rem
    v_s[...] = v
    wl_s[...] = jnp.exp(tot)
    g_s[...], r_s[...], k_s[...] = g, r, kmod

    if is_prompt:
        _rw_chunks(L, [(s, s * L) for s in range(nsq)], scr[:9], s_out_ref)
    else:
        def body(i, c):
            _rw_chunks(L, [(i * grp + j, pl.multiple_of((i * grp + j) * L, SUBLANES)) for j in range(grp)],
                       scr[:9], s_out_ref)
            return c
        lax.fori_loop(0, nsq // grp, body, 0)

    out = _rw_post(y_s[...], r_s[...], k_s[...], v_s[...], g_s[...], prm).astype(BF16)
    if is_prompt:
        for s in range(nsq):
            o_ref[s] = out[s * L:(s + 1) * L]
    else:
        o_ref[...] = out


def _rwkv(P, row0, B, T, sh0, s0, prm_list, s0_layer=None):
    is_prompt = T % CHUNK == 0
    wide = 4 * GROUP_W
    s_shape = (B, RW_H, RW_HD, RW_HD)
    if is_prompt:
        L, nsq, grid = CHUNK, B, (T // CHUNK,)
        rows = nsq * L
        in_specs = [pl.BlockSpec((L, wide), functools.partial(lambda s, c: ((row0 + s * T) // L + c, 0), s))
                    for s in range(nsq)]
        in_specs += [pl.BlockSpec(sh0.shape, lambda c: (0, 0, 0)), pl.BlockSpec(s0.shape, lambda c: (0, 0, 0, 0))]
        out_specs = [pl.BlockSpec((nsq, L, GROUP_W), lambda c: (0, c, 0)),
                     pl.BlockSpec(sh0.shape, lambda c: (0, 0, 0)), pl.BlockSpec(s0.shape, lambda c: (0, 0, 0, 0))]
        out_shape = [jax.ShapeDtypeStruct((nsq, T, GROUP_W), BF16)]
        args = [P] * nsq
    else:
        L, nsq, grid = T, 16, (B // 16,)
        rows = nsq * L
        s_spec = (pl.BlockSpec((nsq, RW_H, RW_HD, RW_HD), lambda i: (i, 0, 0, 0)) if s0_layer is None else
                  pl.BlockSpec((None, nsq, RW_H, RW_HD, RW_HD), lambda i: (s0_layer, i, 0, 0, 0)))
        in_specs = [pl.BlockSpec((rows, wide), lambda i: (row0 // rows + i, 0)),
                    pl.BlockSpec((nsq, 1, wide), lambda i: (i, 0, 0)), s_spec]
        out_specs = [pl.BlockSpec((rows, GROUP_W), lambda i: (i, 0)),
                     pl.BlockSpec((nsq, 1, wide), lambda i: (i, 0, 0)),
                     pl.BlockSpec((nsq, RW_H, RW_HD, RW_HD), lambda i: (i, 0, 0, 0))]
        out_shape = [jax.ShapeDtypeStruct((B * T, GROUP_W), BF16)]
        args = [P]
    seq_of_row = jnp.arange(rows) // L
    same = seq_of_row[:, None] == seq_of_row[None, :]
    tri = (same & (jnp.arange(rows)[:, None] >= jnp.arange(rows)[None, :])).astype(BF16)
    consts = list(prm_list) + [tri, same.astype(BF16)]
    in_specs += [pl.BlockSpec(x.shape, lambda *a: (0, 0)) for x in consts]
    out_shape += [jax.ShapeDtypeStruct(sh0.shape, F32), jax.ShapeDtypeStruct(s_shape, F32)]
    scratch = [pltpu.VMEM((rows, GROUP_W), F32)] * 12
    if is_prompt:
        scratch.append(pltpu.VMEM(sh0.shape, F32))
    outs = pl.pallas_call(
        functools.partial(_rwkv_kernel, is_prompt, nsq, L, 2),
        grid=grid,
        in_specs=in_specs,
        out_specs=out_specs,
        out_shape=out_shape,
        scratch_shapes=scratch,
        compiler_params=_cparams(1),
        name="rwkv_prompt" if is_prompt else "rwkv_sample",
    )(*args, sh0, s0, *consts)
    return outs[0].reshape(B * T, GROUP_W), outs[1], outs[2]


SEQ_GROUP = 4


def _causal(L):
    return _iota((L, L), 0) >= _iota((L, L), 1)


def _heads(units):
    return [(u, h) for u in range(len(units)) for h in range(HEADS)]


def _hs(h):
    return slice(h * HEAD_D, (h + 1) * HEAD_D)


def _ret_units(L, units, S_ref, consts):
    cosv, sinv, dm_ref, qd_ref, kd_ref, gn, c_dec = consts
    uh = _heads(units)

    def rot(x):
        return x * cosv + pltpu.roll(x, HEAD_D // 2, 1) * sinv

    qr = [rot(units[u][0](h * HEAD_D, HEAD_D)) for u, h in uh]
    kr = [rot(units[u][0](GROUP_W + h * HEAD_D, HEAD_D)) * (HEAD_D ** -0.5) for u, h in uh]
    vb = [_bf(units[u][0](2 * GROUP_W + h * HEAD_D, HEAD_D)) for u, h in uh]
    S = [S_ref[units[u][3], h] for u, h in uh]
    A = [_nt(_bf(qr[i]), _bf(kr[i])) * dm_ref[h] for i, (u, h) in enumerate(uh)]
    o = [_nn(_bf(A[i]), vb[i]) + _nn(_bf(qr[i] * qd_ref[h]), _bf(S[i])) for i, (u, h) in enumerate(uh)]
    for i, (u, h) in enumerate(uh):
        S_ref[units[u][3], h] = c_dec[h] * S[i] + _tn(_bf(kr[i] * kd_ref[h]), vb[i])
    for i, (u, h) in enumerate(uh):
        oc = o[i] - jnp.mean(o[i], axis=1, keepdims=True)
        on = oc * lax.rsqrt(jnp.mean(oc * oc, axis=1, keepdims=True) + GN_EPS) * gn[:, _hs(h)]
        g = units[u][0](3 * GROUP_W + h * HEAD_D, HEAD_D)
        units[u][2](h * HEAD_D, _bf(_silu(g) * on))


def _ml_units(L, units, st_refs, consts):
    C_ref, n_ref, m_ref, cv_ref = st_refs
    ltri, ones_l, conv_w, conv_b, bias_sm, gn = consts
    uh = _heads(units)
    conv, xs, bc_all = [], [], []
    for get, get_sm, _, sidx in units:
        x = get(0, 2 * GROUP_W)
        prev8 = cv_ref[sidx]
        c = (conv_b + _shift_rows(x, prev8, 3) * conv_w[0:1] + _shift_rows(x, prev8, 2) * conv_w[1:2]
             + _shift_rows(x, prev8, 1) * conv_w[2:3] + x * conv_w[3:4])
        cv_ref[sidx] = x[L - SUBLANES:L]
        conv.append(_silu(c))
        xs.append(get_sm(LANES, LANES) + bias_sm)
        bc_all.append(_dot3r(ltri, _log_sigmoid(xs[-1])))
    causal = _causal(L)
    lane0 = _iota((L, LANES), 1) == 0
    q = [conv[u][:, _hs(h)] for u, h in uh]
    k = [conv[u][:, GROUP_W + h * HEAD_D:GROUP_W + (h + 1) * HEAD_D] * (HEAD_D ** -0.5) for u, h in uh]
    vb = [_bf(units[u][0](2 * GROUP_W + h * HEAD_D, HEAD_D)) for u, h in uh]
    ig = [xs[u][:, SM_MLI - LANES + h:SM_MLI - LANES + h + 1] for u, h in uh]
    bc = [bc_all[u][:, SM_MLF - LANES + h:SM_MLF - LANES + h + 1] for u, h in uh]
    m_prev = [m_ref[units[u][3], h][:, 0:1] for u, h in uh]
    row = [_dot3r(ones_l, jnp.where(lane0, ig[i] - bc[i], 0.0), "nt") for i in range(len(uh))]
    dlog = [jnp.where(causal, bc[i] + row[i], -jnp.inf) for i in range(len(uh))]
    inter = [bc[i] + m_prev[i] for i in range(len(uh))]
    m_t = [jnp.maximum(inter[i], jnp.max(dlog[i], axis=1, keepdims=True)) for i in range(len(uh))]
    dmat = [jnp.exp(dlog[i] - m_t[i]) for i in range(len(uh))]
    s_in = [jnp.exp(inter[i] - m_t[i]) for i in range(len(uh))]
    C = [C_ref[units[u][3], h] for u, h in uh]
    n = [n_ref[units[u][3], h] for u, h in uh]
    qb = [_bf(x) for x in q]
    A = [_nt(qb[i], _bf(k[i])) * dmat[i] for i in range(len(uh))]
    num = [_nn(_bf(A[i]), vb[i]) + s_in[i] * _nt(qb[i], _bf(C[i])) for i in range(len(uh))]
    den = [jnp.sum(A[i], axis=1, keepdims=True) + s_in[i] * jnp.sum(q[i] * n[i], axis=1, keepdims=True)
           for i in range(len(uh))]
    hh = [num[i] / jnp.maximum(jnp.abs(den[i]), jnp.exp(-m_t[i])) for i in range(len(uh))]
    for i, (u, h) in enumerate(uh):
        sidx = units[u][3]
        m_new = m_t[i][L - 1:L, :]
        b_last = bc[i][L - 1:L, :]
        carry = jnp.exp(b_last + m_prev[i] - m_new)
        wk = jnp.exp(b_last - bc[i] + ig[i] - m_new) * k[i]
        C_ref[sidx, h] = carry * C[i] + _tn(vb[i], _bf(wk))
        n_ref[sidx, h] = carry * n[i] + jnp.sum(wk, axis=0, keepdims=True)
        m_ref[sidx, h] = jnp.broadcast_to(m_new, (1, LANES))
    for i, (u, h) in enumerate(uh):
        hc = hh[i] - jnp.mean(hh[i], axis=1, keepdims=True)
        hn = hc * lax.rsqrt(jnp.mean(hc * hc, axis=1, keepdims=True) + GN_EPS) * gn[:, _hs(h)]
        o_pre = units[u][0](3 * GROUP_W + h * HEAD_D, HEAD_D)
        units[u][2](h * HEAD_D, _bf(_sigmoid(o_pre) * hn))


def _gl_units(L, units, S_ref, consts):
    ltri, a2p, ab, gn = consts
    uh = _heads(units)
    mid = L // 2 - 1
    q_in, k_in, q_st, k_st, e_last = [], [], [], [], []
    for get, get_sm, _, _ in units:
        la = _log_sigmoid(_nn(_bf(get_sm(LANES, LANES)), a2p) + ab) / GL_TAU
        bc = _dot3r(ltri, la)
        b_mid = bc[mid:mid + 1, :]
        b_last = bc[L - 1:L, :]
        qall = get(0, GL_DK * HEADS) * (GL_DK ** -0.5)
        kall = get(GL_DK * HEADS, GL_DK * HEADS)
        q_in.append(_bf(qall * jnp.exp(jnp.minimum(bc - b_mid, 80.0))))
        k_in.append(_bf(kall * jnp.exp(jnp.minimum(b_mid - bc, 80.0))))
        q_st.append(_bf(qall * jnp.exp(bc)))
        k_st.append(_bf(kall * jnp.exp(b_last - bc)))
        e_last.append(jnp.exp(b_last))
    causal = _causal(L)
    eye = _iota((GL_DK, GL_DK), 0) == _iota((GL_DK, GL_DK), 1)

    def ks(h):
        return slice(h * GL_DK, (h + 1) * GL_DK)

    vb = [_bf(units[u][0](GROUP_W + h * HEAD_D, HEAD_D)) for u, h in uh]
    S = [S_ref[units[u][3], h] for u, h in uh]
    A = [jnp.where(causal, _nt(q_in[u][:, ks(h)], k_in[u][:, ks(h)]), 0.0) for u, h in uh]
    o = [_nn(_bf(A[i]), vb[i]) + _nn(q_st[u][:, ks(h)], _bf(S[i])) for i, (u, h) in enumerate(uh)]
    for i, (u, h) in enumerate(uh):
        e_col = jnp.sum(jnp.where(eye, e_last[u][:, ks(h)], 0.0), axis=1, keepdims=True)
        S_ref[units[u][3], h] = e_col * S[i] + _tn(k_st[u][:, ks(h)], vb[i])
    for i, (u, h) in enumerate(uh):
        on = o[i] * lax.rsqrt(jnp.mean(o[i] * o[i], axis=1, keepdims=True) + GN_EPS) * gn[:, _hs(h)]
        g = units[u][0](2 * GROUP_W + h * HEAD_D, HEAD_D)
        units[u][2](h * HEAD_D, _bf(_silu(g) * on))


def _chunk_kernel(kind, nslab, nunit, L, n_state, n_const, has_sm, *refs):
    pos = 0
    p_refs = refs[pos:pos + nslab]; pos += nslab
    sm_refs = refs[pos:pos + (nslab if has_sm else 0)]; pos += (nslab if has_sm else 0)
    st_in = refs[pos:pos + n_state]; pos += n_state
    c_refs = refs[pos:pos + n_const]; pos += n_const
    o_ref = refs[pos]; pos += 1
    st_out = refs[pos:pos + n_state]; pos += n_state
    c = pl.program_id(1)

    @pl.when(c == 0)
    def _():
        for a, b in zip(st_in, st_out):
            b[...] = a[...]

    def unit(slab, seq, sidx):
        r0 = seq * L if isinstance(seq, int) else pl.multiple_of(seq * L, SUBLANES)

        def get(col, width):
            return p_refs[slab][pl.ds(r0, L), col:col + width]

        def get_sm(col, width):
            return sm_refs[slab][pl.ds(r0, L), col:col + width]

        def put(col, val):
            o_ref[slab, pl.ds(r0, L), col:col + val.shape[1]] = val

        return (get, get_sm, put, sidx)

    def run(units):
        if kind == "ret":
            cosv, sinv = c_refs[0][...], c_refs[1][...]
            consts = (cosv, sinv, c_refs[2], c_refs[3], c_refs[4], c_refs[5][...], _RT_CDEC[L])
            _ret_units(L, units, st_out[0], consts)
        elif kind == "ml":
            _ml_units(L, units, st_out, tuple(r[...] for r in c_refs))
        else:
            _gl_units(L, units, st_out[0], tuple(r[...] for r in c_refs))

    if nunit == 1:
        run([unit(s, 0, s) for s in range(nslab)])
    else:
        def body(i, carry):
            run([unit(0, i * SEQ_GROUP + j, i * SEQ_GROUP + j) for j in range(SEQ_GROUP)])
            return carry
        lax.fori_loop(0, nunit // SEQ_GROUP, body, 0)


def _rt_log_gamma():
    return [math.log1p(-(2.0 ** (-5.0 - h))) for h in range(HEADS)]


_RT_CDEC = {L: [math.exp(L * lg) for lg in _rt_log_gamma()] for L in (8, CHUNK)}


def _chunk_call(kind, P, col_blk, col_w, sm, row0, nseq, T, states, consts, const_chunked, name):
    has_sm = sm
    if T % CHUNK == 0:
        L, nslab, nunit = CHUNK, nseq, 1
        grid = (1, T // L)
        rows = L
        def pmap(s, i, c):
            return ((row0 + s * T) // L + c, col_blk)
        def smap(s, i, c):
            return ((row0 + s * T) // L + c, 3)
        def omap(i, c):
            return (0, c, 0)
        sblk = nseq
    else:
        L, nslab, nunit = T, 1, 16
        grid = (nseq // nunit, 1)
        rows = L * nunit
        def pmap(s, i, c):
            return (row0 // rows + i, col_blk)
        def smap(s, i, c):
            return (row0 // rows + i, 3)
        def omap(i, c):
            return (0, i, 0)
        sblk = nunit
    in_specs = [pl.BlockSpec((rows, col_w), functools.partial(pmap, s)) for s in range(nslab)]
    args = [P] * nslab
    if has_sm:
        in_specs += [pl.BlockSpec((rows, GROUP_W), functools.partial(smap, s)) for s in range(nslab)]
        args += [P] * nslab
    st_specs, st_shapes = [], []
    for st in states:
        arr, layer = st if isinstance(st, tuple) else (st, None)
        tail = arr.shape[1:] if layer is None else arr.shape[2:]
        nd = len(tail)
        st_specs.append(pl.BlockSpec((sblk,) + tail, lambda i, c, nd=nd: (i,) + (0,) * nd))
        st_shapes.append(jax.ShapeDtypeStruct((nseq,) + tail, F32))
        if layer is None:
            in_specs.append(st_specs[-1])
        else:
            in_specs.append(pl.BlockSpec((None, sblk) + tail, lambda i, c, nd=nd, layer=layer: (layer, i) + (0,) * nd))
        args.append(arr)
    for cst, chunked in zip(consts, const_chunked):
        if chunked:
            in_specs.append(pl.BlockSpec((None,) + cst.shape[1:], lambda i, c: (c, 0, 0)))
        else:
            in_specs.append(pl.BlockSpec(cst.shape, lambda i, c, nd=cst.ndim: (0,) * nd))
        args.append(cst)
    out_specs = [pl.BlockSpec((nslab, rows, GROUP_W), omap)] + st_specs
    out_shape = [jax.ShapeDtypeStruct((nslab, nseq * T // nslab, GROUP_W), BF16)] + st_shapes
    outs = pl.pallas_call(
        functools.partial(_chunk_kernel, kind, nslab, nunit, L, len(states), len(consts), has_sm),
        grid=grid,
        in_specs=in_specs,
        out_specs=out_specs,
        out_shape=out_shape,
        compiler_params=_cparams(2),
        name=name,
    )(*args)
    return outs[0].reshape(nseq * T, GROUP_W), outs[1:]


def _pack_w_in(w_in):
    z = jnp.zeros(w_in.shape[:2] + (GROUP_W - 184,), w_in.dtype)
    parts = [w_in[..., 0:1696], w_in[..., 3744:3752], w_in[..., 6824:6840], z,
             w_in[..., 1696:3744], w_in[..., 3752:5800], w_in[..., 5800:6824], w_in[..., 6840:7352]]
    return jnp.concatenate(parts, axis=-1).astype(BF16)


def _row(v):
    return v.reshape(1, -1).astype(F32)


def _rows_at(mat, r0, rows_total):
    z = jnp.zeros((rows_total, mat.shape[1]), F32).at[r0:r0 + mat.shape[0]].set(mat)
    return z.astype(BF16)


def _ret_consts(pos0, T, L, gn):
    pos = pos0 + jnp.arange(T, dtype=F32)
    inv = 1.0 / (10000.0 ** jnp.linspace(0.0, 1.0, HEAD_D // 2, dtype=F32))
    ang = pos[:, None] * inv[None, :]
    cos, sin = jnp.cos(ang), jnp.sin(ang)
    cos2 = jnp.concatenate([cos, cos], axis=1).reshape(T // L, L, HEAD_D)
    sin2 = jnp.concatenate([-sin, sin], axis=1).reshape(T // L, L, HEAD_D)
    lg = jnp.log1p(-jnp.exp2(-5.0 - jnp.arange(HEADS, dtype=F32)))
    idx = jnp.arange(L, dtype=F32)
    rel = idx[:, None] - idx[None, :]
    dmat = jnp.where(rel >= 0, jnp.exp(jnp.maximum(rel, 0.0) * lg[:, None, None]), 0.0)
    q_dec = jnp.broadcast_to(jnp.exp((idx + 1.0) * lg[:, None])[..., None], (HEADS, L, HEAD_D))
    k_dec = jnp.broadcast_to(jnp.exp((L - 1.0 - idx) * lg[:, None])[..., None], (HEADS, L, HEAD_D))
    return [cos2, sin2, dmat, q_dec, k_dec, _row(gn)], [True, True, False, False, False, False]


def _tri(L):
    return jnp.tril(jnp.ones((L, L), F32)).astype(BF16)


def _moe_plan(ids, tm, n_tiles):
    e = ids.reshape(-1)
    n = e.shape[0]
    onehot = (e[:, None] == jnp.arange(N_EXPERTS, dtype=jnp.int32)[None, :]).astype(jnp.int32)
    rank = jnp.sum((jnp.cumsum(onehot, axis=0) - 1) * onehot, axis=1)
    counts = jnp.sum(onehot, axis=0)
    padded = ((counts + tm - 1) // tm) * tm
    ends = jnp.cumsum(padded)
    starts = ends - padded
    dest = jnp.sum(starts[None, :] * onehot, axis=1) + rank
    row_src = jnp.zeros((n_tiles * tm,), jnp.int32).at[dest].set(jnp.arange(n, dtype=jnp.int32) // 2)
    tile_start = jnp.arange(n_tiles, dtype=jnp.int32) * tm
    tile_expert = jnp.minimum(jnp.sum((tile_start[:, None] >= ends[None, :]).astype(jnp.int32), axis=1),
                              N_EXPERTS - 1)
    return row_src, dest.reshape(-1, 2), tile_expert, (ends[-1:] // tm).astype(jnp.int32)


def kernel(x_prompt, x_sample, state_rw_shift, state_rw_wkv, state_ml_conv, state_ml_C, state_ml_n, state_ml_m, state_rt_S, state_gl_S, w_in, rw_mu, rw_w0, rw_w2, rw_a0, rw_a2, rw_g2, rw_kk, rw_ka, rw_rk, rw_ln_w, rw_ln_b, ml_conv_w, ml_conv_b, ml_bi, ml_bf, ml_gn_w, rt_gn_w, gl_a2, gl_ab, gl_gn_w, w_out, ln1_w, ln1_b, ln2_w, ln2_b, ffn_w1, ffn_w3, ffn_w2, moe_router, moe_w1, moe_w3, moe_w2):
    Bp, Tp = x_prompt.shape[:2]
    Bs, Ts = x_sample.shape[:2]
    Mp, Ms = Bp * Tp, Bs * Ts
    M = Mp + Ms
    depth = w_in.shape[0]
    x = jnp.concatenate([x_prompt.reshape(Mp, D_MODEL), x_sample.reshape(Ms, D_MODEL)], axis=0)
    xb = x.astype(BF16)
    w_in_p = _pack_w_in(w_in)
    w_out_b = w_out.astype(BF16)
    blockdiag = (jnp.arange(LANES)[:, None] // RW_HD) == (jnp.arange(LANES)[None, :] // RW_HD)
    g64 = blockdiag.astype(BF16)
    g64m = (blockdiag.astype(F32) / RW_HD).astype(BF16)
    pad_sh = 4 * GROUP_W - RW_COLS
    outs = {k: [[], []] for k in ("sh", "wkv", "conv", "C", "n", "m", "rt", "gl")}

    for l in range(depth):
        tm_mm = _pick(M, (1024, 512, 256, 128))
        P = _mm([xb], [w_in_p[l]], tm_mm, GROUP_W, "mm_in")

        prm = [_row(jnp.pad(rw_mu[l], (0, pad_sh))), _row(rw_w0[l]), _row(rw_a0[l]),
               _rows_at(rw_w2[l], 0, 2 * LANES), _rows_at(rw_a2[l], 32, 2 * LANES), _rows_at(rw_g2[l], 64, 2 * LANES),
               _row(rw_kk[l]), _row(rw_ka[l]), _row(rw_rk[l]), _row(rw_ln_w[l]), _row(rw_ln_b[l]), g64, g64m]
        o_rw_p, sh_p, wkv_p = _rwkv(P, 0, Bp, Tp, jnp.zeros((Bp, SUBLANES, 4 * GROUP_W), F32),
                                    jnp.zeros((Bp, RW_H, RW_HD, RW_HD), F32), prm)
        sh_in = jnp.pad(state_rw_shift[l], ((0, 0), (0, pad_sh)))[:, None, :]
        o_rw_s, sh_s, wkv_s = _rwkv(P, Mp, Bs, Ts, sh_in, state_rw_wkv, prm, s0_layer=l)
        outs["sh"][0].append(sh_p[:, SUBLANES - 1, :RW_COLS])
        outs["sh"][1].append(sh_s[:, 0, :RW_COLS])
        outs["wkv"][0].append(wkv_p)
        outs["wkv"][1].append(wkv_s)

        bias_sm = jnp.zeros((LANES,), F32).at[SM_MLI - LANES:SM_MLI - LANES + HEADS].set(ml_bi[l])
        bias_sm = bias_sm.at[SM_MLF - LANES:SM_MLF - LANES + HEADS].set(ml_bf[l])

        def ml_consts(L):
            return [_tri(L), jnp.ones((L, LANES), BF16), ml_conv_w[l], _row(ml_conv_b[l]), _row(bias_sm), _row(ml_gn_w[l])]

        def ml_states(C, n, m, conv):
            b = n.shape[0]
            return [C, n[:, :, None, :], jnp.broadcast_to(m[:, :, None, None], (b, HEADS, 1, LANES)),
                    jnp.pad(conv, ((0, 0), (SUBLANES - conv.shape[1], 0), (0, 0)))]

        zp = lambda *s: jnp.zeros((Bp,) + s, F32)
        o_ml_p, st_p = _chunk_call("ml", P, 1, 4 * GROUP_W, True, 0, Bp, Tp,
                                   ml_states(zp(HEADS, HEAD_D, HEAD_D), zp(HEADS, HEAD_D), zp(HEADS), zp(3, 2 * GROUP_W)),
                                   ml_consts(CHUNK), [False] * 6, "mlstm_prompt")
        o_ml_s, st_s = _chunk_call("ml", P, 1, 4 * GROUP_W, True, Mp, Bs, Ts,
                                   ml_states((state_ml_C, l), state_ml_n[l], state_ml_m[l], state_ml_conv[l]),
                                   ml_consts(Ts), [False] * 6, "mlstm_sample")
        for g, st in enumerate((st_p, st_s)):
            outs["C"][g].append(st[0])
            outs["n"][g].append(st[1][:, :, 0, :])
            outs["m"][g].append(st[2][:, :, 0, 0])
            outs["conv"][g].append(st[3][:, SUBLANES - 3:, :])

        c_p, ch_p = _ret_consts(0.0, Tp, CHUNK, rt_gn_w[l])
        c_s, ch_s = _ret_consts(float(PAST_LEN), Ts, Ts, rt_gn_w[l])
        o_rt_p, st_p = _chunk_call("ret", P, 2, 4 * GROUP_W, False, 0, Bp, Tp, [zp(HEADS, HEAD_D, HEAD_D)], c_p, ch_p, "ret_prompt")
        o_rt_s, st_s = _chunk_call("ret", P, 2, 4 * GROUP_W, False, Mp, Bs, Ts, [(state_rt_S, l)], c_s, ch_s, "ret_sample")
        outs["rt"][0].append(st_p[0])
        outs["rt"][1].append(st_s[0])

        def gl_consts(L):
            return [_tri(L), _rows_at(gl_a2[l], SM_GLA - LANES, LANES), _row(gl_ab[l]), _row(gl_gn_w[l])]

        o_gl_p, st_p = _chunk_call("gl", P, 4, 3 * GROUP_W, True, 0, Bp, Tp, [zp(HEADS, GL_DK, HEAD_D)],
                                   gl_consts(CHUNK), [False] * 4, "gla_prompt")
        o_gl_s, st_s = _chunk_call("gl", P, 4, 3 * GROUP_W, True, Mp, Bs, Ts, [(state_gl_S, l)],
                                   gl_consts(Ts), [False] * 4, "gla_sample")
        outs["gl"][0].append(st_p[0])
        outs["gl"][1].append(st_s[0])

        mixed = [jnp.concatenate([a, b], axis=0) for a, b in
                 ((o_rw_p, o_rw_s), (o_ml_p, o_ml_s), (o_rt_p, o_rt_s), (o_gl_p, o_gl_s))]
        mix = _mm(mixed, [w_out_b[l, g * GROUP_W:(g + 1) * GROUP_W] for g in range(4)], tm_mm, GROUP_W, "mm_out")
        x, xb = _add_ln(x, [mix], None, ln1_w[l], ln1_b[l])

        if l % 2 == 0:
            tm = _pick(M, (512, 256, 128))
            y = _ffn(xb, jnp.zeros((M // tm,), jnp.int32), jnp.full((1,), M // tm, jnp.int32),
                     ffn_w1[l // 2][None].astype(BF16), ffn_w3[l // 2][None].astype(BF16),
                     ffn_w2[l // 2][None].astype(BF16), tm)
            x, xb = _add_ln(x, [y], None, ln2_w[l], ln2_b[l])
        else:
            tm = _pick(M, (512, 256, 128))
            n_tiles = 2 * M // tm + N_EXPERTS
            gates, ids = _router(x, jnp.pad(moe_router[l // 2], ((0, 0), (0, LANES - N_EXPERTS))))
            row_src, dest, tile_expert, n_used = _moe_plan(ids[:, :2], tm, n_tiles)
            xs = jnp.take(xb, row_src, axis=0)
            ysorted = _ffn(xs, tile_expert, n_used, moe_w1[l // 2].astype(BF16), moe_w3[l // 2].astype(BF16),
                           moe_w2[l // 2].astype(BF16), tm)
            ys = [jnp.take(ysorted, dest[:, k], axis=0) for k in range(2)]
            x, xb = _add_ln(x, ys, [gates[:, 0:1], gates[:, 1:2]], ln2_w[l], ln2_b[l])

    def st(key, g):
        return jnp.stack(outs[key][g])

    y_prompt = x[:Mp].reshape(Bp, Tp, D_MODEL)
    y_sample = x[Mp:].reshape(Bs, Ts, D_MODEL)
    res = [y_prompt, y_sample]
    for key in ("sh", "wkv", "conv", "C", "n", "m", "rt", "gl"):
        res += [st(key, 0), st(key, 1)]
    return tuple(res)
```

```python
import functools
import math

import jax
import jax.numpy as jnp
from jax import lax
from jax.experimental import pallas as pl
from jax.experimental.pallas import tpu as pltpu

F32 = jnp.float32
BF16 = jnp.bfloat16

D_MODEL = 2048
GROUP_W = 512
RW_COLS = 1696
RW_H, RW_HD = 8, 64
HEADS, HEAD_D = 4, 128
GL_DK = 64
CHUNK = 64
ALPHA = 4.0 ** 0.25
LN_EPS = 1e-5
GN_EPS = 1e-5
RW_GN_EPS = 64e-5
GL_TAU = 16.0
PAST_LEN = 16384
N_EXPERTS = 8

LANES = 128
SUBLANES = 8
VMEM_LIMIT = 48 * 1024 * 1024

N_PACKED = 15 * GROUP_W
SM_MLI, SM_MLF, SM_GLA = 160, 164, 168


def _nn(a, b):
    return jnp.dot(a, b, preferred_element_type=F32)


def _nt(a, b):
    return lax.dot_general(a, b, (((1,), (1,)), ((), ())), preferred_element_type=F32)


def _tn(a, b):
    return lax.dot_general(a, b, (((0,), (0,)), ((), ())), preferred_element_type=F32)


def _bf(x):
    return x.astype(BF16)


def _split3(x):
    hi = x.astype(BF16)
    r1 = x - hi.astype(F32)
    mid = r1.astype(BF16)
    lo = (r1 - mid.astype(F32)).astype(BF16)
    return hi, mid, lo


def _dot3(x, m, kind="nn"):
    f = {"nn": _nn, "nt": _nt, "tn": _tn}[kind]
    hi, mid, lo = _split3(x)
    return f(hi, m) + f(mid, m) + f(lo, m)


def _dot3r(m, x, kind="nn"):
    f = {"nn": _nn, "nt": _nt, "tn": _tn}[kind]
    hi, mid, lo = _split3(x)
    return f(m, hi) + f(m, mid) + f(m, lo)


def _sigmoid(x):
    return 1.0 / (1.0 + jnp.exp(-x))


def _softplus(x):
    return jnp.maximum(x, 0.0) + jnp.log1p(jnp.exp(-jnp.abs(x)))


def _log_sigmoid(x):
    return -_softplus(-x)


def _silu(x):
    return x * _sigmoid(x)


def _iota(shape, axis):
    return lax.broadcasted_iota(jnp.int32, shape, axis)


def _shift_rows(x, prev8, j):
    rows = x.shape[0]
    xr = pltpu.roll(x, j, 0)
    pr = pltpu.roll(prev8, j, 0)
    first = jnp.where(_iota((SUBLANES, x.shape[1]), 0) < j, pr, xr[0:SUBLANES])
    if rows == SUBLANES:
        return first
    return jnp.concatenate([first, xr[SUBLANES:]], axis=0)


def _pick(n, cands):
    return next(c for c in cands if n % c == 0)


def _cparams(n_axes):
    return pltpu.CompilerParams(dimension_semantics=("arbitrary",) * n_axes, vmem_limit_bytes=VMEM_LIMIT)


def _mm_kernel(n_in, *refs):
    xs, ws, o_ref = refs[:n_in], refs[n_in:2 * n_in], refs[2 * n_in]
    acc = _nn(xs[0][...], ws[0][...])
    for x_ref, w_ref in zip(xs[1:], ws[1:]):
        acc = acc + _nn(x_ref[...], w_ref[...])
    o_ref[...] = acc


def _mm(xs, w, layer, tn, name):
    M, N = xs[0].shape[0], w.shape[2]
    n_in = len(xs)
    tm = _pick(M, (1024, 512, 256, 128))
    in_specs = ([pl.BlockSpec((tm, x.shape[1]), lambda j, i: (i, 0)) for x in xs]
                + [pl.BlockSpec((None, x.shape[1], tn), functools.partial(lambda g, j, i: (layer, g, j), g))
                   for g, x in enumerate(xs)])
    return pl.pallas_call(
        functools.partial(_mm_kernel, n_in),
        grid=(N // tn, M // tm),
        in_specs=in_specs,
        out_specs=pl.BlockSpec((tm, tn), lambda j, i: (i, j)),
        out_shape=jax.ShapeDtypeStruct((M, N), F32),
        compiler_params=_cparams(2),
        name=name,
    )(*xs, *([w] * n_in))


def _add_ln_kernel(n_y, gated, x_ref, *refs):
    ys = refs[:n_y]
    gs = refs[n_y:2 * n_y] if gated else ()
    w_ref, b_ref, o_ref, ob_ref = refs[-4:]
    z = ALPHA * x_ref[...]
    for i, y_ref in enumerate(ys):
        y = y_ref[...]
        if gated:
            y = gs[i][...] * y
        z = z + y
    mu = jnp.mean(z, axis=-1, keepdims=True)
    zc = z - mu
    var = jnp.mean(zc * zc, axis=-1, keepdims=True)
    out = zc * lax.rsqrt(var + LN_EPS) * w_ref[...] + b_ref[...]
    o_ref[...] = out
    ob_ref[...] = out.astype(BF16)


def _add_ln(x, ys, gates, w, b):
    M = x.shape[0]
    tm = _pick(M, (256, 128))
    gated = gates is not None
    row = pl.BlockSpec((tm, D_MODEL), lambda i: (i, 0))
    in_specs = [row] + [row] * len(ys)
    args = [x] + list(ys)
    if gated:
        in_specs += [pl.BlockSpec((tm, 1), lambda i: (i, 0))] * len(ys)
        args += list(gates)
    in_specs += [pl.BlockSpec((1, D_MODEL), lambda i: (0, 0))] * 2
    args += [w.reshape(1, D_MODEL), b.reshape(1, D_MODEL)]
    return pl.pallas_call(
        functools.partial(_add_ln_kernel, len(ys), gated),
        grid=(M // tm,),
        in_specs=in_specs,
        out_specs=[row, row],
        out_shape=[jax.ShapeDtypeStruct((M, D_MODEL), F32), jax.ShapeDtypeStruct((M, D_MODEL), BF16)],
        compiler_params=_cparams(1),
        name="add_ln",
    )(*args)


def _ffn_up_kernel(te_ref, nu_ref, x_ref, w1_ref, w3_ref, h_ref):
    @pl.when(pl.program_id(0) < nu_ref[0])
    def _():
        x = x_ref[...]
        h_ref[...] = (_silu(_nn(x, w1_ref[...])) * _nn(x, w3_ref[...])).astype(BF16)


def _ffn_down_kernel(te_ref, nu_ref, h_ref, w2_ref, o_ref):
    @pl.when(pl.program_id(0) < nu_ref[0])
    def _():
        o_ref[...] = _nn(h_ref[...], w2_ref[...])


FFN_TF = 1408
FFN_TN = 512


def _ffn(xs, tile_expert, n_used, w1, w3, w2, tm):
    R = xs.shape[0]
    F = w1.shape[2]
    nf, nn = F // FFN_TF, D_MODEL // FFN_TN

    def last(i, j, nu, n):
        return jnp.where(i < nu[0], j, n - 1)

    up_spec = pltpu.PrefetchScalarGridSpec(
        num_scalar_prefetch=2,
        grid=(R // tm, nf),
        in_specs=[
            pl.BlockSpec((tm, D_MODEL), lambda i, f, te, nu: (i, 0)),
            pl.BlockSpec((None, D_MODEL, FFN_TF), lambda i, f, te, nu: (te[i], 0, last(i, f, nu, nf))),
            pl.BlockSpec((None, D_MODEL, FFN_TF), lambda i, f, te, nu: (te[i], 0, last(i, f, nu, nf))),
        ],
        out_specs=pl.BlockSpec((tm, FFN_TF), lambda i, f, te, nu: (i, f)),
    )
    h = pl.pallas_call(
        _ffn_up_kernel,
        grid_spec=up_spec,
        out_shape=jax.ShapeDtypeStruct((R, F), BF16),
        compiler_params=_cparams(2),
        name="ffn_up",
    )(tile_expert, n_used, xs, w1, w3)
    down_spec = pltpu.PrefetchScalarGridSpec(
        num_scalar_prefetch=2,
        grid=(R // tm, nn),
        in_specs=[
            pl.BlockSpec((tm, F), lambda i, n, te, nu: (i, 0)),
            pl.BlockSpec((None, F, FFN_TN), lambda i, n, te, nu: (te[i], 0, last(i, n, nu, nn))),
        ],
        out_specs=pl.BlockSpec((tm, FFN_TN), lambda i, n, te, nu: (i, n)),
    )
    return pl.pallas_call(
        _ffn_down_kernel,
        grid_spec=down_spec,
        out_shape=jax.ShapeDtypeStruct((R, D_MODEL), F32),
        compiler_params=_cparams(2),
        name="ffn_down",
    )(tile_expert, n_used, h, w2)


def _router_kernel(x_ref, r_ref, g_ref, i_ref):
    xh, xm, xl = _split3(x_ref[...])
    rh, rm, rl = _split3(r_ref[...])
    logits = (_nn(xh, rh) + _nn(xh, rm) + _nn(xm, rh)) + (_nn(xh, rl) + _nn(xl, rh) + _nn(xm, rm))
    shape = logits.shape
    lane = _iota(shape, 1).astype(F32)
    neg = jnp.float32(-jnp.inf)
    l1 = jnp.where(lane < N_EXPERTS, logits, neg)
    m1 = jnp.max(l1, axis=1, keepdims=True)
    i1 = jnp.min(jnp.where(l1 == m1, lane, float(LANES)), axis=1, keepdims=True)
    l2 = jnp.where(lane == i1, neg, l1)
    m2 = jnp.max(l2, axis=1, keepdims=True)
    i2 = jnp.min(jnp.where(l2 == m2, lane, float(LANES)), axis=1, keepdims=True)
    e = jnp.exp(m2 - m1)
    g1 = 1.0 / (1.0 + e)
    g2 = e / (1.0 + e)
    g_ref[...] = jnp.where(lane == 0.0, g1, jnp.where(lane == 1.0, g2, 0.0))
    i_ref[...] = jnp.where(lane == 0.0, i1, jnp.where(lane == 1.0, i2, 0.0)).astype(jnp.int32)


def _router(x, router_pad):
    M = x.shape[0]
    tm = _pick(M, (512, 256, 128))
    return pl.pallas_call(
        _router_kernel,
        grid=(M // tm,),
        in_specs=[pl.BlockSpec((tm, D_MODEL), lambda i: (i, 0)),
                  pl.BlockSpec((D_MODEL, LANES), lambda i: (0, 0))],
        out_specs=[pl.BlockSpec((tm, LANES), lambda i: (i, 0))] * 2,
        out_shape=[jax.ShapeDtypeStruct((M, LANES), F32), jax.ShapeDtypeStruct((M, LANES), jnp.int32)],
        compiler_params=_cparams(1),
        name="router",
    )(x, router_pad)


def _rw_prep(pm, prm):
    (w0, a0, w2p, a2p, g2p, k_k, k_a, _, _, _, g64, _) = prm
    r = pm[:, 0:512]
    k = pm[:, 512:1024]
    v = pm[:, 1024:1536]
    sm = pm[:, 1536:1792]
    logw = -_softplus(-(w0 + _nn(_bf(jnp.tanh(sm)), w2p))) - 0.5
    lw = -jnp.exp(logw)
    a = _sigmoid(a0 + _nn(_bf(sm), a2p))
    g = _nn(_bf(_sigmoid(sm)), g2p)
    kk = k * k_k
    ss = jnp.concatenate([_dot3((kk * kk)[:, LANES * p:LANES * (p + 1)], g64) for p in range(4)], axis=1)
    kk = kk / jnp.maximum(jnp.sqrt(ss), 1e-12)
    kmod = k * (1.0 + (a - 1.0) * k_a)
    return r, lw, kmod, kk, kk * a, v, g


def _rw_post(y, r, kmod, v, g, prm):
    (_, _, _, _, _, _, _, r_k, ln_w, ln_b, g64, g64m) = prm

    def per_head(x, m):
        return jnp.concatenate([_dot3(x[:, LANES * p:LANES * (p + 1)], m) for p in range(4)], axis=1)

    yc = y - per_head(y, g64m)
    var = per_head(yc * yc, g64m)
    yn = yc * lax.rsqrt(var + RW_GN_EPS) * ln_w + ln_b
    bonus = per_head(r * kmod * r_k, g64) * v
    return (yn + bonus) * g


def _rw_chunks(L, seqs, scr, s_ref):
    q_s, rr_s, kt_s, bt_s, kp_s, bp_s, v_s, wl_s, y_s = scr
    units = [(si, r0, h) for (si, r0) in seqs for h in range(RW_H)]

    def fetch(ref, r0, h):
        return ref[pl.ds(r0, L), RW_HD * h:RW_HD * (h + 1)]

    row = _iota((2 * L, 2 * L), 0)
    col = _iota((2 * L, 2 * L), 1)
    colm = jnp.where(col >= L, col - L, col)
    mask4 = colm <= jnp.where(row < L, row - 1, row - L)
    qf = [fetch(q_s, r0, h) for (_, r0, h) in units]
    rf = [fetch(rr_s, r0, h) for (_, r0, h) in units]
    vb = [_bf(fetch(v_s, r0, h)) for (_, r0, h) in units]
    g4 = [jnp.where(mask4, _nt(_bf(jnp.concatenate([qf[i], rf[i]], axis=0)),
                               _bf(jnp.concatenate([fetch(kt_s, r0, h), fetch(bt_s, r0, h)], axis=0))), 0.0)
          for i, (_, r0, h) in enumerate(units)]
    av = [_nn(_bf(g4[i][:, :L]), vb[i]) for i in range(len(units))]
    x = [jnp.concatenate([qf[i], av[i][:L]], axis=1) for i in range(len(units))]
    pw = [g4[i][:L, L:] for i in range(len(units))]
    x = [x[i] - _nn(_bf(pw[i]), _bf(x[i])) for i in range(len(units))]
    for _ in range(int(math.log2(L)) - 1):
        pw = [_nn(_bf(m), _bf(m)) for m in pw]
        x = [x[i] + _nn(_bf(pw[i]), _bf(x[i])) for i in range(len(units))]
    ry = [jnp.concatenate([rf[i], av[i][L:]], axis=1) - _nn(_bf(g4[i][L:, L:]), _bf(x[i])) for i in range(len(units))]
    gh = [_tn(_bf(x[i]), _bf(fetch(bp_s, r0, h))) for i, (_, r0, h) in enumerate(units)]
    hk = [_tn(vb[i], _bf(fetch(kp_s, r0, h))) for i, (_, r0, h) in enumerate(units)]
    ys = []
    for i, (si, r0, h) in enumerate(units):
        S = s_ref[si, h]
        Sb = _bf(S)
        ys.append(_nt(_bf(ry[i][:, :RW_HD]), Sb) + ry[i][:, RW_HD:])
        w_last = fetch(wl_s, r0, h)[0:1, :]
        s_ref[si, h] = S * w_last - _nn(Sb, _bf(gh[i][:RW_HD])) + hk[i] - gh[i][RW_HD:]
    for i in range(0, len(units), 2):
        _, r0, h = units[i]
        y_s[pl.ds(r0, L), RW_HD * h:RW_HD * (h + 2)] = jnp.concatenate([ys[i], ys[i + 1]], axis=1)


N_RW_PRM = 13


def _rwkv_kernel(is_prompt, nsq, L, grp, n_alias, *refs):
    n_p = nsq if is_prompt else 1
    p_refs = refs[:n_p]
    sh0_ref, s0_ref = refs[n_p:n_p + 2]
    pos = n_p + 2
    prm_refs = refs[pos:pos + N_RW_PRM]
    pos += N_RW_PRM
    tri_ref, ones_ref = refs[pos:pos + 2]
    pos += 2 + n_alias
    o_ref, sh_out_ref, s_out_ref = refs[pos:pos + 3]
    scr = refs[pos + 3:]
    g_s, r_s, k_s = scr[9:12]
    mu = prm_refs[0][...]
    prm = tuple(r[...] for r in prm_refs[1:])
    if is_prompt:
        prev_scr = scr[12]

        @pl.when(pl.program_id(0) == 0)
        def _():
            s_out_ref[...] = s0_ref[...]
            prev_scr[...] = sh0_ref[...]

        ps, pps = [], []
        for s in range(nsq):
            ps.append(p_refs[s][...])
            pps.append(_shift_rows(ps[s], prev_scr[s], 1))
            prev_scr[s] = ps[s][L - SUBLANES:L]
        p = jnp.concatenate(ps, axis=0)
        pp = jnp.concatenate(pps, axis=0)
        sh_out_ref[...] = prev_scr[...]
    else:
        p = p_refs[0][...]
        p3 = p.reshape(nsq, L, p.shape[1])
        pp = jnp.where(_iota(p3.shape, 1) == 0, sh0_ref[...], pltpu.roll(p3, 1, 1)).reshape(p.shape)
        sh_out_ref[...] = p3[:, L - 1:L, :]
        s_out_ref[...] = s0_ref[...]

    r, lw, kmod, kk, bb, v, g = _rw_prep(p + (pp - p) * mu, prm)
    cum = _dot3r(tri_ref[...], lw)
    tot = _dot3r(ones_ref[...], lw)
    e_neg = jnp.exp(-cum)
    e_rem = jnp.exp(tot - cum)
    q_s, rr_s, kt_s, bt_s, kp_s, bp_s, v_s, wl_s, y_s = scr[:9]
    q_s[...] = kk * jnp.exp(cum - lw)
    rr_s[...] = r * jnp.exp(cum)
    kt_s[...] = kmod * e_neg
    bt_s[...] = bb * e_neg
    kp_s[...] = kmod * e_rem
    bp_s[...] = bb * e_rem
    v_s[...] = v
    wl_s[...] = jnp.exp(tot)
    g_s[...], r_s[...], k_s[...] = g, r, kmod

    if is_prompt:
        _rw_chunks(L, [(s, s * L) for s in range(nsq)], scr[:9], s_out_ref)
    else:
        def body(i, c):
            _rw_chunks(L, [(i * grp + j, pl.multiple_of((i * grp + j) * L, SUBLANES)) for j in range(grp)],
                       scr[:9], s_out_ref)
            return c
        lax.fori_loop(0, nsq // grp, body, 0)

    out = _rw_post(y_s[...], r_s[...], k_s[...], v_s[...], g_s[...], prm).astype(BF16)
    if is_prompt:
        for s in range(nsq):
            o_ref[s] = out[s * L:(s + 1) * L]
    else:
        o_ref[...] = out


def _rwkv(P, row0, B, T, sh0, s0, prm_list, s0_layer=None, stacked=None):
    is_prompt = T % CHUNK == 0
    wide = 4 * GROUP_W
    s_shape = (B, RW_H, RW_HD, RW_HD)
    if is_prompt:
        L, nsq, grid = CHUNK, B, (T // CHUNK,)
        rows = nsq * L
        in_specs = [pl.BlockSpec((L, wide), functools.partial(lambda s, c: ((row0 + s * T) // L + c, 0), s))
                    for s in range(nsq)]
        in_specs += [pl.BlockSpec(sh0.shape, lambda c: (0, 0, 0)), pl.BlockSpec(s0.shape, lambda c: (0, 0, 0, 0))]
        out_specs = [pl.BlockSpec((nsq, L, GROUP_W), lambda c: (0, c, 0)),
                     pl.BlockSpec(sh0.shape, lambda c: (0, 0, 0)), pl.BlockSpec(s0.shape, lambda c: (0, 0, 0, 0))]
        out_shape = [jax.ShapeDtypeStruct((nsq, T, GROUP_W), BF16)]
        args = [P] * nsq
    else:
        L, nsq, grid = T, 16, (B // 16,)
        rows = nsq * L
        s_spec = (pl.BlockSpec((nsq, RW_H, RW_HD, RW_HD), lambda i: (i, 0, 0, 0)) if s0_layer is None else
                  pl.BlockSpec((None, nsq, RW_H, RW_HD, RW_HD), lambda i: (s0_layer, i, 0, 0, 0)))
        in_specs = [pl.BlockSpec((rows, wide), lambda i: (row0 // rows + i, 0)),
                    pl.BlockSpec((nsq, 1, wide), lambda i: (i, 0, 0)), s_spec]
        out_specs = [pl.BlockSpec((rows, GROUP_W), lambda i: (i, 0)),
                     pl.BlockSpec((nsq, 1, wide), lambda i: (i, 0, 0)),
                     pl.BlockSpec((nsq, RW_H, RW_HD, RW_HD), lambda i: (i, 0, 0, 0))]
        out_shape = [jax.ShapeDtypeStruct((B * T, GROUP_W), BF16)]
        args = [P]
    seq_of_row = jnp.arange(rows) // L
    same = seq_of_row[:, None] == seq_of_row[None, :]
    tri = (same & (jnp.arange(rows)[:, None] >= jnp.arange(rows)[None, :])).astype(BF16)
    consts = list(prm_list) + [tri, same.astype(BF16)]
    in_specs += [pl.BlockSpec(x.shape, lambda *a: (0, 0)) for x in consts]
    out_shape += [jax.ShapeDtypeStruct(sh0.shape, F32), jax.ShapeDtypeStruct(s_shape, F32)]
    args += [sh0, s0] + consts
    aliases = {}
    if stacked is not None:
        depth, layer, prev = stacked
        blk = out_specs[2].block_shape
        out_specs[2] = pl.BlockSpec((None,) + tuple(blk), (lambda c: (layer, 0, 0, 0, 0)) if is_prompt
                                    else (lambda i: (layer, i, 0, 0, 0)))
        out_shape[2] = jax.ShapeDtypeStruct((depth,) + s_shape, F32)
        if prev is not None:
            aliases[len(args)] = 2
            in_specs.append(pl.BlockSpec(memory_space=pl.ANY))
            args.append(prev)
    scratch = [pltpu.VMEM((rows, GROUP_W), F32)] * 12
    if is_prompt:
        scratch.append(pltpu.VMEM(sh0.shape, F32))
    outs = pl.pallas_call(
        functools.partial(_rwkv_kernel, is_prompt, nsq, L, 2, len(aliases)),
        grid=grid,
        in_specs=in_specs,
        out_specs=out_specs,
        out_shape=out_shape,
        input_output_aliases=aliases,
        scratch_shapes=scratch,
        compiler_params=_cparams(1),
        name="rwkv_prompt" if is_prompt else "rwkv_sample",
    )(*args)
    return outs[0].reshape(B * T, GROUP_W), outs[1], outs[2]


SEQ_GROUP = 4


def _causal(L):
    return _iota((L, L), 0) >= _iota((L, L), 1)


def _heads(units):
    return [(u, h) for u in range(len(units)) for h in range(HEADS)]


def _hs(h):
    return slice(h * HEAD_D, (h + 1) * HEAD_D)


def _ret_units(L, units, S_ref, consts):
    cosv, sinv, dm_ref, qd_ref, kd_ref, gn, c_dec = consts
    uh = _heads(units)

    def rot(x):
        return x * cosv + pltpu.roll(x, HEAD_D // 2, 1) * sinv

    qr = [rot(units[u][0](h * HEAD_D, HEAD_D)) for u, h in uh]
    kr = [rot(units[u][0](GROUP_W + h * HEAD_D, HEAD_D)) * (HEAD_D ** -0.5) for u, h in uh]
    vb = [_bf(units[u][0](2 * GROUP_W + h * HEAD_D, HEAD_D)) for u, h in uh]
    S = [S_ref[units[u][3], h] for u, h in uh]
    A = [_nt(_bf(qr[i]), _bf(kr[i])) * dm_ref[h] for i, (u, h) in enumerate(uh)]
    o = [_nn(_bf(A[i]), vb[i]) + _nn(_bf(qr[i] * qd_ref[h]), _bf(S[i])) for i, (u, h) in enumerate(uh)]
    for i, (u, h) in enumerate(uh):
        S_ref[units[u][3], h] = c_dec[h] * S[i] + _tn(_bf(kr[i] * kd_ref[h]), vb[i])
    for i, (u, h) in enumerate(uh):
        oc = o[i] - jnp.mean(o[i], axis=1, keepdims=True)
        on = oc * lax.rsqrt(jnp.mean(oc * oc, axis=1, keepdims=True) + GN_EPS) * gn[:, _hs(h)]
        g = units[u][0](3 * GROUP_W + h * HEAD_D, HEAD_D)
        units[u][2](h * HEAD_D, _bf(_silu(g) * on))


def _ml_units(L, units, st_refs, consts):
    C_ref, n_ref, m_ref, cv_ref = st_refs
    ltri, ones_l, conv_w, conv_b, bias_sm, gn = consts
    uh = _heads(units)
    conv, xs, bc_all = [], [], []
    for get, get_sm, _, sidx in units:
        x = get(0, 2 * GROUP_W)
        prev8 = cv_ref[sidx]
        c = (conv_b + _shift_rows(x, prev8, 3) * conv_w[0:1] + _shift_rows(x, prev8, 2) * conv_w[1:2]
             + _shift_rows(x, prev8, 1) * conv_w[2:3] + x * conv_w[3:4])
        cv_ref[sidx] = x[L - SUBLANES:L]
        conv.append(_silu(c))
        xs.append(get_sm(LANES, LANES) + bias_sm)
        bc_all.append(_dot3r(ltri, _log_sigmoid(xs[-1])))
    causal = _causal(L)
    lane0 = _iota((L, LANES), 1) == 0
    q = [conv[u][:, _hs(h)] for u, h in uh]
    k = [conv[u][:, GROUP_W + h * HEAD_D:GROUP_W + (h + 1) * HEAD_D] * (HEAD_D ** -0.5) for u, h in uh]
    vb = [_bf(units[u][0](2 * GROUP_W + h * HEAD_D, HEAD_D)) for u, h in uh]
    ig = [xs[u][:, SM_MLI - LANES + h:SM_MLI - LANES + h + 1] for u, h in uh]
    bc = [bc_all[u][:, SM_MLF - LANES + h:SM_MLF - LANES + h + 1] for u, h in uh]
    m_prev = [m_ref[units[u][3], h][:, 0:1] for u, h in uh]
    row = [_dot3r(ones_l, jnp.where(lane0, ig[i] - bc[i], 0.0), "nt") for i in range(len(uh))]
    dlog = [jnp.where(causal, bc[i] + row[i], -jnp.inf) for i in range(len(uh))]
    inter = [bc[i] + m_prev[i] for i in range(len(uh))]
    m_t = [jnp.maximum(inter[i], jnp.max(dlog[i], axis=1, keepdims=True)) for i in range(len(uh))]
    dmat = [jnp.exp(dlog[i] - m_t[i]) for i in range(len(uh))]
    s_in = [jnp.exp(inter[i] - m_t[i]) for i in range(len(uh))]
    C = [C_ref[units[u][3], h] for u, h in uh]
    n = [n_ref[units[u][3], h] for u, h in uh]
    qb = [_bf(x) for x in q]
    A = [_nt(qb[i], _bf(k[i])) * dmat[i] for i in range(len(uh))]
    num = [_nn(_bf(A[i]), vb[i]) + s_in[i] * _nt(qb[i], _bf(C[i])) for i in range(len(uh))]
    den = [jnp.sum(A[i], axis=1, keepdims=True) + s_in[i] * jnp.sum(q[i] * n[i], axis=1, keepdims=True)
           for i in range(len(uh))]
    hh = [num[i] / jnp.maximum(jnp.abs(den[i]), jnp.exp(-m_t[i])) for i in range(len(uh))]
    for i, (u, h) in enumerate(uh):
        sidx = units[u][3]
        m_new = m_t[i][L - 1:L, :]
        b_last = bc[i][L - 1:L, :]
        carry = jnp.exp(b_last + m_prev[i] - m_new)
        wk = jnp.exp(b_last - bc[i] + ig[i] - m_new) * k[i]
        C_ref[sidx, h] = carry * C[i] + _tn(vb[i], _bf(wk))
        n_ref[sidx, h] = carry * n[i] + jnp.sum(wk, axis=0, keepdims=True)
        m_ref[sidx, h] = jnp.broadcast_to(m_new, (1, LANES))
    for i, (u, h) in enumerate(uh):
        hc = hh[i] - jnp.mean(hh[i], axis=1, keepdims=True)
        hn = hc * lax.rsqrt(jnp.mean(hc * hc, axis=1, keepdims=True) + GN_EPS) * gn[:, _hs(h)]
        o_pre = units[u][0](3 * GROUP_W + h * HEAD_D, HEAD_D)
        units[u][2](h * HEAD_D, _bf(_sigmoid(o_pre) * hn))


def _gl_units(L, units, S_ref, consts):
    ltri, a2p, ab, gn = consts
    uh = _heads(units)
    mid = L // 2 - 1
    q_in, k_in, q_st, k_st, e_last = [], [], [], [], []
    for get, get_sm, _, _ in units:
        la = _log_sigmoid(_nn(_bf(get_sm(LANES, LANES)), a2p) + ab) / GL_TAU
        bc = _dot3r(ltri, la)
        b_mid = bc[mid:mid + 1, :]
        b_last = bc[L - 1:L, :]
        qall = get(0, GL_DK * HEADS) * (GL_DK ** -0.5)
        kall = get(GL_DK * HEADS, GL_DK * HEADS)
        q_in.append(_bf(qall * jnp.exp(jnp.minimum(bc - b_mid, 80.0))))
        k_in.append(_bf(kall * jnp.exp(jnp.minimum(b_mid - bc, 80.0))))
        q_st.append(_bf(qall * jnp.exp(bc)))
        k_st.append(_bf(kall * jnp.exp(b_last - bc)))
        e_last.append(jnp.exp(b_last))
    causal = _causal(L)
    eye = _iota((GL_DK, GL_DK), 0) == _iota((GL_DK, GL_DK), 1)

    def ks(h):
        return slice(h * GL_DK, (h + 1) * GL_DK)

    vb = [_bf(units[u][0](GROUP_W + h * HEAD_D, HEAD_D)) for u, h in uh]
    S = [S_ref[units[u][3], h] for u, h in uh]
    A = [jnp.where(causal, _nt(q_in[u][:, ks(h)], k_in[u][:, ks(h)]), 0.0) for u, h in uh]
    o = [_nn(_bf(A[i]), vb[i]) + _nn(q_st[u][:, ks(h)], _bf(S[i])) for i, (u, h) in enumerate(uh)]
    for i, (u, h) in enumerate(uh):
        e_col = jnp.sum(jnp.where(eye, e_last[u][:, ks(h)], 0.0), axis=1, keepdims=True)
        S_ref[units[u][3], h] = e_col * S[i] + _tn(k_st[u][:, ks(h)], vb[i])
    for i, (u, h) in enumerate(uh):
        on = o[i] * lax.rsqrt(jnp.mean(o[i] * o[i], axis=1, keepdims=True) + GN_EPS) * gn[:, _hs(h)]
        g = units[u][0](2 * GROUP_W + h * HEAD_D, HEAD_D)
        units[u][2](h * HEAD_D, _bf(_silu(g) * on))


def _chunk_kernel(kind, nslab, nunit, L, n_state, n_const, n_alias, has_sm, *refs):
    pos = 0
    p_refs = refs[pos:pos + nslab]; pos += nslab
    sm_refs = refs[pos:pos + (nslab if has_sm else 0)]; pos += (nslab if has_sm else 0)
    st_in = refs[pos:pos + n_state]; pos += n_state
    c_refs = refs[pos:pos + n_const]; pos += n_const + n_alias
    o_ref = refs[pos]; pos += 1
    st_out = refs[pos:pos + n_state]; pos += n_state
    c = pl.program_id(1)

    @pl.when(c == 0)
    def _():
        for a, b in zip(st_in, st_out):
            b[...] = a[...]

    def unit(slab, seq, sidx):
        r0 = seq * L if isinstance(seq, int) else pl.multiple_of(seq * L, SUBLANES)

        def get(col, width):
            return p_refs[slab][pl.ds(r0, L), col:col + width]

        def get_sm(col, width):
            return sm_refs[slab][pl.ds(r0, L), col:col + width]

        def put(col, val):
            o_ref[slab, pl.ds(r0, L), col:col + val.shape[1]] = val

        return (get, get_sm, put, sidx)

    def run(units):
        if kind == "ret":
            cosv, sinv = c_refs[0][...], c_refs[1][...]
            consts = (cosv, sinv, c_refs[2], c_refs[3], c_refs[4], c_refs[5][...], _RT_CDEC[L])
            _ret_units(L, units, st_out[0], consts)
        elif kind == "ml":
            _ml_units(L, units, st_out, tuple(r[...] for r in c_refs))
        else:
            _gl_units(L, units, st_out[0], tuple(r[...] for r in c_refs))

    if nunit == 1:
        run([unit(s, 0, s) for s in range(nslab)])
    else:
        def body(i, carry):
            run([unit(0, i * SEQ_GROUP + j, i * SEQ_GROUP + j) for j in range(SEQ_GROUP)])
            return carry
        lax.fori_loop(0, nunit // SEQ_GROUP, body, 0)


def _rt_log_gamma():
    return [math.log1p(-(2.0 ** (-5.0 - h))) for h in range(HEADS)]


_RT_CDEC = {L: [math.exp(L * lg) for lg in _rt_log_gamma()] for L in (8, CHUNK)}


def _chunk_call(kind, P, col_blk, col_w, sm, row0, nseq, T, states, consts, const_chunked, name, stacked=None):
    has_sm = sm
    if T % CHUNK == 0:
        L, nslab, nunit = CHUNK, nseq, 1
        grid = (1, T // L)
        rows = L
        def pmap(s, i, c):
            return ((row0 + s * T) // L + c, col_blk)
        def smap(s, i, c):
            return ((row0 + s * T) // L + c, 3)
        def omap(i, c):
            return (0, c, 0)
        sblk = nseq
    else:
        L, nslab, nunit = T, 1, 16
        grid = (nseq // nunit, 1)
        rows = L * nunit
        def pmap(s, i, c):
            return (row0 // rows + i, col_blk)
        def smap(s, i, c):
            return (row0 // rows + i, 3)
        def omap(i, c):
            return (0, i, 0)
        sblk = nunit
    in_specs = [pl.BlockSpec((rows, col_w), functools.partial(pmap, s)) for s in range(nslab)]
    args = [P] * nslab
    if has_sm:
        in_specs += [pl.BlockSpec((rows, GROUP_W), functools.partial(smap, s)) for s in range(nslab)]
        args += [P] * nslab
    st_specs, st_shapes = [], []
    for st in states:
        arr, layer = st if isinstance(st, tuple) else (st, None)
        tail = arr.shape[1:] if layer is None else arr.shape[2:]
        nd = len(tail)
        st_specs.append(pl.BlockSpec((sblk,) + tail, lambda i, c, nd=nd: (i,) + (0,) * nd))
        st_shapes.append(jax.ShapeDtypeStruct((nseq,) + tail, F32))
        if layer is None:
            in_specs.append(st_specs[-1])
        else:
            in_specs.append(pl.BlockSpec((None, sblk) + tail, lambda i, c, nd=nd, layer=layer: (layer, i) + (0,) * nd))
        args.append(arr)
    for cst, chunked in zip(consts, const_chunked):
        if chunked:
            in_specs.append(pl.BlockSpec((None,) + cst.shape[1:], lambda i, c: (c, 0, 0)))
        else:
            in_specs.append(pl.BlockSpec(cst.shape, lambda i, c, nd=cst.ndim: (0,) * nd))
        args.append(cst)
    aliases = {}
    if stacked is not None:
        depth, layer, prev = stacked
        tail = st_shapes[0].shape[1:]
        st_specs[0] = pl.BlockSpec((None, sblk) + tail, lambda i, c, nd=len(tail): (layer, i) + (0,) * nd)
        st_shapes[0] = jax.ShapeDtypeStruct((depth, nseq) + tail, F32)
        if prev is not None:
            aliases[len(args)] = 1
            in_specs.append(pl.BlockSpec(memory_space=pl.ANY))
            args.append(prev)
    out_specs = [pl.BlockSpec((nslab, rows, GROUP_W), omap)] + st_specs
    out_shape = [jax.ShapeDtypeStruct((nslab, nseq * T // nslab, GROUP_W), BF16)] + st_shapes
    outs = pl.pallas_call(
        functools.partial(_chunk_kernel, kind, nslab, nunit, L, len(states), len(consts), len(aliases), has_sm),
        grid=grid,
        in_specs=in_specs,
        out_specs=out_specs,
        out_shape=out_shape,
        input_output_aliases=aliases,
        compiler_params=_cparams(2),
        name=name,
    )(*args)
    return outs[0].reshape(nseq * T, GROUP_W), outs[1:]


def _pack_w_in(w_in):
    z = jnp.zeros(w_in.shape[:2] + (GROUP_W - 184,), w_in.dtype)
    parts = [w_in[..., 0:1696], w_in[..., 3744:3752], w_in[..., 6824:6840], z,
             w_in[..., 1696:3744], w_in[..., 3752:5800], w_in[..., 5800:6824], w_in[..., 6840:7352]]
    return jnp.concatenate(parts, axis=-1).astype(BF16)


def _row(v):
    return v.reshape(1, -1).astype(F32)


def _rows_at(mat, r0, rows_total):
    z = jnp.zeros((rows_total, mat.shape[1]), F32).at[r0:r0 + mat.shape[0]].set(mat)
    return z.astype(BF16)


def _ret_consts(pos0, T, L, gn):
    pos = pos0 + jnp.arange(T, dtype=F32)
    inv = 1.0 / (10000.0 ** jnp.linspace(0.0, 1.0, HEAD_D // 2, dtype=F32))
    ang = pos[:, None] * inv[None, :]
    cos, sin = jnp.cos(ang), jnp.sin(ang)
    cos2 = jnp.concatenate([cos, cos], axis=1).reshape(T // L, L, HEAD_D)
    sin2 = jnp.concatenate([-sin, sin], axis=1).reshape(T // L, L, HEAD_D)
    lg = jnp.log1p(-jnp.exp2(-5.0 - jnp.arange(HEADS, dtype=F32)))
    idx = jnp.arange(L, dtype=F32)
    rel = idx[:, None] - idx[None, :]
    dmat = jnp.where(rel >= 0, jnp.exp(jnp.maximum(rel, 0.0) * lg[:, None, None]), 0.0)
    q_dec = jnp.broadcast_to(jnp.exp((idx + 1.0) * lg[:, None])[..., None], (HEADS, L, HEAD_D))
    k_dec = jnp.broadcast_to(jnp.exp((L - 1.0 - idx) * lg[:, None])[..., None], (HEADS, L, HEAD_D))
    return [cos2, sin2, dmat, q_dec, k_dec, _row(gn)], [True, True, False, False, False, False]


def _tri(L):
    return jnp.tril(jnp.ones((L, L), F32)).astype(BF16)


def _moe_plan(ids, tm, n_tiles):
    e = ids.reshape(-1)
    n = e.shape[0]
    onehot = (e[:, None] == jnp.arange(N_EXPERTS, dtype=jnp.int32)[None, :]).astype(jnp.int32)
    rank = jnp.sum((jnp.cumsum(onehot, axis=0) - 1) * onehot, axis=1)
    counts = jnp.sum(onehot, axis=0)
    padded = ((counts + tm - 1) // tm) * tm
    ends = jnp.cumsum(padded)
    starts = ends - padded
    dest = jnp.sum(starts[None, :] * onehot, axis=1) + rank
    row_src = jnp.zeros((n_tiles * tm,), jnp.int32).at[dest].set(jnp.arange(n, dtype=jnp.int32) // 2)
    tile_start = jnp.arange(n_tiles, dtype=jnp.int32) * tm
    tile_expert = jnp.minimum(jnp.sum((tile_start[:, None] >= ends[None, :]).astype(jnp.int32), axis=1),
                              N_EXPERTS - 1)
    return row_src, dest.reshape(-1, 2), tile_expert, (ends[-1:] // tm).astype(jnp.int32)


def kernel(x_prompt, x_sample, state_rw_shift, state_rw_wkv, state_ml_conv, state_ml_C, state_ml_n, state_ml_m, state_rt_S, state_gl_S, w_in, rw_mu, rw_w0, rw_w2, rw_a0, rw_a2, rw_g2, rw_kk, rw_ka, rw_rk, rw_ln_w, rw_ln_b, ml_conv_w, ml_conv_b, ml_bi, ml_bf, ml_gn_w, rt_gn_w, gl_a2, gl_ab, gl_gn_w, w_out, ln1_w, ln1_b, ln2_w, ln2_b, ffn_w1, ffn_w3, ffn_w2, moe_router, moe_w1, moe_w3, moe_w2):
    Bp, Tp = x_prompt.shape[:2]
    Bs, Ts = x_sample.shape[:2]
    Mp, Ms = Bp * Tp, Bs * Ts
    M = Mp + Ms
    depth = w_in.shape[0]
    x = [x_prompt.reshape(Mp, D_MODEL), x_sample.reshape(Ms, D_MODEL)]
    xb = [v.astype(BF16) for v in x]
    w_in_p = _pack_w_in(w_in)
    w_out_b = w_out.astype(BF16)
    blockdiag = (jnp.arange(LANES)[:, None] // RW_HD) == (jnp.arange(LANES)[None, :] // RW_HD)
    g64 = blockdiag.astype(BF16)
    g64m = (blockdiag.astype(F32) / RW_HD).astype(BF16)
    pad_sh = 4 * GROUP_W - RW_COLS
    outs = {k: [[], []] for k in ("sh", "conv", "n", "m")}
    big = {k: [None, None] for k in ("wkv", "C", "rt", "gl")}

    for l in range(depth):
        P = [_mm([v], w_in_p, l, GROUP_W, "mm_in") for v in xb]

        prm = [_row(jnp.pad(rw_mu[l], (0, pad_sh))), _row(rw_w0[l]), _row(rw_a0[l]),
               _rows_at(rw_w2[l], 0, 2 * LANES), _rows_at(rw_a2[l], 32, 2 * LANES), _rows_at(rw_g2[l], 64, 2 * LANES),
               _row(rw_kk[l]), _row(rw_ka[l]), _row(rw_rk[l]), _row(rw_ln_w[l]), _row(rw_ln_b[l]), g64, g64m]
        o_rw_p, sh_p, big["wkv"][0] = _rwkv(P[0], 0, Bp, Tp, jnp.zeros((Bp, SUBLANES, 4 * GROUP_W), F32),
                                             jnp.zeros((Bp, RW_H, RW_HD, RW_HD), F32), prm,
                                             stacked=(depth, l, big["wkv"][0]))
        sh_in = jnp.pad(state_rw_shift[l], ((0, 0), (0, pad_sh)))[:, None, :]
        o_rw_s, sh_s, big["wkv"][1] = _rwkv(P[1], 0, Bs, Ts, sh_in, state_rw_wkv, prm, s0_layer=l,
                                             stacked=(depth, l, big["wkv"][1]))
        outs["sh"][0].append(sh_p[:, SUBLANES - 1, :RW_COLS])
        outs["sh"][1].append(sh_s[:, 0, :RW_COLS])

        bias_sm = jnp.zeros((LANES,), F32).at[SM_MLI - LANES:SM_MLI - LANES + HEADS].set(ml_bi[l])
        bias_sm = bias_sm.at[SM_MLF - LANES:SM_MLF - LANES + HEADS].set(ml_bf[l])

        def ml_consts(L):
            return [_tri(L), jnp.ones((L, LANES), BF16), ml_conv_w[l], _row(ml_conv_b[l]), _row(bias_sm), _row(ml_gn_w[l])]

        def ml_states(C, n, m, conv):
            b = n.shape[0]
            return [C, n[:, :, None, :], jnp.broadcast_to(m[:, :, None, None], (b, HEADS, 1, LANES)),
                    jnp.pad(conv, ((0, 0), (SUBLANES - conv.shape[1], 0), (0, 0)))]

        zp = lambda *s: jnp.zeros((Bp,) + s, F32)
        o_ml_p, st_p = _chunk_call("ml", P[0], 1, 4 * GROUP_W, True, 0, Bp, Tp,
                                   ml_states(zp(HEADS, HEAD_D, HEAD_D), zp(HEADS, HEAD_D), zp(HEADS), zp(3, 2 * GROUP_W)),
                                   ml_consts(CHUNK), [False] * 6, "mlstm_prompt", stacked=(depth, l, big["C"][0]))
        o_ml_s, st_s = _chunk_call("ml", P[1], 1, 4 * GROUP_W, True, 0, Bs, Ts,
                                   ml_states((state_ml_C, l), state_ml_n[l], state_ml_m[l], state_ml_conv[l]),
                                   ml_consts(Ts), [False] * 6, "mlstm_sample", stacked=(depth, l, big["C"][1]))
        for g, st in enumerate((st_p, st_s)):
            big["C"][g] = st[0]
            outs["n"][g].append(st[1][:, :, 0, :])
            outs["m"][g].append(st[2][:, :, 0, 0])
            outs["conv"][g].append(st[3][:, SUBLANES - 3:, :])

        c_p, ch_p = _ret_consts(0.0, Tp, CHUNK, rt_gn_w[l])
        c_s, ch_s = _ret_consts(float(PAST_LEN), Ts, Ts, rt_gn_w[l])
        o_rt_p, (big["rt"][0],) = _chunk_call("ret", P[0], 2, 4 * GROUP_W, False, 0, Bp, Tp, [zp(HEADS, HEAD_D, HEAD_D)],
                                              c_p, ch_p, "ret_prompt", stacked=(depth, l, big["rt"][0]))
        o_rt_s, (big["rt"][1],) = _chunk_call("ret", P[1], 2, 4 * GROUP_W, False, 0, Bs, Ts, [(state_rt_S, l)],
                                              c_s, ch_s, "ret_sample", stacked=(depth, l, big["rt"][1]))

        def gl_consts(L):
            return [_tri(L), _rows_at(gl_a2[l], SM_GLA - LANES, LANES), _row(gl_ab[l]), _row(gl_gn_w[l])]

        o_gl_p, (big["gl"][0],) = _chunk_call("gl", P[0], 4, 3 * GROUP_W, True, 0, Bp, Tp, [zp(HEADS, GL_DK, HEAD_D)],
                                              gl_consts(CHUNK), [False] * 4, "gla_prompt",
                                              stacked=(depth, l, big["gl"][0]))
        o_gl_s, (big["gl"][1],) = _chunk_call("gl", P[1], 4, 3 * GROUP_W, True, 0, Bs, Ts, [(state_gl_S, l)],
                                              gl_consts(Ts), [False] * 4, "gla_sample",
                                              stacked=(depth, l, big["gl"][1]))

        mixed = ((o_rw_p, o_ml_p, o_rt_p, o_gl_p), (o_rw_s, o_ml_s, o_rt_s, o_gl_s))
        for g in range(2):
            mix = _mm(list(mixed[g]), w_out_b, l, GROUP_W, "mm_out")
            x[g], xb[g] = _add_ln(x[g], [mix], None, ln1_w[l], ln1_b[l])

        if l % 2 == 0:
            w1, w3, w2 = (w[l // 2][None].astype(BF16) for w in (ffn_w1, ffn_w3, ffn_w2))
            for g in range(2):
                rows = x[g].shape[0]
                tm = _pick(rows, (512, 256, 128))
                y = _ffn(xb[g], jnp.zeros((rows // tm,), jnp.int32), jnp.full((1,), rows // tm, jnp.int32), w1, w3, w2, tm)
                x[g], xb[g] = _add_ln(x[g], [y], None, ln2_w[l], ln2_b[l])
        else:
            tm = _pick(M, (512, 256, 128))
            n_tiles = 2 * M // tm + N_EXPERTS
            router_pad = jnp.pad(moe_router[l // 2], ((0, 0), (0, LANES - N_EXPERTS)))
            routed = [_router(v, router_pad) for v in x]
            gates = [r[0] for r in routed]
            ids = jnp.concatenate([r[1][:, :2] for r in routed], axis=0)
            row_src, dest, tile_expert, n_used = _moe_plan(ids, tm, n_tiles)
            xs = jnp.take(jnp.concatenate(xb, axis=0), row_src, axis=0, mode="clip")
            ysorted = _ffn(xs, tile_expert, n_used, moe_w1[l // 2].astype(BF16), moe_w3[l // 2].astype(BF16),
                           moe_w2[l // 2].astype(BF16), tm)
            for g, (r0, r1) in enumerate(((0, Mp), (Mp, M))):
                ys = [jnp.take(ysorted, dest[r0:r1, k], axis=0, mode="clip") for k in range(2)]
                x[g], xb[g] = _add_ln(x[g], ys, [gates[g][:, 0:1], gates[g][:, 1:2]], ln2_w[l], ln2_b[l])

    res = [x[0].reshape(Bp, Tp, D_MODEL), x[1].reshape(Bs, Ts, D_MODEL)]
    for key in ("sh", "wkv", "conv", "C", "n", "m", "rt", "gl"):
        res += list(big[key]) if key in big else [jnp.stack(outs[key][0]), jnp.stack(outs[key][1])]
    return tuple(res)
```

```python
import functools
import math

import jax
import jax.numpy as jnp
from jax import lax
from jax.experimental import pallas as pl
from jax.experimental.pallas import tpu as pltpu

F32 = jnp.float32
BF16 = jnp.bfloat16

D_MODEL = 2048
GROUP_W = 512
RW_COLS = 1696
RW_H, RW_HD = 8, 64
HEADS, HEAD_D = 4, 128
GL_DK = 64
CHUNK = 64
ALPHA = 4.0 ** 0.25
LN_EPS = 1e-5
GN_EPS = 1e-5
RW_GN_EPS = 64e-5
GL_TAU = 16.0
PAST_LEN = 16384
N_EXPERTS = 8

LANES = 128
SUBLANES = 8
VMEM_LIMIT = 48 * 1024 * 1024

N_PACKED = 15 * GROUP_W
SM_MLI, SM_MLF, SM_GLA = 160, 164, 168


def _nn(a, b):
    return jnp.dot(a, b, preferred_element_type=F32)


def _nt(a, b):
    return lax.dot_general(a, b, (((1,), (1,)), ((), ())), preferred_element_type=F32)


def _tn(a, b):
    return lax.dot_general(a, b, (((0,), (0,)), ((), ())), preferred_element_type=F32)


def _bf(x):
    return x.astype(BF16)


def _split3(x):
    hi = x.astype(BF16)
    r1 = x - hi.astype(F32)
    mid = r1.astype(BF16)
    lo = (r1 - mid.astype(F32)).astype(BF16)
    return hi, mid, lo


def _dot3(x, m, kind="nn"):
    f = {"nn": _nn, "nt": _nt, "tn": _tn}[kind]
    hi, mid, lo = _split3(x)
    return f(hi, m) + f(mid, m) + f(lo, m)


def _dot3r(m, x, kind="nn"):
    f = {"nn": _nn, "nt": _nt, "tn": _tn}[kind]
    hi, mid, lo = _split3(x)
    return f(m, hi) + f(m, mid) + f(m, lo)


def _sigmoid(x):
    return 1.0 / (1.0 + jnp.exp(-x))


def _softplus(x):
    return jnp.maximum(x, 0.0) + jnp.log1p(jnp.exp(-jnp.abs(x)))


def _log_sigmoid(x):
    return -_softplus(-x)


def _silu(x):
    return x * _sigmoid(x)


def _iota(shape, axis):
    return lax.broadcasted_iota(jnp.int32, shape, axis)


def _shift_rows(x, prev8, j):
    rows = x.shape[0]
    xr = pltpu.roll(x, j, 0)
    pr = pltpu.roll(prev8, j, 0)
    first = jnp.where(_iota((SUBLANES, x.shape[1]), 0) < j, pr, xr[0:SUBLANES])
    if rows == SUBLANES:
        return first
    return jnp.concatenate([first, xr[SUBLANES:]], axis=0)


def _pick(n, cands):
    return next(c for c in cands if n % c == 0)


def _cparams(n_axes):
    return pltpu.CompilerParams(dimension_semantics=("arbitrary",) * n_axes, vmem_limit_bytes=VMEM_LIMIT)


def _mm_kernel(n_in, *refs):
    xs, ws, o_ref = refs[:n_in], refs[n_in:2 * n_in], refs[2 * n_in]
    acc = _nn(xs[0][...], ws[0][...])
    for x_ref, w_ref in zip(xs[1:], ws[1:]):
        acc = acc + _nn(x_ref[...], w_ref[...])
    o_ref[...] = acc


def _mm(xs, w, layer, tn, name):
    M, N = xs[0].shape[0], w.shape[2]
    n_in = len(xs)
    tm = _pick(M, (1024, 512, 256, 128))
    in_specs = ([pl.BlockSpec((tm, x.shape[1]), lambda j, i: (i, 0)) for x in xs]
                + [pl.BlockSpec((None, x.shape[1], tn), functools.partial(lambda g, j, i: (layer, g, j), g))
                   for g, x in enumerate(xs)])
    return pl.pallas_call(
        functools.partial(_mm_kernel, n_in),
        grid=(N // tn, M // tm),
        in_specs=in_specs,
        out_specs=pl.BlockSpec((tm, tn), lambda j, i: (i, j)),
        out_shape=jax.ShapeDtypeStruct((M, N), F32),
        compiler_params=_cparams(2),
        name=name,
    )(*xs, *([w] * n_in))


def _add_ln_kernel(n_y, gated, x_ref, *refs):
    ys = refs[:n_y]
    gs = refs[n_y:2 * n_y] if gated else ()
    w_ref, b_ref, o_ref, ob_ref = refs[-4:]
    z = ALPHA * x_ref[...]
    for i, y_ref in enumerate(ys):
        y = y_ref[...]
        if gated:
            y = gs[i][...] * y
        z = z + y
    mu = jnp.mean(z, axis=-1, keepdims=True)
    zc = z - mu
    var = jnp.mean(zc * zc, axis=-1, keepdims=True)
    out = zc * lax.rsqrt(var + LN_EPS) * w_ref[...] + b_ref[...]
    o_ref[...] = out
    ob_ref[...] = out.astype(BF16)


def _add_ln(x, ys, gates, w, b):
    M = x.shape[0]
    tm = _pick(M, (256, 128))
    gated = gates is not None
    row = pl.BlockSpec((tm, D_MODEL), lambda i: (i, 0))
    in_specs = [row] + [row] * len(ys)
    args = [x] + list(ys)
    if gated:
        in_specs += [pl.BlockSpec((tm, 1), lambda i: (i, 0))] * len(ys)
        args += list(gates)
    in_specs += [pl.BlockSpec((1, D_MODEL), lambda i: (0, 0))] * 2
    args += [w.reshape(1, D_MODEL), b.reshape(1, D_MODEL)]
    return pl.pallas_call(
        functools.partial(_add_ln_kernel, len(ys), gated),
        grid=(M // tm,),
        in_specs=in_specs,
        out_specs=[row, row],
        out_shape=[jax.ShapeDtypeStruct((M, D_MODEL), F32), jax.ShapeDtypeStruct((M, D_MODEL), BF16)],
        compiler_params=_cparams(1),
        name="add_ln",
    )(*args)


def _ffn_up_kernel(te_ref, nu_ref, x_ref, w1_ref, w3_ref, h_ref):
    @pl.when(pl.program_id(0) < nu_ref[0])
    def _():
        x = x_ref[...].astype(BF16)
        h_ref[...] = (_silu(_nn(x, w1_ref[...])) * _nn(x, w3_ref[...])).astype(BF16)


def _ffn_down_kernel(te_ref, nu_ref, h_ref, w2_ref, o_ref):
    @pl.when(pl.program_id(0) < nu_ref[0])
    def _():
        o_ref[...] = _nn(h_ref[...], w2_ref[...])


FFN_TF = 1408
FFN_TN = 512


def _ffn(xs, tile_expert, n_used, w1, w3, w2, tm):
    R = xs.shape[0]
    F = w1.shape[2]
    nf, nn = F // FFN_TF, D_MODEL // FFN_TN

    def last(i, j, nu, n):
        return jnp.where(i < nu[0], j, n - 1)

    up_spec = pltpu.PrefetchScalarGridSpec(
        num_scalar_prefetch=2,
        grid=(R // tm, nf),
        in_specs=[
            pl.BlockSpec((tm, D_MODEL), lambda i, f, te, nu: (i, 0)),
            pl.BlockSpec((None, D_MODEL, FFN_TF), lambda i, f, te, nu: (te[i], 0, last(i, f, nu, nf))),
            pl.BlockSpec((None, D_MODEL, FFN_TF), lambda i, f, te, nu: (te[i], 0, last(i, f, nu, nf))),
        ],
        out_specs=pl.BlockSpec((tm, FFN_TF), lambda i, f, te, nu: (i, f)),
    )
    h = pl.pallas_call(
        _ffn_up_kernel,
        grid_spec=up_spec,
        out_shape=jax.ShapeDtypeStruct((R, F), BF16),
        compiler_params=_cparams(2),
        name="ffn_up",
    )(tile_expert, n_used, xs, w1, w3)
    down_spec = pltpu.PrefetchScalarGridSpec(
        num_scalar_prefetch=2,
        grid=(R // tm, nn),
        in_specs=[
            pl.BlockSpec((tm, F), lambda i, n, te, nu: (i, 0)),
            pl.BlockSpec((None, F, FFN_TN), lambda i, n, te, nu: (te[i], 0, last(i, n, nu, nn))),
        ],
        out_specs=pl.BlockSpec((tm, FFN_TN), lambda i, n, te, nu: (i, n)),
    )
    return pl.pallas_call(
        _ffn_down_kernel,
        grid_spec=down_spec,
        out_shape=jax.ShapeDtypeStruct((R, D_MODEL), F32),
        compiler_params=_cparams(2),
        name="ffn_down",
    )(tile_expert, n_used, h, w2)


def _router_kernel(x_ref, r_ref, g_ref, i_ref):
    xh, xm, xl = _split3(x_ref[...])
    rh, rm, rl = _split3(r_ref[...])
    logits = (_nn(xh, rh) + _nn(xh, rm) + _nn(xm, rh)) + (_nn(xh, rl) + _nn(xl, rh) + _nn(xm, rm))
    shape = logits.shape
    lane = _iota(shape, 1).astype(F32)
    neg = jnp.float32(-jnp.inf)
    l1 = jnp.where(lane < N_EXPERTS, logits, neg)
    m1 = jnp.max(l1, axis=1, keepdims=True)
    i1 = jnp.min(jnp.where(l1 == m1, lane, float(LANES)), axis=1, keepdims=True)
    l2 = jnp.where(lane == i1, neg, l1)
    m2 = jnp.max(l2, axis=1, keepdims=True)
    i2 = jnp.min(jnp.where(l2 == m2, lane, float(LANES)), axis=1, keepdims=True)
    e = jnp.exp(m2 - m1)
    g1 = 1.0 / (1.0 + e)
    g2 = e / (1.0 + e)
    g_ref[...] = jnp.where(lane == 0.0, g1, jnp.where(lane == 1.0, g2, 0.0))
    i_ref[...] = jnp.where(lane == 0.0, i1, jnp.where(lane == 1.0, i2, 0.0)).astype(jnp.int32)


def _router(x, router_pad):
    M = x.shape[0]
    tm = _pick(M, (512, 256, 128))
    return pl.pallas_call(
        _router_kernel,
        grid=(M // tm,),
        in_specs=[pl.BlockSpec((tm, D_MODEL), lambda i: (i, 0)),
                  pl.BlockSpec((D_MODEL, LANES), lambda i: (0, 0))],
        out_specs=[pl.BlockSpec((tm, LANES), lambda i: (i, 0))] * 2,
        out_shape=[jax.ShapeDtypeStruct((M, LANES), F32), jax.ShapeDtypeStruct((M, LANES), jnp.int32)],
        compiler_params=_cparams(1),
        name="router",
    )(x, router_pad)


def _rw_prep(pm, prm):
    (w0, a0, w2p, a2p, g2p, k_k, k_a, _, _, _, g64, _) = prm
    r = pm[:, 0:512]
    k = pm[:, 512:1024]
    v = pm[:, 1024:1536]
    sm = pm[:, 1536:1792]
    logw = -_softplus(-(w0 + _nn(_bf(jnp.tanh(sm)), w2p))) - 0.5
    lw = -jnp.exp(logw)
    a = _sigmoid(a0 + _nn(_bf(sm), a2p))
    g = _nn(_bf(_sigmoid(sm)), g2p)
    kk = k * k_k
    ss = jnp.concatenate([_dot3((kk * kk)[:, LANES * p:LANES * (p + 1)], g64) for p in range(4)], axis=1)
    kk = kk / jnp.maximum(jnp.sqrt(ss), 1e-12)
    kmod = k * (1.0 + (a - 1.0) * k_a)
    return r, lw, kmod, kk, kk * a, v, g


def _rw_post(y, r, kmod, v, g, prm):
    (_, _, _, _, _, _, _, r_k, ln_w, ln_b, g64, g64m) = prm

    def per_head(x, m):
        return jnp.concatenate([_dot3(x[:, LANES * p:LANES * (p + 1)], m) for p in range(4)], axis=1)

    yc = y - per_head(y, g64m)
    var = per_head(yc * yc, g64m)
    yn = yc * lax.rsqrt(var + RW_GN_EPS) * ln_w + ln_b
    bonus = per_head(r * kmod * r_k, g64) * v
    return (yn + bonus) * g


def _rw_chunks(L, seqs, scr, s_ref):
    q_s, rr_s, kt_s, bt_s, kp_s, bp_s, v_s, wl_s, y_s = scr
    units = [(si, r0, h) for (si, r0) in seqs for h in range(RW_H)]

    def fetch(ref, r0, h):
        return ref[pl.ds(r0, L), RW_HD * h:RW_HD * (h + 1)]

    row = _iota((2 * L, 2 * L), 0)
    col = _iota((2 * L, 2 * L), 1)
    colm = jnp.where(col >= L, col - L, col)
    mask4 = colm <= jnp.where(row < L, row - 1, row - L)
    qf = [fetch(q_s, r0, h) for (_, r0, h) in units]
    rf = [fetch(rr_s, r0, h) for (_, r0, h) in units]
    vb = [_bf(fetch(v_s, r0, h)) for (_, r0, h) in units]
    g4 = [jnp.where(mask4, _nt(_bf(jnp.concatenate([qf[i], rf[i]], axis=0)),
                               _bf(jnp.concatenate([fetch(kt_s, r0, h), fetch(bt_s, r0, h)], axis=0))), 0.0)
          for i, (_, r0, h) in enumerate(units)]
    av = [_nn(_bf(g4[i][:, :L]), vb[i]) for i in range(len(units))]
    x = [jnp.concatenate([qf[i], av[i][:L]], axis=1) for i in range(len(units))]
    pw = [g4[i][:L, L:] for i in range(len(units))]
    x = [x[i] - _nn(_bf(pw[i]), _bf(x[i])) for i in range(len(units))]
    for _ in range(int(math.log2(L)) - 1):
        pw = [_nn(_bf(m), _bf(m)) for m in pw]
        x = [x[i] + _nn(_bf(pw[i]), _bf(x[i])) for i in range(len(units))]
    ry = [jnp.concatenate([rf[i], av[i][L:]], axis=1) - _nn(_bf(g4[i][L:, L:]), _bf(x[i])) for i in range(len(units))]
    gh = [_tn(_bf(x[i]), _bf(fetch(bp_s, r0, h))) for i, (_, r0, h) in enumerate(units)]
    hk = [_tn(vb[i], _bf(fetch(kp_s, r0, h))) for i, (_, r0, h) in enumerate(units)]
    ys = []
    for i, (si, r0, h) in enumerate(units):
        S = s_ref[si, h]
        Sb = _bf(S)
        ys.append(_nt(_bf(ry[i][:, :RW_HD]), Sb) + ry[i][:, RW_HD:])
        w_last = fetch(wl_s, r0, h)[0:1, :]
        s_ref[si, h] = S * w_last - _nn(Sb, _bf(gh[i][:RW_HD])) + hk[i] - gh[i][RW_HD:]
    for i in range(0, len(units), 2):
        _, r0, h = units[i]
        y_s[pl.ds(r0, L), RW_HD * h:RW_HD * (h + 2)] = jnp.concatenate([ys[i], ys[i + 1]], axis=1)


N_RW_PRM = 13


def _rwkv_kernel(is_prompt, nsq, L, grp, n_alias, *refs):
    n_p = nsq if is_prompt else 1
    p_refs = refs[:n_p]
    sh0_ref, s0_ref = refs[n_p:n_p + 2]
    pos = n_p + 2
    prm_refs = refs[pos:pos + N_RW_PRM]
    pos += N_RW_PRM
    tri_ref, ones_ref = refs[pos:pos + 2]
    pos += 2 + n_alias
    o_ref, sh_out_ref, s_out_ref = refs[pos:pos + 3]
    scr = refs[pos + 3:]
    g_s, r_s, k_s = scr[9:12]
    mu = prm_refs[0][...]
    prm = tuple(r[...] for r in prm_refs[1:])
    if is_prompt:
        prev_scr = scr[12]

        @pl.when(pl.program_id(0) == 0)
        def _():
            s_out_ref[...] = s0_ref[...]
            prev_scr[...] = sh0_ref[...]

        ps, pps = [], []
        for s in range(nsq):
            ps.append(p_refs[s][...])
            pps.append(_shift_rows(ps[s], prev_scr[s], 1))
            prev_scr[s] = ps[s][L - SUBLANES:L]
        p = jnp.concatenate(ps, axis=0)
        pp = jnp.concatenate(pps, axis=0)
        sh_out_ref[...] = prev_scr[...]
    else:
        p = p_refs[0][...]
        p3 = p.reshape(nsq, L, p.shape[1])
        pp = jnp.where(_iota(p3.shape, 1) == 0, sh0_ref[...], pltpu.roll(p3, 1, 1)).reshape(p.shape)
        sh_out_ref[...] = p3[:, L - 1:L, :]
        s_out_ref[...] = s0_ref[...]

    r, lw, kmod, kk, bb, v, g = _rw_prep(p + (pp - p) * mu, prm)
    cum = _dot3r(tri_ref[...], lw)
    tot = _dot3r(ones_ref[...], lw)
    e_neg = jnp.exp(-cum)
    e_rem = jnp.exp(tot - cum)
    q_s, rr_s, kt_s, bt_s, kp_s, bp_s, v_s, wl_s, y_s = scr[:9]
    q_s[...] = kk * jnp.exp(cum - lw)
    rr_s[...] = r * jnp.exp(cum)
    kt_s[...] = kmod * e_neg
    bt_s[...] = bb * e_neg
    kp_s[...] = kmod * e_rem
    bp_s[...] = bb * e_rem
    v_s[...] = v
    wl_s[...] = jnp.exp(tot)
    g_s[...], r_s[...], k_s[...] = g, r, kmod

    if is_prompt:
        _rw_chunks(L, [(s, s * L) for s in range(nsq)], scr[:9], s_out_ref)
    else:
        def body(i, c):
            _rw_chunks(L, [(i * grp + j, pl.multiple_of((i * grp + j) * L, SUBLANES)) for j in range(grp)],
                       scr[:9], s_out_ref)
            return c
        lax.fori_loop(0, nsq // grp, body, 0)

    out = _rw_post(y_s[...], r_s[...], k_s[...], v_s[...], g_s[...], prm).astype(BF16)
    if is_prompt:
        for s in range(nsq):
            o_ref[s] = out[s * L:(s + 1) * L]
    else:
        o_ref[...] = out


def _rwkv(P, row0, B, T, sh0, s0, prm_list, s0_layer=None, stacked=None):
    is_prompt = T % CHUNK == 0
    wide = 4 * GROUP_W
    s_shape = (B, RW_H, RW_HD, RW_HD)
    if is_prompt:
        L, nsq, grid = CHUNK, B, (T // CHUNK,)
        rows = nsq * L
        in_specs = [pl.BlockSpec((L, wide), functools.partial(lambda s, c: ((row0 + s * T) // L + c, 0), s))
                    for s in range(nsq)]
        in_specs += [pl.BlockSpec(sh0.shape, lambda c: (0, 0, 0)), pl.BlockSpec(s0.shape, lambda c: (0, 0, 0, 0))]
        out_specs = [pl.BlockSpec((nsq, L, GROUP_W), lambda c: (0, c, 0)),
                     pl.BlockSpec(sh0.shape, lambda c: (0, 0, 0)), pl.BlockSpec(s0.shape, lambda c: (0, 0, 0, 0))]
        out_shape = [jax.ShapeDtypeStruct((nsq, T, GROUP_W), BF16)]
        args = [P] * nsq
    else:
        L, nsq, grid = T, 16, (B // 16,)
        rows = nsq * L
        s_spec = (pl.BlockSpec((nsq, RW_H, RW_HD, RW_HD), lambda i: (i, 0, 0, 0)) if s0_layer is None else
                  pl.BlockSpec((None, nsq, RW_H, RW_HD, RW_HD), lambda i: (s0_layer, i, 0, 0, 0)))
        in_specs = [pl.BlockSpec((rows, wide), lambda i: (row0 // rows + i, 0)),
                    pl.BlockSpec((nsq, 1, wide), lambda i: (i, 0, 0)), s_spec]
        out_specs = [pl.BlockSpec((rows, GROUP_W), lambda i: (i, 0)),
                     pl.BlockSpec((nsq, 1, wide), lambda i: (i, 0, 0)),
                     pl.BlockSpec((nsq, RW_H, RW_HD, RW_HD), lambda i: (i, 0, 0, 0))]
        out_shape = [jax.ShapeDtypeStruct((B * T, GROUP_W), BF16)]
        args = [P]
    seq_of_row = jnp.arange(rows) // L
    same = seq_of_row[:, None] == seq_of_row[None, :]
    tri = (same & (jnp.arange(rows)[:, None] >= jnp.arange(rows)[None, :])).astype(BF16)
    consts = list(prm_list) + [tri, same.astype(BF16)]
    in_specs += [pl.BlockSpec(x.shape, lambda *a: (0, 0)) for x in consts]
    out_shape += [jax.ShapeDtypeStruct(sh0.shape, F32), jax.ShapeDtypeStruct(s_shape, F32)]
    args += [sh0, s0] + consts
    aliases = {}
    if stacked is not None:
        depth, layer, prev = stacked
        blk = out_specs[2].block_shape
        out_specs[2] = pl.BlockSpec((None,) + tuple(blk), (lambda c: (layer, 0, 0, 0, 0)) if is_prompt
                                    else (lambda i: (layer, i, 0, 0, 0)))
        out_shape[2] = jax.ShapeDtypeStruct((depth,) + s_shape, F32)
        if prev is not None:
            aliases[len(args)] = 2
            in_specs.append(pl.BlockSpec(memory_space=pl.ANY))
            args.append(prev)
    scratch = [pltpu.VMEM((rows, GROUP_W), F32)] * 12
    if is_prompt:
        scratch.append(pltpu.VMEM(sh0.shape, F32))
    outs = pl.pallas_call(
        functools.partial(_rwkv_kernel, is_prompt, nsq, L, 2, len(aliases)),
        grid=grid,
        in_specs=in_specs,
        out_specs=out_specs,
        out_shape=out_shape,
        input_output_aliases=aliases,
        scratch_shapes=scratch,
        compiler_params=_cparams(1),
        name="rwkv_prompt" if is_prompt else "rwkv_sample",
    )(*args)
    return outs[0].reshape(B * T, GROUP_W), outs[1], outs[2]


SEQ_GROUP = 4
GL_SAFE_SPREAD = 40.0


def _causal(L):
    return _iota((L, L), 0) >= _iota((L, L), 1)


def _heads(units):
    return [(u, h) for u in range(len(units)) for h in range(HEADS)]


def _hs(h):
    return slice(h * HEAD_D, (h + 1) * HEAD_D)


def _ret_units(L, units, S_ref, consts):
    cosv, sinv, dm_ref, qd_ref, kd_ref, gn, c_dec = consts
    uh = _heads(units)

    def rot(x):
        return x * cosv + pltpu.roll(x, HEAD_D // 2, 1) * sinv

    qr = [rot(units[u][0](h * HEAD_D, HEAD_D)) for u, h in uh]
    kr = [rot(units[u][0](GROUP_W + h * HEAD_D, HEAD_D)) * (HEAD_D ** -0.5) for u, h in uh]
    vb = [_bf(units[u][0](2 * GROUP_W + h * HEAD_D, HEAD_D)) for u, h in uh]
    S = [S_ref[units[u][3], h] for u, h in uh]
    A = [_nt(_bf(qr[i]), _bf(kr[i])) * dm_ref[h] for i, (u, h) in enumerate(uh)]
    o = [_nn(_bf(A[i]), vb[i]) + _nn(_bf(qr[i] * qd_ref[h]), _bf(S[i])) for i, (u, h) in enumerate(uh)]
    for i, (u, h) in enumerate(uh):
        S_ref[units[u][3], h] = c_dec[h] * S[i] + _tn(_bf(kr[i] * kd_ref[h]), vb[i])
    for i, (u, h) in enumerate(uh):
        oc = o[i] - jnp.mean(o[i], axis=1, keepdims=True)
        on = oc * lax.rsqrt(jnp.mean(oc * oc, axis=1, keepdims=True) + GN_EPS) * gn[:, _hs(h)]
        g = units[u][0](3 * GROUP_W + h * HEAD_D, HEAD_D)
        units[u][2](h * HEAD_D, _bf(_silu(g) * on))


def _ml_units(L, units, st_refs, consts):
    C_ref, n_ref, m_ref, cv_ref = st_refs
    ltri, ones_l, conv_w, conv_b, bias_sm, gn = consts
    uh = _heads(units)
    conv, xs, bc_all = [], [], []
    for get, get_sm, _, sidx in units:
        x = get(0, 2 * GROUP_W)
        prev8 = cv_ref[sidx]
        c = (conv_b + _shift_rows(x, prev8, 3) * conv_w[0:1] + _shift_rows(x, prev8, 2) * conv_w[1:2]
             + _shift_rows(x, prev8, 1) * conv_w[2:3] + x * conv_w[3:4])
        cv_ref[sidx] = x[L - SUBLANES:L]
        conv.append(_silu(c))
        xs.append(get_sm(LANES, LANES) + bias_sm)
        bc_all.append(_dot3r(ltri, _log_sigmoid(xs[-1])))
    causal = _causal(L)
    lane0 = _iota((L, LANES), 1) == 0
    q = [conv[u][:, _hs(h)] for u, h in uh]
    k = [conv[u][:, GROUP_W + h * HEAD_D:GROUP_W + (h + 1) * HEAD_D] * (HEAD_D ** -0.5) for u, h in uh]
    vb = [_bf(units[u][0](2 * GROUP_W + h * HEAD_D, HEAD_D)) for u, h in uh]
    ig = [xs[u][:, SM_MLI - LANES + h:SM_MLI - LANES + h + 1] for u, h in uh]
    bc = [bc_all[u][:, SM_MLF - LANES + h:SM_MLF - LANES + h + 1] for u, h in uh]
    m_prev = [m_ref[units[u][3], h][:, 0:1] for u, h in uh]
    row = [_dot3r(ones_l, jnp.where(lane0, ig[i] - bc[i], 0.0), "nt") for i in range(len(uh))]
    dlog = [jnp.where(causal, bc[i] + row[i], -jnp.inf) for i in range(len(uh))]
    inter = [bc[i] + m_prev[i] for i in range(len(uh))]
    m_t = [jnp.maximum(inter[i], jnp.max(dlog[i], axis=1, keepdims=True)) for i in range(len(uh))]
    dmat = [jnp.exp(dlog[i] - m_t[i]) for i in range(len(uh))]
    s_in = [jnp.exp(inter[i] - m_t[i]) for i in range(len(uh))]
    C = [C_ref[units[u][3], h] for u, h in uh]
    n = [n_ref[units[u][3], h] for u, h in uh]
    qb = [_bf(x) for x in q]
    A = [_nt(qb[i], _bf(k[i])) * dmat[i] for i in range(len(uh))]
    num = [_nn(_bf(A[i]), vb[i]) + s_in[i] * _nt(qb[i], _bf(C[i])) for i in range(len(uh))]
    den = [jnp.sum(A[i], axis=1, keepdims=True) + s_in[i] * jnp.sum(q[i] * n[i], axis=1, keepdims=True)
           for i in range(len(uh))]
    hh = [num[i] / jnp.maximum(jnp.abs(den[i]), jnp.exp(-m_t[i])) for i in range(len(uh))]
    for i, (u, h) in enumerate(uh):
        sidx = units[u][3]
        m_new = m_t[i][L - 1:L, :]
        b_last = bc[i][L - 1:L, :]
        carry = jnp.exp(b_last + m_prev[i] - m_new)
        wk = jnp.exp(b_last - bc[i] + ig[i] - m_new) * k[i]
        C_ref[sidx, h] = carry * C[i] + _tn(vb[i], _bf(wk))
        n_ref[sidx, h] = carry * n[i] + jnp.sum(wk, axis=0, keepdims=True)
        m_ref[sidx, h] = jnp.broadcast_to(m_new, (1, LANES))
    for i, (u, h) in enumerate(uh):
        hc = hh[i] - jnp.mean(hh[i], axis=1, keepdims=True)
        hn = hc * lax.rsqrt(jnp.mean(hc * hc, axis=1, keepdims=True) + GN_EPS) * gn[:, _hs(h)]
        o_pre = units[u][0](3 * GROUP_W + h * HEAD_D, HEAD_D)
        units[u][2](h * HEAD_D, _bf(_sigmoid(o_pre) * hn))


def _gl_units(L, units, S_ref, consts):
    ltri, a2p, ab, gn = consts
    uh = _heads(units)
    mid = L // 2 - 1
    qs, kall, bcs, q_st, k_st, e_last = [], [], [], [], [], []
    spread = jnp.float32(0.0)
    for get, get_sm, _, _ in units:
        la = _log_sigmoid(_nn(_bf(get_sm(LANES, LANES)), a2p) + ab) / GL_TAU
        bc = _dot3r(ltri, la)
        b_last = bc[L - 1:L, :]
        qs.append(get(0, GL_DK * HEADS) * (GL_DK ** -0.5))
        kall.append(get(GL_DK * HEADS, GL_DK * HEADS))
        bcs.append(bc)
        spread = jnp.maximum(spread, jnp.max(jnp.abs(bc - bc[mid:mid + 1, :])))
        q_st.append(_bf(qs[-1] * jnp.exp(bc)))
        k_st.append(_bf(kall[-1] * jnp.exp(b_last - bc)))
        e_last.append(jnp.exp(b_last))
    causal = _causal(L)
    eye = _iota((GL_DK, GL_DK), 0) == _iota((GL_DK, GL_DK), 1)

    def ks(h):
        return slice(h * GL_DK, (h + 1) * GL_DK)

    def scores_factored():
        q_in = [_bf(qs[u] * jnp.exp(bcs[u] - bcs[u][mid:mid + 1, :])) for u in range(len(units))]
        k_in = [_bf(kall[u] * jnp.exp(bcs[u][mid:mid + 1, :] - bcs[u])) for u in range(len(units))]
        return [jnp.where(causal, _nt(q_in[u][:, ks(h)], k_in[u][:, ks(h)]), 0.0) for u, h in uh]

    def scores_direct():
        rows = _iota((L, GL_DK * HEADS), 0)
        cols = _iota((L, L), 1)

        def body(s, acc):
            out = []
            for u in range(len(units)):
                pick = rows == s
                k_s = jnp.sum(jnp.where(pick, kall[u], 0.0), axis=0, keepdims=True)
                b_s = jnp.sum(jnp.where(pick, bcs[u], 0.0), axis=0, keepdims=True)
                t = qs[u] * k_s * jnp.exp(jnp.where(rows >= s, bcs[u] - b_s, -jnp.inf))
                for h in range(HEADS):
                    col = jnp.sum(t[:, ks(h)], axis=1, keepdims=True)
                    out.append(jnp.where(cols == s, col, acc[u * HEADS + h]))
            return out

        return lax.fori_loop(0, L, body, [jnp.zeros((L, L), F32) for _ in uh])

    A = lax.cond(spread < GL_SAFE_SPREAD, scores_factored, scores_direct)
    vb = [_bf(units[u][0](GROUP_W + h * HEAD_D, HEAD_D)) for u, h in uh]
    S = [S_ref[units[u][3], h] for u, h in uh]
    o = [_nn(_bf(A[i]), vb[i]) + _nn(q_st[u][:, ks(h)], _bf(S[i])) for i, (u, h) in enumerate(uh)]
    for i, (u, h) in enumerate(uh):
        e_col = jnp.sum(jnp.where(eye, e_last[u][:, ks(h)], 0.0), axis=1, keepdims=True)
        S_ref[units[u][3], h] = e_col * S[i] + _tn(k_st[u][:, ks(h)], vb[i])
    for i, (u, h) in enumerate(uh):
        on = o[i] * lax.rsqrt(jnp.mean(o[i] * o[i], axis=1, keepdims=True) + GN_EPS) * gn[:, _hs(h)]
        g = units[u][0](2 * GROUP_W + h * HEAD_D, HEAD_D)
        units[u][2](h * HEAD_D, _bf(_silu(g) * on))


def _chunk_kernel(kind, nslab, nunit, L, n_state, n_const, n_alias, has_sm, *refs):
    pos = 0
    p_refs = refs[pos:pos + nslab]; pos += nslab
    sm_refs = refs[pos:pos + (nslab if has_sm else 0)]; pos += (nslab if has_sm else 0)
    st_in = refs[pos:pos + n_state]; pos += n_state
    c_refs = refs[pos:pos + n_const]; pos += n_const + n_alias
    o_ref = refs[pos]; pos += 1
    st_out = refs[pos:pos + n_state]; pos += n_state
    c = pl.program_id(1)

    @pl.when(c == 0)
    def _():
        for a, b in zip(st_in, st_out):
            b[...] = a[...]

    def unit(slab, seq, sidx):
        r0 = seq * L if isinstance(seq, int) else pl.multiple_of(seq * L, SUBLANES)

        def get(col, width):
            return p_refs[slab][pl.ds(r0, L), col:col + width]

        def get_sm(col, width):
            return sm_refs[slab][pl.ds(r0, L), col:col + width]

        def put(col, val):
            o_ref[slab, pl.ds(r0, L), col:col + val.shape[1]] = val

        return (get, get_sm, put, sidx)

    def run(units):
        if kind == "ret":
            cosv, sinv = c_refs[0][...], c_refs[1][...]
            consts = (cosv, sinv, c_refs[2], c_refs[3], c_refs[4], c_refs[5][...], _RT_CDEC[L])
            _ret_units(L, units, st_out[0], consts)
        elif kind == "ml":
            _ml_units(L, units, st_out, tuple(r[...] for r in c_refs))
        else:
            _gl_units(L, units, st_out[0], tuple(r[...] for r in c_refs))

    if nunit == 1:
        run([unit(s, 0, s) for s in range(nslab)])
    else:
        def body(i, carry):
            run([unit(0, i * SEQ_GROUP + j, i * SEQ_GROUP + j) for j in range(SEQ_GROUP)])
            return carry
        lax.fori_loop(0, nunit // SEQ_GROUP, body, 0)


def _rt_log_gamma():
    return [math.log1p(-(2.0 ** (-5.0 - h))) for h in range(HEADS)]


_RT_CDEC = {L: [math.exp(L * lg) for lg in _rt_log_gamma()] for L in (8, CHUNK)}


def _chunk_call(kind, P, col_blk, col_w, sm, row0, nseq, T, states, consts, const_chunked, name, stacked=None):
    has_sm = sm
    if T % CHUNK == 0:
        L, nslab, nunit = CHUNK, nseq, 1
        grid = (1, T // L)
        rows = L
        def pmap(s, i, c):
            return ((row0 + s * T) // L + c, col_blk)
        def smap(s, i, c):
            return ((row0 + s * T) // L + c, 3)
        def omap(i, c):
            return (0, c, 0)
        sblk = nseq
    else:
        L, nslab, nunit = T, 1, 16
        grid = (nseq // nunit, 1)
        rows = L * nunit
        def pmap(s, i, c):
            return (row0 // rows + i, col_blk)
        def smap(s, i, c):
            return (row0 // rows + i, 3)
        def omap(i, c):
            return (0, i, 0)
        sblk = nunit
    in_specs = [pl.BlockSpec((rows, col_w), functools.partial(pmap, s)) for s in range(nslab)]
    args = [P] * nslab
    if has_sm:
        in_specs += [pl.BlockSpec((rows, GROUP_W), functools.partial(smap, s)) for s in range(nslab)]
        args += [P] * nslab
    st_specs, st_shapes = [], []
    for st in states:
        arr, layer = st if isinstance(st, tuple) else (st, None)
        tail = arr.shape[1:] if layer is None else arr.shape[2:]
        nd = len(tail)
        st_specs.append(pl.BlockSpec((sblk,) + tail, lambda i, c, nd=nd: (i,) + (0,) * nd))
        st_shapes.append(jax.ShapeDtypeStruct((nseq,) + tail, F32))
        if layer is None:
            in_specs.append(st_specs[-1])
        else:
            in_specs.append(pl.BlockSpec((None, sblk) + tail, lambda i, c, nd=nd, layer=layer: (layer, i) + (0,) * nd))
        args.append(arr)
    for cst, chunked in zip(consts, const_chunked):
        if chunked:
            in_specs.append(pl.BlockSpec((None,) + cst.shape[1:], lambda i, c: (c, 0, 0)))
        else:
            in_specs.append(pl.BlockSpec(cst.shape, lambda i, c, nd=cst.ndim: (0,) * nd))
        args.append(cst)
    aliases = {}
    if stacked is not None:
        depth, layer, prev = stacked
        tail = st_shapes[0].shape[1:]
        st_specs[0] = pl.BlockSpec((None, sblk) + tail, lambda i, c, nd=len(tail): (layer, i) + (0,) * nd)
        st_shapes[0] = jax.ShapeDtypeStruct((depth, nseq) + tail, F32)
        if prev is not None:
            aliases[len(args)] = 1
            in_specs.append(pl.BlockSpec(memory_space=pl.ANY))
            args.append(prev)
    out_specs = [pl.BlockSpec((nslab, rows, GROUP_W), omap)] + st_specs
    out_shape = [jax.ShapeDtypeStruct((nslab, nseq * T // nslab, GROUP_W), BF16)] + st_shapes
    outs = pl.pallas_call(
        functools.partial(_chunk_kernel, kind, nslab, nunit, L, len(states), len(consts), len(aliases), has_sm),
        grid=grid,
        in_specs=in_specs,
        out_specs=out_specs,
        out_shape=out_shape,
        input_output_aliases=aliases,
        compiler_params=_cparams(2),
        name=name,
    )(*args)
    return outs[0].reshape(nseq * T, GROUP_W), outs[1:]


def _pack_w_in(w_in):
    z = jnp.zeros(w_in.shape[:2] + (GROUP_W - 184,), w_in.dtype)
    parts = [w_in[..., 0:1696], w_in[..., 3744:3752], w_in[..., 6824:6840], z,
             w_in[..., 1696:3744], w_in[..., 3752:5800], w_in[..., 5800:6824], w_in[..., 6840:7352]]
    return jnp.concatenate(parts, axis=-1).astype(BF16)


def _row(v):
    return v.reshape(1, -1).astype(F32)


def _rows_at(mat, r0, rows_total):
    z = jnp.zeros((rows_total, mat.shape[1]), F32).at[r0:r0 + mat.shape[0]].set(mat)
    return z.astype(BF16)


def _ret_consts(pos0, T, L, gn):
    pos = pos0 + jnp.arange(T, dtype=F32)
    inv = 1.0 / (10000.0 ** jnp.linspace(0.0, 1.0, HEAD_D // 2, dtype=F32))
    ang = pos[:, None] * inv[None, :]
    cos, sin = jnp.cos(ang), jnp.sin(ang)
    cos2 = jnp.concatenate([cos, cos], axis=1).reshape(T // L, L, HEAD_D)
    sin2 = jnp.concatenate([-sin, sin], axis=1).reshape(T // L, L, HEAD_D)
    lg = jnp.log1p(-jnp.exp2(-5.0 - jnp.arange(HEADS, dtype=F32)))
    idx = jnp.arange(L, dtype=F32)
    rel = idx[:, None] - idx[None, :]
    dmat = jnp.where(rel >= 0, jnp.exp(jnp.maximum(rel, 0.0) * lg[:, None, None]), 0.0)
    q_dec = jnp.broadcast_to(jnp.exp((idx + 1.0) * lg[:, None])[..., None], (HEADS, L, HEAD_D))
    k_dec = jnp.broadcast_to(jnp.exp((L - 1.0 - idx) * lg[:, None])[..., None], (HEADS, L, HEAD_D))
    return [cos2, sin2, dmat, q_dec, k_dec, _row(gn)], [True, True, False, False, False, False]


def _tri(L):
    return jnp.tril(jnp.ones((L, L), F32)).astype(BF16)


def _moe_plan(ids, tm, n_tiles):
    e = ids.reshape(-1)
    n = e.shape[0]
    onehot = (e[:, None] == jnp.arange(N_EXPERTS, dtype=jnp.int32)[None, :]).astype(jnp.int32)
    rank = jnp.sum((jnp.cumsum(onehot, axis=0) - 1) * onehot, axis=1)
    counts = jnp.sum(onehot, axis=0)
    padded = ((counts + tm - 1) // tm) * tm
    ends = jnp.cumsum(padded)
    starts = ends - padded
    dest = jnp.sum(starts[None, :] * onehot, axis=1) + rank
    row_src = jnp.zeros((n_tiles * tm,), jnp.int32).at[dest].set(jnp.arange(n, dtype=jnp.int32) // 2,
                                                                 unique_indices=True)
    tile_start = jnp.arange(n_tiles, dtype=jnp.int32) * tm
    tile_expert = jnp.minimum(jnp.sum((tile_start[:, None] >= ends[None, :]).astype(jnp.int32), axis=1),
                              N_EXPERTS - 1)
    return row_src, dest.reshape(-1, 2), tile_expert, (ends[-1:] // tm).astype(jnp.int32)


def kernel(x_prompt, x_sample, state_rw_shift, state_rw_wkv, state_ml_conv, state_ml_C, state_ml_n, state_ml_m, state_rt_S, state_gl_S, w_in, rw_mu, rw_w0, rw_w2, rw_a0, rw_a2, rw_g2, rw_kk, rw_ka, rw_rk, rw_ln_w, rw_ln_b, ml_conv_w, ml_conv_b, ml_bi, ml_bf, ml_gn_w, rt_gn_w, gl_a2, gl_ab, gl_gn_w, w_out, ln1_w, ln1_b, ln2_w, ln2_b, ffn_w1, ffn_w3, ffn_w2, moe_router, moe_w1, moe_w3, moe_w2):
    Bp, Tp = x_prompt.shape[:2]
    Bs, Ts = x_sample.shape[:2]
    Mp, Ms = Bp * Tp, Bs * Ts
    M = Mp + Ms
    depth = w_in.shape[0]
    x = [x_prompt.reshape(Mp, D_MODEL), x_sample.reshape(Ms, D_MODEL)]
    xb = [v.astype(BF16) for v in x]
    w_in_p = _pack_w_in(w_in)
    w_out_b = w_out.astype(BF16)
    blockdiag = (jnp.arange(LANES)[:, None] // RW_HD) == (jnp.arange(LANES)[None, :] // RW_HD)
    g64 = blockdiag.astype(BF16)
    g64m = (blockdiag.astype(F32) / RW_HD).astype(BF16)
    pad_sh = 4 * GROUP_W - RW_COLS
    outs = {k: [[], []] for k in ("sh", "conv", "n", "m")}
    big = {k: [None, None] for k in ("wkv", "C", "rt", "gl")}

    for l in range(depth):
        P = [_mm([v], w_in_p, l, GROUP_W, "mm_in") for v in xb]

        prm = [_row(jnp.pad(rw_mu[l], (0, pad_sh))), _row(rw_w0[l]), _row(rw_a0[l]),
               _rows_at(rw_w2[l], 0, 2 * LANES), _rows_at(rw_a2[l], 32, 2 * LANES), _rows_at(rw_g2[l], 64, 2 * LANES),
               _row(rw_kk[l]), _row(rw_ka[l]), _row(rw_rk[l]), _row(rw_ln_w[l]), _row(rw_ln_b[l]), g64, g64m]
        o_rw_p, sh_p, big["wkv"][0] = _rwkv(P[0], 0, Bp, Tp, jnp.zeros((Bp, SUBLANES, 4 * GROUP_W), F32),
                                             jnp.zeros((Bp, RW_H, RW_HD, RW_HD), F32), prm,
                                             stacked=(depth, l, big["wkv"][0]))
        sh_in = jnp.pad(state_rw_shift[l], ((0, 0), (0, pad_sh)))[:, None, :]
        o_rw_s, sh_s, big["wkv"][1] = _rwkv(P[1], 0, Bs, Ts, sh_in, state_rw_wkv, prm, s0_layer=l,
                                             stacked=(depth, l, big["wkv"][1]))
        outs["sh"][0].append(sh_p[:, SUBLANES - 1, :RW_COLS])
        outs["sh"][1].append(sh_s[:, 0, :RW_COLS])

        bias_sm = jnp.zeros((LANES,), F32).at[SM_MLI - LANES:SM_MLI - LANES + HEADS].set(ml_bi[l])
        bias_sm = bias_sm.at[SM_MLF - LANES:SM_MLF - LANES + HEADS].set(ml_bf[l])

        def ml_consts(L):
            return [_tri(L), jnp.ones((L, LANES), BF16), ml_conv_w[l], _row(ml_conv_b[l]), _row(bias_sm), _row(ml_gn_w[l])]

        def ml_states(C, n, m, conv):
            b = n.shape[0]
            return [C, n[:, :, None, :], jnp.broadcast_to(m[:, :, None, None], (b, HEADS, 1, LANES)),
                    jnp.pad(conv, ((0, 0), (SUBLANES - conv.shape[1], 0), (0, 0)))]

        zp = lambda *s: jnp.zeros((Bp,) + s, F32)
        o_ml_p, st_p = _chunk_call("ml", P[0], 1, 4 * GROUP_W, True, 0, Bp, Tp,
                                   ml_states(zp(HEADS, HEAD_D, HEAD_D), zp(HEADS, HEAD_D), zp(HEADS), zp(3, 2 * GROUP_W)),
                                   ml_consts(CHUNK), [False] * 6, "mlstm_prompt", stacked=(depth, l, big["C"][0]))
        o_ml_s, st_s = _chunk_call("ml", P[1], 1, 4 * GROUP_W, True, 0, Bs, Ts,
                                   ml_states((state_ml_C, l), state_ml_n[l], state_ml_m[l], state_ml_conv[l]),
                                   ml_consts(Ts), [False] * 6, "mlstm_sample", stacked=(depth, l, big["C"][1]))
        for g, st in enumerate((st_p, st_s)):
            big["C"][g] = st[0]
            outs["n"][g].append(st[1][:, :, 0, :])
            outs["m"][g].append(st[2][:, :, 0, 0])
            outs["conv"][g].append(st[3][:, SUBLANES - 3:, :])

        c_p, ch_p = _ret_consts(0.0, Tp, CHUNK, rt_gn_w[l])
        c_s, ch_s = _ret_consts(float(PAST_LEN), Ts, Ts, rt_gn_w[l])
        o_rt_p, (big["rt"][0],) = _chunk_call("ret", P[0], 2, 4 * GROUP_W, False, 0, Bp, Tp, [zp(HEADS, HEAD_D, HEAD_D)],
                                              c_p, ch_p, "ret_prompt", stacked=(depth, l, big["rt"][0]))
        o_rt_s, (big["rt"][1],) = _chunk_call("ret", P[1], 2, 4 * GROUP_W, False, 0, Bs, Ts, [(state_rt_S, l)],
                                              c_s, ch_s, "ret_sample", stacked=(depth, l, big["rt"][1]))

        def gl_consts(L):
            return [_tri(L), _rows_at(gl_a2[l], SM_GLA - LANES, LANES), _row(gl_ab[l]), _row(gl_gn_w[l])]

        o_gl_p, (big["gl"][0],) = _chunk_call("gl", P[0], 4, 3 * GROUP_W, True, 0, Bp, Tp, [zp(HEADS, GL_DK, HEAD_D)],
                                              gl_consts(CHUNK), [False] * 4, "gla_prompt",
                                              stacked=(depth, l, big["gl"][0]))
        o_gl_s, (big["gl"][1],) = _chunk_call("gl", P[1], 4, 3 * GROUP_W, True, 0, Bs, Ts, [(state_gl_S, l)],
                                              gl_consts(Ts), [False] * 4, "gla_sample",
                                              stacked=(depth, l, big["gl"][1]))

        mixed = ((o_rw_p, o_ml_p, o_rt_p, o_gl_p), (o_rw_s, o_ml_s, o_rt_s, o_gl_s))
        for g in range(2):
            mix = _mm(list(mixed[g]), w_out_b, l, GROUP_W, "mm_out")
            x[g], xb[g] = _add_ln(x[g], [mix], None, ln1_w[l], ln1_b[l])

        if l % 2 == 0:
            w1, w3, w2 = (w[l // 2][None].astype(BF16) for w in (ffn_w1, ffn_w3, ffn_w2))
            for g in range(2):
                rows = x[g].shape[0]
                tm = _pick(rows, (512, 256, 128))
                y = _ffn(xb[g], jnp.zeros((rows // tm,), jnp.int32), jnp.full((1,), rows // tm, jnp.int32), w1, w3, w2, tm)
                x[g], xb[g] = _add_ln(x[g], [y], None, ln2_w[l], ln2_b[l])
        else:
            tm = _pick(M, (512, 256, 128))
            n_tiles = 2 * M // tm + N_EXPERTS
            router_pad = jnp.pad(moe_router[l // 2], ((0, 0), (0, LANES - N_EXPERTS)))
            routed = [_router(v, router_pad) for v in x]
            gates = [r[0] for r in routed]
            ids = jnp.concatenate([r[1][:, :2] for r in routed], axis=0)
            row_src, dest, tile_expert, n_used = _moe_plan(ids, tm, n_tiles)
            xs = jnp.take(jnp.concatenate(x, axis=0), row_src, axis=0, mode="clip")
            ysorted = _ffn(xs, tile_expert, n_used, moe_w1[l // 2].astype(BF16), moe_w3[l // 2].astype(BF16),
                           moe_w2[l // 2].astype(BF16), tm)
            for g, (r0, r1) in enumerate(((0, Mp), (Mp, M))):
                ys = [jnp.take(ysorted, dest[r0:r1, k], axis=0, mode="clip") for k in range(2)]
                x[g], xb[g] = _add_ln(x[g], ys, [gates[g][:, 0:1], gates[g][:, 1:2]], ln2_w[l], ln2_b[l])

    res = [x[0].reshape(Bp, Tp, D_MODEL), x[1].reshape(Bs, Ts, D_MODEL)]
    for key in ("sh", "wkv", "conv", "C", "n", "m", "rt", "gl"):
        res += list(big[key]) if key in big else [jnp.stack(outs[key][0]), jnp.stack(outs[key][1])]
    return tuple(res)
```

```python
import functools
import math

import jax
import jax.numpy as jnp
from jax import lax
from jax.experimental import pallas as pl
from jax.experimental.pallas import tpu as pltpu

F32 = jnp.float32
BF16 = jnp.bfloat16

D_MODEL = 2048
GROUP_W = 512
RW_COLS = 1696
RW_H, RW_HD = 8, 64
HEADS, HEAD_D = 4, 128
GL_DK = 64
CHUNK = 64
ALPHA = 4.0 ** 0.25
LN_EPS = 1e-5
GN_EPS = 1e-5
RW_GN_EPS = 64e-5
GL_TAU = 16.0
PAST_LEN = 16384
N_EXPERTS = 8

LANES = 128
SUBLANES = 8
VMEM_LIMIT = 48 * 1024 * 1024

N_PACKED = 15 * GROUP_W
SM_MLI, SM_MLF, SM_GLA = 160, 164, 168


def _nn(a, b):
    return jnp.dot(a, b, preferred_element_type=F32)


def _nt(a, b):
    return lax.dot_general(a, b, (((1,), (1,)), ((), ())), preferred_element_type=F32)


def _tn(a, b):
    return lax.dot_general(a, b, (((0,), (0,)), ((), ())), preferred_element_type=F32)


def _bf(x):
    return x.astype(BF16)


def _split3(x):
    hi = x.astype(BF16)
    r1 = x - hi.astype(F32)
    mid = r1.astype(BF16)
    lo = (r1 - mid.astype(F32)).astype(BF16)
    return hi, mid, lo


def _dot3(x, m, kind="nn"):
    f = {"nn": _nn, "nt": _nt, "tn": _tn}[kind]
    hi, mid, lo = _split3(x)
    return f(hi, m) + f(mid, m) + f(lo, m)


def _dot3r(m, x, kind="nn"):
    f = {"nn": _nn, "nt": _nt, "tn": _tn}[kind]
    hi, mid, lo = _split3(x)
    return f(m, hi) + f(m, mid) + f(m, lo)


def _sigmoid(x):
    return 1.0 / (1.0 + jnp.exp(-x))


def _softplus(x):
    return jnp.maximum(x, 0.0) + jnp.log1p(jnp.exp(-jnp.abs(x)))


def _log_sigmoid(x):
    return -_softplus(-x)


def _silu(x):
    return x * _sigmoid(x)


def _iota(shape, axis):
    return lax.broadcasted_iota(jnp.int32, shape, axis)


def _shift_rows(x, prev8, j):
    rows = x.shape[0]
    xr = pltpu.roll(x, j, 0)
    pr = pltpu.roll(prev8, j, 0)
    first = jnp.where(_iota((SUBLANES, x.shape[1]), 0) < j, pr, xr[0:SUBLANES])
    if rows == SUBLANES:
        return first
    return jnp.concatenate([first, xr[SUBLANES:]], axis=0)


def _pick(n, cands):
    return next(c for c in cands if n % c == 0)


def _cparams(n_axes):
    return pltpu.CompilerParams(dimension_semantics=("arbitrary",) * n_axes, vmem_limit_bytes=VMEM_LIMIT)


def _mm_kernel(n_in, *refs):
    xs, ws, o_ref = refs[:n_in], refs[n_in:2 * n_in], refs[2 * n_in]
    acc = _nn(xs[0][...], ws[0][...])
    for x_ref, w_ref in zip(xs[1:], ws[1:]):
        acc = acc + _nn(x_ref[...], w_ref[...])
    o_ref[...] = acc


def _mm(xs, w, layer, tn, name):
    M, N = xs[0].shape[0], w.shape[2]
    n_in = len(xs)
    tm = _pick(M, (1024, 512, 256, 128))
    in_specs = ([pl.BlockSpec((tm, x.shape[1]), lambda j, i: (i, 0)) for x in xs]
                + [pl.BlockSpec((None, x.shape[1], tn), functools.partial(lambda g, j, i: (layer, g, j), g))
                   for g, x in enumerate(xs)])
    return pl.pallas_call(
        functools.partial(_mm_kernel, n_in),
        grid=(N // tn, M // tm),
        in_specs=in_specs,
        out_specs=pl.BlockSpec((tm, tn), lambda j, i: (i, j)),
        out_shape=jax.ShapeDtypeStruct((M, N), F32),
        compiler_params=_cparams(2),
        name=name,
    )(*xs, *([w] * n_in))


def _add_ln_kernel(n_y, gated, x_ref, *refs):
    ys = refs[:n_y]
    gs = refs[n_y:2 * n_y] if gated else ()
    w_ref, b_ref, o_ref, ob_ref = refs[-4:]
    z = ALPHA * x_ref[...]
    for i, y_ref in enumerate(ys):
        y = y_ref[...]
        if gated:
            y = gs[i][...] * y
        z = z + y
    mu = jnp.mean(z, axis=-1, keepdims=True)
    zc = z - mu
    var = jnp.mean(zc * zc, axis=-1, keepdims=True)
    out = zc * lax.rsqrt(var + LN_EPS) * w_ref[...] + b_ref[...]
    o_ref[...] = out
    ob_ref[...] = out.astype(BF16)


def _add_ln(x, ys, gates, w, b):
    M = x.shape[0]
    tm = _pick(M, (256, 128))
    gated = gates is not None
    row = pl.BlockSpec((tm, D_MODEL), lambda i: (i, 0))
    in_specs = [row] + [row] * len(ys)
    args = [x] + list(ys)
    if gated:
        in_specs += [pl.BlockSpec((tm, 1), lambda i: (i, 0))] * len(ys)
        args += list(gates)
    in_specs += [pl.BlockSpec((1, D_MODEL), lambda i: (0, 0))] * 2
    args += [w.reshape(1, D_MODEL), b.reshape(1, D_MODEL)]
    return pl.pallas_call(
        functools.partial(_add_ln_kernel, len(ys), gated),
        grid=(M // tm,),
        in_specs=in_specs,
        out_specs=[row, row],
        out_shape=[jax.ShapeDtypeStruct((M, D_MODEL), F32), jax.ShapeDtypeStruct((M, D_MODEL), BF16)],
        compiler_params=_cparams(1),
        name="add_ln",
    )(*args)


def _ffn_up_kernel(te_ref, nu_ref, x_ref, w1_ref, w3_ref, h_ref):
    @pl.when(pl.program_id(0) < nu_ref[0])
    def _():
        x = x_ref[...]
        h_ref[...] = (_silu(_nn(x, w1_ref[...])) * _nn(x, w3_ref[...])).astype(BF16)

    @pl.when(pl.program_id(0) >= nu_ref[0])
    def _():
        h_ref[...] = jnp.zeros(h_ref.shape, BF16)


def _ffn_up_gather_kernel(te_ref, nu_ref, src_ref, x_hbm, w1_ref, w3_ref, h_ref, xbuf, sem):
    i, f = pl.program_id(0), pl.program_id(1)
    tm = xbuf.shape[1]
    n_used = nu_ref[0]

    def row_copy(tile, slot, r):
        src = x_hbm.at[pl.ds(src_ref[tile * tm + r], 1)]
        return pltpu.make_async_copy(src, xbuf.at[slot, pl.ds(r, 1)], sem.at[slot])

    def start_tile(tile, slot):
        def body(r, c):
            row_copy(tile, slot, r).start()
            return c
        lax.fori_loop(0, tm, body, 0, unroll=8)

    def wait_tile(tile, slot):
        def body(r, c):
            row_copy(tile, slot, r).wait()
            return c
        lax.fori_loop(0, tm, body, 0, unroll=8)

    slot = i % 2

    @pl.when((f == 0) & (i == 0) & (n_used > 0))
    def _():
        start_tile(0, 0)

    @pl.when((f == 0) & (i < n_used))
    def _():
        wait_tile(i, slot)

        @pl.when(i + 1 < n_used)
        def _():
            start_tile(i + 1, 1 - slot)

    @pl.when(i < n_used)
    def _():
        x = xbuf[slot].astype(BF16)
        h_ref[...] = (_silu(_nn(x, w1_ref[...])) * _nn(x, w3_ref[...])).astype(BF16)

    @pl.when(i >= n_used)
    def _():
        h_ref[...] = jnp.zeros(h_ref.shape, BF16)


def _ffn_down_kernel(te_ref, nu_ref, h_ref, w2_ref, o_ref):
    @pl.when(pl.program_id(0) < nu_ref[0])
    def _():
        o_ref[...] = _nn(h_ref[...], w2_ref[...])

    @pl.when(pl.program_id(0) >= nu_ref[0])
    def _():
        o_ref[...] = jnp.zeros(o_ref.shape, F32)


FFN_TF = 1408
FFN_TN = 512


def _ffn(xs, tile_expert, n_used, w1, w3, w2, tm, row_src=None):
    R = xs.shape[0] if row_src is None else row_src.shape[0]
    F = w1.shape[2]
    nf, nn = F // FFN_TF, D_MODEL // FFN_TN

    def last(i, j, nu, n):
        return jnp.where(i < nu[0], j, n - 1)

    if row_src is None:
        up_spec = pltpu.PrefetchScalarGridSpec(
            num_scalar_prefetch=2,
            grid=(R // tm, nf),
            in_specs=[
                pl.BlockSpec((tm, D_MODEL), lambda i, f, te, nu: (i, 0)),
                pl.BlockSpec((None, D_MODEL, FFN_TF), lambda i, f, te, nu: (te[i], 0, last(i, f, nu, nf))),
                pl.BlockSpec((None, D_MODEL, FFN_TF), lambda i, f, te, nu: (te[i], 0, last(i, f, nu, nf))),
            ],
            out_specs=pl.BlockSpec((tm, FFN_TF), lambda i, f, te, nu: (i, f)),
        )
        h = pl.pallas_call(
            _ffn_up_kernel,
            grid_spec=up_spec,
            out_shape=jax.ShapeDtypeStruct((R, F), BF16),
            compiler_params=_cparams(2),
            name="ffn_up",
        )(tile_expert, n_used, xs, w1, w3)
    else:
        up_spec = pltpu.PrefetchScalarGridSpec(
            num_scalar_prefetch=3,
            grid=(R // tm, nf),
            in_specs=[
                pl.BlockSpec(memory_space=pl.ANY),
                pl.BlockSpec((None, D_MODEL, FFN_TF), lambda i, f, te, nu, src: (te[i], 0, last(i, f, nu, nf))),
                pl.BlockSpec((None, D_MODEL, FFN_TF), lambda i, f, te, nu, src: (te[i], 0, last(i, f, nu, nf))),
            ],
            out_specs=pl.BlockSpec((tm, FFN_TF), lambda i, f, te, nu, src: (i, f)),
            scratch_shapes=[pltpu.VMEM((2, tm, D_MODEL), F32), pltpu.SemaphoreType.DMA((2,))],
        )
        h = pl.pallas_call(
            _ffn_up_gather_kernel,
            grid_spec=up_spec,
            out_shape=jax.ShapeDtypeStruct((R, F), BF16),
            compiler_params=_cparams(2),
            name="ffn_up_gather",
        )(tile_expert, n_used, row_src, xs, w1, w3)
    down_spec = pltpu.PrefetchScalarGridSpec(
        num_scalar_prefetch=2,
        grid=(R // tm, nn),
        in_specs=[
            pl.BlockSpec((tm, F), lambda i, n, te, nu: (i, 0)),
            pl.BlockSpec((None, F, FFN_TN), lambda i, n, te, nu: (te[i], 0, last(i, n, nu, nn))),
        ],
        out_specs=pl.BlockSpec((tm, FFN_TN), lambda i, n, te, nu: (i, n)),
    )
    return pl.pallas_call(
        _ffn_down_kernel,
        grid_spec=down_spec,
        out_shape=jax.ShapeDtypeStruct((R, D_MODEL), F32),
        compiler_params=_cparams(2),
        name="ffn_down",
    )(tile_expert, n_used, h, w2)


def _router_kernel(x_ref, r_ref, g_ref, i_ref):
    xh, xm, xl = _split3(x_ref[...])
    rh, rm, rl = _split3(r_ref[...])
    logits = (_nn(xh, rh) + _nn(xh, rm) + _nn(xm, rh)) + (_nn(xh, rl) + _nn(xl, rh) + _nn(xm, rm))
    shape = logits.shape
    lane = _iota(shape, 1).astype(F32)
    neg = jnp.float32(-jnp.inf)
    l1 = jnp.where(lane < N_EXPERTS, logits, neg)
    m1 = jnp.max(l1, axis=1, keepdims=True)
    i1 = jnp.min(jnp.where(l1 == m1, lane, float(LANES)), axis=1, keepdims=True)
    l2 = jnp.where(lane == i1, neg, l1)
    m2 = jnp.max(l2, axis=1, keepdims=True)
    i2 = jnp.min(jnp.where(l2 == m2, lane, float(LANES)), axis=1, keepdims=True)
    e = jnp.exp(m2 - m1)
    g1 = 1.0 / (1.0 + e)
    g2 = e / (1.0 + e)
    g_ref[...] = jnp.where(lane == 0.0, g1, jnp.where(lane == 1.0, g2, 0.0))
    i_ref[...] = jnp.where(lane == 0.0, i1, jnp.where(lane == 1.0, i2, 0.0)).astype(jnp.int32)


def _router(x, router_pad):
    M = x.shape[0]
    tm = _pick(M, (512, 256, 128))
    return pl.pallas_call(
        _router_kernel,
        grid=(M // tm,),
        in_specs=[pl.BlockSpec((tm, D_MODEL), lambda i: (i, 0)),
                  pl.BlockSpec((D_MODEL, LANES), lambda i: (0, 0))],
        out_specs=[pl.BlockSpec((tm, LANES), lambda i: (i, 0))] * 2,
        out_shape=[jax.ShapeDtypeStruct((M, LANES), F32), jax.ShapeDtypeStruct((M, LANES), jnp.int32)],
        compiler_params=_cparams(1),
        name="router",
    )(x, router_pad)


def _rw_prep(pm, prm):
    (w0, a0, w2p, a2p, g2p, k_k, k_a, _, _, _, g64, _) = prm
    r = pm[:, 0:512]
    k = pm[:, 512:1024]
    v = pm[:, 1024:1536]
    sm = pm[:, 1536:1792]
    logw = -_softplus(-(w0 + _nn(_bf(jnp.tanh(sm)), w2p))) - 0.5
    lw = -jnp.exp(logw)
    a = _sigmoid(a0 + _nn(_bf(sm), a2p))
    g = _nn(_bf(_sigmoid(sm)), g2p)
    kk = k * k_k
    ss = jnp.concatenate([_dot3((kk * kk)[:, LANES * p:LANES * (p + 1)], g64) for p in range(4)], axis=1)
    kk = kk / jnp.maximum(jnp.sqrt(ss), 1e-12)
    kmod = k * (1.0 + (a - 1.0) * k_a)
    return r, lw, kmod, kk, kk * a, v, g


def _rw_post(y, r, kmod, v, g, prm):
    (_, _, _, _, _, _, _, r_k, ln_w, ln_b, g64, g64m) = prm

    def per_head(x, m):
        return jnp.concatenate([_dot3(x[:, LANES * p:LANES * (p + 1)], m) for p in range(4)], axis=1)

    yc = y - per_head(y, g64m)
    var = per_head(yc * yc, g64m)
    yn = yc * lax.rsqrt(var + RW_GN_EPS) * ln_w + ln_b
    bonus = per_head(r * kmod * r_k, g64) * v
    return (yn + bonus) * g


def _rw_chunks(L, seqs, scr, s_ref):
    q_s, rr_s, kt_s, bt_s, kp_s, bp_s, v_s, wl_s, y_s = scr
    units = [(si, r0, h) for (si, r0) in seqs for h in range(RW_H)]

    def fetch(ref, r0, h):
        return ref[pl.ds(r0, L), RW_HD * h:RW_HD * (h + 1)]

    row = _iota((2 * L, 2 * L), 0)
    col = _iota((2 * L, 2 * L), 1)
    colm = jnp.where(col >= L, col - L, col)
    mask4 = colm <= jnp.where(row < L, row - 1, row - L)
    qf = [fetch(q_s, r0, h) for (_, r0, h) in units]
    rf = [fetch(rr_s, r0, h) for (_, r0, h) in units]
    vb = [_bf(fetch(v_s, r0, h)) for (_, r0, h) in units]
    g4 = [jnp.where(mask4, _nt(_bf(jnp.concatenate([qf[i], rf[i]], axis=0)),
                               _bf(jnp.concatenate([fetch(kt_s, r0, h), fetch(bt_s, r0, h)], axis=0))), 0.0)
          for i, (_, r0, h) in enumerate(units)]
    av = [_nn(_bf(g4[i][:, :L]), vb[i]) for i in range(len(units))]
    x = [jnp.concatenate([qf[i], av[i][:L]], axis=1) for i in range(len(units))]
    pw = [g4[i][:L, L:] for i in range(len(units))]
    x = [x[i] - _nn(_bf(pw[i]), _bf(x[i])) for i in range(len(units))]
    for _ in range(int(math.log2(L)) - 1):
        pw = [_nn(_bf(m), _bf(m)) for m in pw]
        x = [x[i] + _nn(_bf(pw[i]), _bf(x[i])) for i in range(len(units))]
    ry = [jnp.concatenate([rf[i], av[i][L:]], axis=1) - _nn(_bf(g4[i][L:, L:]), _bf(x[i])) for i in range(len(units))]
    gh = [_tn(_bf(x[i]), _bf(fetch(bp_s, r0, h))) for i, (_, r0, h) in enumerate(units)]
    hk = [_tn(vb[i], _bf(fetch(kp_s, r0, h))) for i, (_, r0, h) in enumerate(units)]
    ys = []
    for i, (si, r0, h) in enumerate(units):
        S = s_ref[si, h]
        Sb = _bf(S)
        ys.append(_nt(_bf(ry[i][:, :RW_HD]), Sb) + ry[i][:, RW_HD:])
        w_last = fetch(wl_s, r0, h)[0:1, :]
        s_ref[si, h] = S * w_last - _nn(Sb, _bf(gh[i][:RW_HD])) + hk[i] - gh[i][RW_HD:]
    for i in range(0, len(units), 2):
        _, r0, h = units[i]
        y_s[pl.ds(r0, L), RW_HD * h:RW_HD * (h + 2)] = jnp.concatenate([ys[i], ys[i + 1]], axis=1)


N_RW_PRM = 13


def _rwkv_kernel(is_prompt, nsq, L, grp, n_alias, *refs):
    n_p = nsq if is_prompt else 1
    p_refs = refs[:n_p]
    sh0_ref, s0_ref = refs[n_p:n_p + 2]
    pos = n_p + 2
    prm_refs = refs[pos:pos + N_RW_PRM]
    pos += N_RW_PRM
    tri_ref, ones_ref = refs[pos:pos + 2]
    pos += 2 + n_alias
    o_ref, sh_out_ref, s_out_ref = refs[pos:pos + 3]
    scr = refs[pos + 3:]
    g_s, r_s, k_s = scr[9:12]
    mu = prm_refs[0][...]
    prm = tuple(r[...] for r in prm_refs[1:])
    if is_prompt:
        prev_scr = scr[12]

        @pl.when(pl.program_id(0) == 0)
        def _():
            s_out_ref[...] = s0_ref[...]
            prev_scr[...] = sh0_ref[...]

        ps, pps = [], []
        for s in range(nsq):
            ps.append(p_refs[s][...])
            pps.append(_shift_rows(ps[s], prev_scr[s], 1))
            prev_scr[s] = ps[s][L - SUBLANES:L]
        p = jnp.concatenate(ps, axis=0)
        pp = jnp.concatenate(pps, axis=0)
        sh_out_ref[...] = prev_scr[...]
    else:
        p = p_refs[0][...]
        p3 = p.reshape(nsq, L, p.shape[1])
        pp = jnp.where(_iota(p3.shape, 1) == 0, sh0_ref[...], pltpu.roll(p3, 1, 1)).reshape(p.shape)
        sh_out_ref[...] = p3[:, L - 1:L, :]
        s_out_ref[...] = s0_ref[...]

    r, lw, kmod, kk, bb, v, g = _rw_prep(p + (pp - p) * mu, prm)
    cum = _dot3r(tri_ref[...], lw)
    tot = _dot3r(ones_ref[...], lw)
    e_neg = jnp.exp(-cum)
    e_rem = jnp.exp(tot - cum)
    q_s, rr_s, kt_s, bt_s, kp_s, bp_s, v_s, wl_s, y_s = scr[:9]
    q_s[...] = kk * jnp.exp(cum - lw)
    rr_s[...] = r * jnp.exp(cum)
    kt_s[...] = kmod * e_neg
    bt_s[...] = bb * e_neg
    kp_s[...] = kmod * e_rem
    bp_s[...] = bb * e_rem
    v_s[...] = v
    wl_s[...] = jnp.exp(tot)
    g_s[...], r_s[...], k_s[...] = g, r, kmod

    if is_prompt:
        _rw_chunks(L, [(s, s * L) for s in range(nsq)], scr[:9], s_out_ref)
    else:
        def body(i, c):
            _rw_chunks(L, [(i * grp + j, pl.multiple_of((i * grp + j) * L, SUBLANES)) for j in range(grp)],
                       scr[:9], s_out_ref)
            return c
        lax.fori_loop(0, nsq // grp, body, 0)

    out = _rw_post(y_s[...], r_s[...], k_s[...], v_s[...], g_s[...], prm).astype(BF16)
    if is_prompt:
        for s in range(nsq):
            o_ref[s] = out[s * L:(s + 1) * L]
    else:
        o_ref[...] = out


def _rwkv(P, row0, B, T, sh0, s0, prm_list, s0_layer=None, stacked=None):
    is_prompt = T % CHUNK == 0
    wide = 4 * GROUP_W
    s_shape = (B, RW_H, RW_HD, RW_HD)
    if is_prompt:
        L, nsq, grid = CHUNK, B, (T // CHUNK,)
        rows = nsq * L
        in_specs = [pl.BlockSpec((L, wide), functools.partial(lambda s, c: ((row0 + s * T) // L + c, 0), s))
                    for s in range(nsq)]
        in_specs += [pl.BlockSpec(sh0.shape, lambda c: (0, 0, 0)), pl.BlockSpec(s0.shape, lambda c: (0, 0, 0, 0))]
        out_specs = [pl.BlockSpec((nsq, L, GROUP_W), lambda c: (0, c, 0)),
                     pl.BlockSpec(sh0.shape, lambda c: (0, 0, 0)), pl.BlockSpec(s0.shape, lambda c: (0, 0, 0, 0))]
        out_shape = [jax.ShapeDtypeStruct((nsq, T, GROUP_W), BF16)]
        args = [P] * nsq
    else:
        L, nsq, grid = T, 16, (B // 16,)
        rows = nsq * L
        s_spec = (pl.BlockSpec((nsq, RW_H, RW_HD, RW_HD), lambda i: (i, 0, 0, 0)) if s0_layer is None else
                  pl.BlockSpec((None, nsq, RW_H, RW_HD, RW_HD), lambda i: (s0_layer, i, 0, 0, 0)))
        in_specs = [pl.BlockSpec((rows, wide), lambda i: (row0 // rows + i, 0)),
                    pl.BlockSpec((nsq, 1, wide), lambda i: (i, 0, 0)), s_spec]
        out_specs = [pl.BlockSpec((rows, GROUP_W), lambda i: (i, 0)),
                     pl.BlockSpec((nsq, 1, wide), lambda i: (i, 0, 0)),
                     pl.BlockSpec((nsq, RW_H, RW_HD, RW_HD), lambda i: (i, 0, 0, 0))]
        out_shape = [jax.ShapeDtypeStruct((B * T, GROUP_W), BF16)]
        args = [P]
    seq_of_row = jnp.arange(rows) // L
    same = seq_of_row[:, None] == seq_of_row[None, :]
    tri = (same & (jnp.arange(rows)[:, None] >= jnp.arange(rows)[None, :])).astype(BF16)
    consts = list(prm_list) + [tri, same.astype(BF16)]
    in_specs += [pl.BlockSpec(x.shape, lambda *a: (0, 0)) for x in consts]
    out_shape += [jax.ShapeDtypeStruct(sh0.shape, F32), jax.ShapeDtypeStruct(s_shape, F32)]
    args += [sh0, s0] + consts
    aliases = {}
    if stacked is not None:
        depth, layer, prev = stacked
        blk = out_specs[2].block_shape
        out_specs[2] = pl.BlockSpec((None,) + tuple(blk), (lambda c: (layer, 0, 0, 0, 0)) if is_prompt
                                    else (lambda i: (layer, i, 0, 0, 0)))
        out_shape[2] = jax.ShapeDtypeStruct((depth,) + s_shape, F32)
        if prev is not None:
            aliases[len(args)] = 2
            in_specs.append(pl.BlockSpec(memory_space=pl.ANY))
            args.append(prev)
    scratch = [pltpu.VMEM((rows, GROUP_W), F32)] * 12
    if is_prompt:
        scratch.append(pltpu.VMEM(sh0.shape, F32))
    outs = pl.pallas_call(
        functools.partial(_rwkv_kernel, is_prompt, nsq, L, 2, len(aliases)),
        grid=grid,
        in_specs=in_specs,
        out_specs=out_specs,
        out_shape=out_shape,
        input_output_aliases=aliases,
        scratch_shapes=scratch,
        compiler_params=_cparams(1),
        name="rwkv_prompt" if is_prompt else "rwkv_sample",
    )(*args)
    return outs[0].reshape(B * T, GROUP_W), outs[1], outs[2]


SEQ_GROUP = 4
GL_SAFE_SPREAD = 40.0


def _causal(L):
    return _iota((L, L), 0) >= _iota((L, L), 1)


def _heads(units):
    return [(u, h) for u in range(len(units)) for h in range(HEADS)]


def _hs(h):
    return slice(h * HEAD_D, (h + 1) * HEAD_D)


def _ret_units(L, units, S_ref, consts):
    cosv, sinv, dm_ref, qd_ref, kd_ref, gn, c_dec = consts
    uh = _heads(units)

    def rot(x):
        return x * cosv + pltpu.roll(x, HEAD_D // 2, 1) * sinv

    qr = [rot(units[u][0](h * HEAD_D, HEAD_D)) for u, h in uh]
    kr = [rot(units[u][0](GROUP_W + h * HEAD_D, HEAD_D)) * (HEAD_D ** -0.5) for u, h in uh]
    vb = [_bf(units[u][0](2 * GROUP_W + h * HEAD_D, HEAD_D)) for u, h in uh]
    S = [S_ref[units[u][3], h] for u, h in uh]
    A = [_nt(_bf(qr[i]), _bf(kr[i])) * dm_ref[h] for i, (u, h) in enumerate(uh)]
    o = [_nn(_bf(A[i]), vb[i]) + _nn(_bf(qr[i] * qd_ref[h]), _bf(S[i])) for i, (u, h) in enumerate(uh)]
    for i, (u, h) in enumerate(uh):
        S_ref[units[u][3], h] = c_dec[h] * S[i] + _tn(_bf(kr[i] * kd_ref[h]), vb[i])
    for i, (u, h) in enumerate(uh):
        oc = o[i] - jnp.mean(o[i], axis=1, keepdims=True)
        on = oc * lax.rsqrt(jnp.mean(oc * oc, axis=1, keepdims=True) + GN_EPS) * gn[:, _hs(h)]
        g = units[u][0](3 * GROUP_W + h * HEAD_D, HEAD_D)
        units[u][2](h * HEAD_D, _bf(_silu(g) * on))


def _ml_units(L, units, st_refs, consts):
    C_ref, n_ref, m_ref, cv_ref = st_refs
    ltri, ones_l, conv_w, conv_b, bias_sm, gn = consts
    uh = _heads(units)
    conv, xs, bc_all = [], [], []
    for get, get_sm, _, sidx in units:
        x = get(0, 2 * GROUP_W)
        prev8 = cv_ref[sidx]
        c = (conv_b + _shift_rows(x, prev8, 3) * conv_w[0:1] + _shift_rows(x, prev8, 2) * conv_w[1:2]
             + _shift_rows(x, prev8, 1) * conv_w[2:3] + x * conv_w[3:4])
        cv_ref[sidx] = x[L - SUBLANES:L]
        conv.append(_silu(c))
        xs.append(get_sm(LANES, LANES) + bias_sm)
        bc_all.append(_dot3r(ltri, _log_sigmoid(xs[-1])))
    causal = _causal(L)
    lane0 = _iota((L, LANES), 1) == 0
    q = [conv[u][:, _hs(h)] for u, h in uh]
    k = [conv[u][:, GROUP_W + h * HEAD_D:GROUP_W + (h + 1) * HEAD_D] * (HEAD_D ** -0.5) for u, h in uh]
    vb = [_bf(units[u][0](2 * GROUP_W + h * HEAD_D, HEAD_D)) for u, h in uh]
    ig = [xs[u][:, SM_MLI - LANES + h:SM_MLI - LANES + h + 1] for u, h in uh]
    bc = [bc_all[u][:, SM_MLF - LANES + h:SM_MLF - LANES + h + 1] for u, h in uh]
    m_prev = [m_ref[units[u][3], h][:, 0:1] for u, h in uh]
    row = [_dot3r(ones_l, jnp.where(lane0, ig[i] - bc[i], 0.0), "nt") for i in range(len(uh))]
    dlog = [jnp.where(causal, bc[i] + row[i], -jnp.inf) for i in range(len(uh))]
    inter = [bc[i] + m_prev[i] for i in range(len(uh))]
    m_t = [jnp.maximum(inter[i], jnp.max(dlog[i], axis=1, keepdims=True)) for i in range(len(uh))]
    dmat = [jnp.exp(dlog[i] - m_t[i]) for i in range(len(uh))]
    s_in = [jnp.exp(inter[i] - m_t[i]) for i in range(len(uh))]
    C = [C_ref[units[u][3], h] for u, h in uh]
    n = [n_ref[units[u][3], h] for u, h in uh]
    qb = [_bf(x) for x in q]
    A = [_nt(qb[i], _bf(k[i])) * dmat[i] for i in range(len(uh))]
    num = [_nn(_bf(A[i]), vb[i]) + s_in[i] * _nt(qb[i], _bf(C[i])) for i in range(len(uh))]
    den = [jnp.sum(A[i], axis=1, keepdims=True) + s_in[i] * jnp.sum(q[i] * n[i], axis=1, keepdims=True)
           for i in range(len(uh))]
    hh = [num[i] / jnp.maximum(jnp.abs(den[i]), jnp.exp(-m_t[i])) for i in range(len(uh))]
    for i, (u, h) in enumerate(uh):
        sidx = units[u][3]
        m_new = m_t[i][L - 1:L, :]
        b_last = bc[i][L - 1:L, :]
        carry = jnp.exp(b_last + m_prev[i] - m_new)
        wk = jnp.exp(b_last - bc[i] + ig[i] - m_new) * k[i]
        C_ref[sidx, h] = carry * C[i] + _tn(vb[i], _bf(wk))
        n_ref[sidx, h] = carry * n[i] + jnp.sum(wk, axis=0, keepdims=True)
        m_ref[sidx, h] = jnp.broadcast_to(m_new, (1, LANES))
    for i, (u, h) in enumerate(uh):
        hc = hh[i] - jnp.mean(hh[i], axis=1, keepdims=True)
        hn = hc * lax.rsqrt(jnp.mean(hc * hc, axis=1, keepdims=True) + GN_EPS) * gn[:, _hs(h)]
        o_pre = units[u][0](3 * GROUP_W + h * HEAD_D, HEAD_D)
        units[u][2](h * HEAD_D, _bf(_sigmoid(o_pre) * hn))


def _gl_units(L, units, S_ref, consts):
    ltri, a2p, ab, gn = consts
    uh = _heads(units)
    mid = L // 2 - 1
    qs, kall, bcs, q_st, k_st, e_last = [], [], [], [], [], []
    spread = jnp.float32(0.0)
    for get, get_sm, _, _ in units:
        la = _log_sigmoid(_nn(_bf(get_sm(LANES, LANES)), a2p) + ab) / GL_TAU
        bc = _dot3r(ltri, la)
        b_last = bc[L - 1:L, :]
        qs.append(get(0, GL_DK * HEADS) * (GL_DK ** -0.5))
        kall.append(get(GL_DK * HEADS, GL_DK * HEADS))
        bcs.append(bc)
        spread = jnp.maximum(spread, jnp.max(jnp.abs(bc - bc[mid:mid + 1, :])))
        q_st.append(_bf(qs[-1] * jnp.exp(bc)))
        k_st.append(_bf(kall[-1] * jnp.exp(b_last - bc)))
        e_last.append(jnp.exp(b_last))
    causal = _causal(L)
    eye = _iota((GL_DK, GL_DK), 0) == _iota((GL_DK, GL_DK), 1)

    def ks(h):
        return slice(h * GL_DK, (h + 1) * GL_DK)

    def scores_factored():
        q_in = [_bf(qs[u] * jnp.exp(bcs[u] - bcs[u][mid:mid + 1, :])) for u in range(len(units))]
        k_in = [_bf(kall[u] * jnp.exp(bcs[u][mid:mid + 1, :] - bcs[u])) for u in range(len(units))]
        return [jnp.where(causal, _nt(q_in[u][:, ks(h)], k_in[u][:, ks(h)]), 0.0) for u, h in uh]

    def scores_direct():
        rows = _iota((L, GL_DK * HEADS), 0)
        cols = _iota((L, L), 1)

        def body(s, acc):
            out = []
            for u in range(len(units)):
                pick = rows == s
                k_s = jnp.sum(jnp.where(pick, kall[u], 0.0), axis=0, keepdims=True)
                b_s = jnp.sum(jnp.where(pick, bcs[u], 0.0), axis=0, keepdims=True)
                t = qs[u] * k_s * jnp.exp(jnp.where(rows >= s, bcs[u] - b_s, -jnp.inf))
                for h in range(HEADS):
                    col = jnp.sum(t[:, ks(h)], axis=1, keepdims=True)
                    out.append(jnp.where(cols == s, col, acc[u * HEADS + h]))
            return out

        return lax.fori_loop(0, L, body, [jnp.zeros((L, L), F32) for _ in uh])

    def rest(scores):
        A = scores()
        vb = [_bf(units[u][0](GROUP_W + h * HEAD_D, HEAD_D)) for u, h in uh]
        S = [S_ref[units[u][3], h] for u, h in uh]
        o = [_nn(_bf(A[i]), vb[i]) + _nn(q_st[u][:, ks(h)], _bf(S[i])) for i, (u, h) in enumerate(uh)]
        for i, (u, h) in enumerate(uh):
            e_col = jnp.sum(jnp.where(eye, e_last[u][:, ks(h)], 0.0), axis=1, keepdims=True)
            S_ref[units[u][3], h] = e_col * S[i] + _tn(k_st[u][:, ks(h)], vb[i])
        for i, (u, h) in enumerate(uh):
            on = o[i] * lax.rsqrt(jnp.mean(o[i] * o[i], axis=1, keepdims=True) + GN_EPS) * gn[:, _hs(h)]
            g = units[u][0](2 * GROUP_W + h * HEAD_D, HEAD_D)
            units[u][2](h * HEAD_D, _bf(_silu(g) * on))

    lax.cond(spread < GL_SAFE_SPREAD, lambda: rest(scores_factored), lambda: rest(scores_direct))


def _chunk_kernel(kind, nslab, nunit, L, n_state, n_const, n_alias, has_sm, *refs):
    pos = 0
    p_refs = refs[pos:pos + nslab]; pos += nslab
    sm_refs = refs[pos:pos + (nslab if has_sm else 0)]; pos += (nslab if has_sm else 0)
    st_in = refs[pos:pos + n_state]; pos += n_state
    c_refs = refs[pos:pos + n_const]; pos += n_const + n_alias
    o_ref = refs[pos]; pos += 1
    st_out = refs[pos:pos + n_state]; pos += n_state
    c = pl.program_id(1)

    @pl.when(c == 0)
    def _():
        for a, b in zip(st_in, st_out):
            b[...] = a[...]

    def unit(slab, seq, sidx):
        r0 = seq * L if isinstance(seq, int) else pl.multiple_of(seq * L, SUBLANES)

        def get(col, width):
            return p_refs[slab][pl.ds(r0, L), col:col + width]

        def get_sm(col, width):
            return sm_refs[slab][pl.ds(r0, L), col:col + width]

        def put(col, val):
            o_ref[slab, pl.ds(r0, L), col:col + val.shape[1]] = val

        return (get, get_sm, put, sidx)

    def run(units):
        if kind == "ret":
            cosv, sinv = c_refs[0][...], c_refs[1][...]
            consts = (cosv, sinv, c_refs[2], c_refs[3], c_refs[4], c_refs[5][...], _RT_CDEC[L])
            _ret_units(L, units, st_out[0], consts)
        elif kind == "ml":
            _ml_units(L, units, st_out, tuple(r[...] for r in c_refs))
        else:
            _gl_units(L, units, st_out[0], tuple(r[...] for r in c_refs))

    if nunit == 1:
        run([unit(s, 0, s) for s in range(nslab)])
    else:
        def body(i, carry):
            run([unit(0, i * SEQ_GROUP + j, i * SEQ_GROUP + j) for j in range(SEQ_GROUP)])
            return carry
        lax.fori_loop(0, nunit // SEQ_GROUP, body, 0)


def _rt_log_gamma():
    return [math.log1p(-(2.0 ** (-5.0 - h))) for h in range(HEADS)]


_RT_CDEC = {L: [math.exp(L * lg) for lg in _rt_log_gamma()] for L in (8, CHUNK)}


def _chunk_call(kind, P, col_blk, col_w, sm, row0, nseq, T, states, consts, const_chunked, name, stacked=None):
    has_sm = sm
    if T % CHUNK == 0:
        L, nslab, nunit = CHUNK, nseq, 1
        grid = (1, T // L)
        rows = L
        def pmap(s, i, c):
            return ((row0 + s * T) // L + c, col_blk)
        def smap(s, i, c):
            return ((row0 + s * T) // L + c, 3)
        def omap(i, c):
            return (0, c, 0)
        sblk = nseq
    else:
        L, nslab, nunit = T, 1, 16
        grid = (nseq // nunit, 1)
        rows = L * nunit
        def pmap(s, i, c):
            return (row0 // rows + i, col_blk)
        def smap(s, i, c):
            return (row0 // rows + i, 3)
        def omap(i, c):
            return (0, i, 0)
        sblk = nunit
    in_specs = [pl.BlockSpec((rows, col_w), functools.partial(pmap, s)) for s in range(nslab)]
    args = [P] * nslab
    if has_sm:
        in_specs += [pl.BlockSpec((rows, GROUP_W), functools.partial(smap, s)) for s in range(nslab)]
        args += [P] * nslab
    st_specs, st_shapes = [], []
    for st in states:
        arr, layer = st if isinstance(st, tuple) else (st, None)
        tail = arr.shape[1:] if layer is None else arr.shape[2:]
        nd = len(tail)
        st_specs.append(pl.BlockSpec((sblk,) + tail, lambda i, c, nd=nd: (i,) + (0,) * nd))
        st_shapes.append(jax.ShapeDtypeStruct((nseq,) + tail, F32))
        if layer is None:
            in_specs.append(st_specs[-1])
        else:
            in_specs.append(pl.BlockSpec((None, sblk) + tail, lambda i, c, nd=nd, layer=layer: (layer, i) + (0,) * nd))
        args.append(arr)
    for cst, chunked in zip(consts, const_chunked):
        if chunked:
            in_specs.append(pl.BlockSpec((None,) + cst.shape[1:], lambda i, c: (c, 0, 0)))
        else:
            in_specs.append(pl.BlockSpec(cst.shape, lambda i, c, nd=cst.ndim: (0,) * nd))
        args.append(cst)
    aliases = {}
    if stacked is not None:
        depth, layer, prev = stacked
        tail = st_shapes[0].shape[1:]
        st_specs[0] = pl.BlockSpec((None, sblk) + tail, lambda i, c, nd=len(tail): (layer, i) + (0,) * nd)
        st_shapes[0] = jax.ShapeDtypeStruct((depth, nseq) + tail, F32)
        if prev is not None:
            aliases[len(args)] = 1
            in_specs.append(pl.BlockSpec(memory_space=pl.ANY))
            args.append(prev)
    out_specs = [pl.BlockSpec((nslab, rows, GROUP_W), omap)] + st_specs
    out_shape = [jax.ShapeDtypeStruct((nslab, nseq * T // nslab, GROUP_W), BF16)] + st_shapes
    outs = pl.pallas_call(
        functools.partial(_chunk_kernel, kind, nslab, nunit, L, len(states), len(consts), len(aliases), has_sm),
        grid=grid,
        in_specs=in_specs,
        out_specs=out_specs,
        out_shape=out_shape,
        input_output_aliases=aliases,
        compiler_params=_cparams(2),
        name=name,
    )(*args)
    return outs[0].reshape(nseq * T, GROUP_W), outs[1:]


def _pack_w_in(w_in):
    z = jnp.zeros(w_in.shape[:2] + (GROUP_W - 184,), w_in.dtype)
    parts = [w_in[..., 0:1696], w_in[..., 3744:3752], w_in[..., 6824:6840], z,
             w_in[..., 1696:3744], w_in[..., 3752:5800], w_in[..., 5800:6824], w_in[..., 6840:7352]]
    return jnp.concatenate(parts, axis=-1).astype(BF16)


def _row(v):
    return v.reshape(1, -1).astype(F32)


def _rows_at(mat, r0, rows_total):
    z = jnp.zeros((rows_total, mat.shape[1]), F32).at[r0:r0 + mat.shape[0]].set(mat)
    return z.astype(BF16)


def _ret_consts(pos0, T, L, gn):
    pos = pos0 + jnp.arange(T, dtype=F32)
    inv = 1.0 / (10000.0 ** jnp.linspace(0.0, 1.0, HEAD_D // 2, dtype=F32))
    ang = pos[:, None] * inv[None, :]
    cos, sin = jnp.cos(ang), jnp.sin(ang)
    cos2 = jnp.concatenate([cos, cos], axis=1).reshape(T // L, L, HEAD_D)
    sin2 = jnp.concatenate([-sin, sin], axis=1).reshape(T // L, L, HEAD_D)
    lg = jnp.log1p(-jnp.exp2(-5.0 - jnp.arange(HEADS, dtype=F32)))
    idx = jnp.arange(L, dtype=F32)
    rel = idx[:, None] - idx[None, :]
    dmat = jnp.where(rel >= 0, jnp.exp(jnp.maximum(rel, 0.0) * lg[:, None, None]), 0.0)
    q_dec = jnp.broadcast_to(jnp.exp((idx + 1.0) * lg[:, None])[..., None], (HEADS, L, HEAD_D))
    k_dec = jnp.broadcast_to(jnp.exp((L - 1.0 - idx) * lg[:, None])[..., None], (HEADS, L, HEAD_D))
    return [cos2, sin2, dmat, q_dec, k_dec, _row(gn)], [True, True, False, False, False, False]


def _tri(L):
    return jnp.tril(jnp.ones((L, L), F32)).astype(BF16)


def _moe_plan(ids, tm, n_tiles):
    e = ids.reshape(-1)
    n = e.shape[0]
    onehot = (e[:, None] == jnp.arange(N_EXPERTS, dtype=jnp.int32)[None, :]).astype(jnp.int32)
    rank = jnp.sum((jnp.cumsum(onehot, axis=0) - 1) * onehot, axis=1)
    counts = jnp.sum(onehot, axis=0)
    padded = ((counts + tm - 1) // tm) * tm
    ends = jnp.cumsum(padded)
    starts = ends - padded
    dest = jnp.sum(starts[None, :] * onehot, axis=1) + rank
    row_src = jnp.zeros((n_tiles * tm,), jnp.int32).at[dest].set(jnp.arange(n, dtype=jnp.int32) // 2,
                                                                 unique_indices=True)
    tile_start = jnp.arange(n_tiles, dtype=jnp.int32) * tm
    tile_expert = jnp.minimum(jnp.sum((tile_start[:, None] >= ends[None, :]).astype(jnp.int32), axis=1),
                              N_EXPERTS - 1)
    return row_src, dest.reshape(-1, 2), tile_expert, (ends[-1:] // tm).astype(jnp.int32)


def kernel(x_prompt, x_sample, state_rw_shift, state_rw_wkv, state_ml_conv, state_ml_C, state_ml_n, state_ml_m, state_rt_S, state_gl_S, w_in, rw_mu, rw_w0, rw_w2, rw_a0, rw_a2, rw_g2, rw_kk, rw_ka, rw_rk, rw_ln_w, rw_ln_b, ml_conv_w, ml_conv_b, ml_bi, ml_bf, ml_gn_w, rt_gn_w, gl_a2, gl_ab, gl_gn_w, w_out, ln1_w, ln1_b, ln2_w, ln2_b, ffn_w1, ffn_w3, ffn_w2, moe_router, moe_w1, moe_w3, moe_w2):
    Bp, Tp = x_prompt.shape[:2]
    Bs, Ts = x_sample.shape[:2]
    Mp, Ms = Bp * Tp, Bs * Ts
    M = Mp + Ms
    depth = w_in.shape[0]
    x = [x_prompt.reshape(Mp, D_MODEL), x_sample.reshape(Ms, D_MODEL)]
    xb = [v.astype(BF16) for v in x]
    w_in_p = _pack_w_in(w_in)
    w_out_b = w_out.astype(BF16)
    blockdiag = (jnp.arange(LANES)[:, None] // RW_HD) == (jnp.arange(LANES)[None, :] // RW_HD)
    g64 = blockdiag.astype(BF16)
    g64m = (blockdiag.astype(F32) / RW_HD).astype(BF16)
    pad_sh = 4 * GROUP_W - RW_COLS
    outs = {k: [[], []] for k in ("sh", "conv", "n", "m")}
    big = {"wkv": (RW_H, RW_HD, RW_HD), "C": (HEADS, HEAD_D, HEAD_D), "rt": (HEADS, HEAD_D, HEAD_D),
           "gl": (HEADS, GL_DK, HEAD_D)}
    big = {k: [jnp.zeros((depth, b) + tail, F32) for b in (Bp, Bs)] for k, tail in big.items()}

    for l in range(depth):
        P = [_mm([v], w_in_p, l, GROUP_W, "mm_in") for v in xb]

        prm = [_row(jnp.pad(rw_mu[l], (0, pad_sh))), _row(rw_w0[l]), _row(rw_a0[l]),
               _rows_at(rw_w2[l], 0, 2 * LANES), _rows_at(rw_a2[l], 32, 2 * LANES), _rows_at(rw_g2[l], 64, 2 * LANES),
               _row(rw_kk[l]), _row(rw_ka[l]), _row(rw_rk[l]), _row(rw_ln_w[l]), _row(rw_ln_b[l]), g64, g64m]
        o_rw_p, sh_p, big["wkv"][0] = _rwkv(P[0], 0, Bp, Tp, jnp.zeros((Bp, SUBLANES, 4 * GROUP_W), F32),
                                             jnp.zeros((Bp, RW_H, RW_HD, RW_HD), F32), prm,
                                             stacked=(depth, l, big["wkv"][0]))
        sh_in = jnp.pad(state_rw_shift[l], ((0, 0), (0, pad_sh)))[:, None, :]
        o_rw_s, sh_s, big["wkv"][1] = _rwkv(P[1], 0, Bs, Ts, sh_in, state_rw_wkv, prm, s0_layer=l,
                                             stacked=(depth, l, big["wkv"][1]))
        outs["sh"][0].append(sh_p[:, SUBLANES - 1, :RW_COLS])
        outs["sh"][1].append(sh_s[:, 0, :RW_COLS])

        bias_sm = jnp.zeros((LANES,), F32).at[SM_MLI - LANES:SM_MLI - LANES + HEADS].set(ml_bi[l])
        bias_sm = bias_sm.at[SM_MLF - LANES:SM_MLF - LANES + HEADS].set(ml_bf[l])

        def ml_consts(L):
            return [_tri(L), jnp.ones((L, LANES), BF16), ml_conv_w[l], _row(ml_conv_b[l]), _row(bias_sm), _row(ml_gn_w[l])]

        def ml_states(C, n, m, conv):
            b = n.shape[0]
            return [C, n[:, :, None, :], jnp.broadcast_to(m[:, :, None, None], (b, HEADS, 1, LANES)),
                    jnp.pad(conv, ((0, 0), (SUBLANES - conv.shape[1], 0), (0, 0)))]

        zp = lambda *s: jnp.zeros((Bp,) + s, F32)
        o_ml_p, st_p = _chunk_call("ml", P[0], 1, 4 * GROUP_W, True, 0, Bp, Tp,
                                   ml_states(zp(HEADS, HEAD_D, HEAD_D), zp(HEADS, HEAD_D), zp(HEADS), zp(3, 2 * GROUP_W)),
                                   ml_consts(CHUNK), [False] * 6, "mlstm_prompt", stacked=(depth, l, big["C"][0]))
        o_ml_s, st_s = _chunk_call("ml", P[1], 1, 4 * GROUP_W, True, 0, Bs, Ts,
                                   ml_states((state_ml_C, l), state_ml_n[l], state_ml_m[l], state_ml_conv[l]),
                                   ml_consts(Ts), [False] * 6, "mlstm_sample", stacked=(depth, l, big["C"][1]))
        for g, st in enumerate((st_p, st_s)):
            big["C"][g] = st[0]
            outs["n"][g].append(st[1][:, :, 0, :])
            outs["m"][g].append(st[2][:, :, 0, 0])
            outs["conv"][g].append(st[3][:, SUBLANES - 3:, :])

        c_p, ch_p = _ret_consts(0.0, Tp, CHUNK, rt_gn_w[l])
        c_s, ch_s = _ret_consts(float(PAST_LEN), Ts, Ts, rt_gn_w[l])
        o_rt_p, (big["rt"][0],) = _chunk_call("ret", P[0], 2, 4 * GROUP_W, False, 0, Bp, Tp, [zp(HEADS, HEAD_D, HEAD_D)],
                                              c_p, ch_p, "ret_prompt", stacked=(depth, l, big["rt"][0]))
        o_rt_s, (big["rt"][1],) = _chunk_call("ret", P[1], 2, 4 * GROUP_W, False, 0, Bs, Ts, [(state_rt_S, l)],
                                              c_s, ch_s, "ret_sample", stacked=(depth, l, big["rt"][1]))

        def gl_consts(L):
            return [_tri(L), _rows_at(gl_a2[l], SM_GLA - LANES, LANES), _row(gl_ab[l]), _row(gl_gn_w[l])]

        o_gl_p, (big["gl"][0],) = _chunk_call("gl", P[0], 4, 3 * GROUP_W, True, 0, Bp, Tp, [zp(HEADS, GL_DK, HEAD_D)],
                                              gl_consts(CHUNK), [False] * 4, "gla_prompt",
                                              stacked=(depth, l, big["gl"][0]))
        o_gl_s, (big["gl"][1],) = _chunk_call("gl", P[1], 4, 3 * GROUP_W, True, 0, Bs, Ts, [(state_gl_S, l)],
                                              gl_consts(Ts), [False] * 4, "gla_sample",
                                              stacked=(depth, l, big["gl"][1]))

        mixed = ((o_rw_p, o_ml_p, o_rt_p, o_gl_p), (o_rw_s, o_ml_s, o_rt_s, o_gl_s))
        for g in range(2):
            mix = _mm(list(mixed[g]), w_out_b, l, GROUP_W, "mm_out")
            x[g], xb[g] = _add_ln(x[g], [mix], None, ln1_w[l], ln1_b[l])

        if l % 2 == 0:
            w1, w3, w2 = (w[l // 2][None].astype(BF16) for w in (ffn_w1, ffn_w3, ffn_w2))
            for g in range(2):
                rows = x[g].shape[0]
                tm = _pick(rows, (512, 256, 128))
                y = _ffn(xb[g], jnp.zeros((rows // tm,), jnp.int32), jnp.full((1,), rows // tm, jnp.int32), w1, w3, w2, tm)
                x[g], xb[g] = _add_ln(x[g], [y], None, ln2_w[l], ln2_b[l])
        else:
            tm = _pick(M, (512, 256, 128))
            n_tiles = 2 * M // tm + N_EXPERTS
            router_pad = jnp.pad(moe_router[l // 2], ((0, 0), (0, LANES - N_EXPERTS)))
            routed = [_router(v, router_pad) for v in x]
            gates = [r[0] for r in routed]
            ids = jnp.concatenate([r[1][:, :2] for r in routed], axis=0)
            row_src, dest, tile_expert, n_used = _moe_plan(ids, tm, n_tiles)
            ysorted = _ffn(jnp.concatenate(x, axis=0), tile_expert, n_used, moe_w1[l // 2].astype(BF16),
                           moe_w3[l // 2].astype(BF16), moe_w2[l // 2].astype(BF16), tm, row_src=row_src)
            for g, (r0, r1) in enumerate(((0, Mp), (Mp, M))):
                ys = [jnp.take(ysorted, dest[r0:r1, k], axis=0, mode="clip") for k in range(2)]
                x[g], xb[g] = _add_ln(x[g], ys, [gates[g][:, 0:1], gates[g][:, 1:2]], ln2_w[l], ln2_b[l])

    res = [x[0].reshape(Bp, Tp, D_MODEL), x[1].reshape(Bs, Ts, D_MODEL)]
    for key in ("sh", "wkv", "conv", "C", "n", "m", "rt", "gl"):
        res += list(big[key]) if key in big else [jnp.stack(outs[key][0]), jnp.stack(outs[key][1])]
    return tuple(res)
```

```python
import functools
import math

import jax
import jax.numpy as jnp
from jax import lax
from jax.experimental import pallas as pl
from jax.experimental.pallas import tpu as pltpu

F32 = jnp.float32
BF16 = jnp.bfloat16

D_MODEL = 2048
GROUP_W = 512
RW_COLS = 1696
RW_H, RW_HD = 8, 64
HEADS, HEAD_D = 4, 128
GL_DK = 64
CHUNK = 64
ALPHA = 4.0 ** 0.25
LN_EPS = 1e-5
GN_EPS = 1e-5
RW_GN_EPS = 64e-5
GL_TAU = 16.0
PAST_LEN = 16384
N_EXPERTS = 8

LANES = 128
SUBLANES = 8
VMEM_LIMIT = 48 * 1024 * 1024

N_PACKED = 15 * GROUP_W
SM_MLI, SM_MLF, SM_GLA = 160, 164, 168


def _nn(a, b):
    return jnp.dot(a, b, preferred_element_type=F32)


def _nt(a, b):
    return lax.dot_general(a, b, (((1,), (1,)), ((), ())), preferred_element_type=F32)


def _tn(a, b):
    return lax.dot_general(a, b, (((0,), (0,)), ((), ())), preferred_element_type=F32)


def _bf(x):
    return x.astype(BF16)


def _split3(x):
    hi = x.astype(BF16)
    r1 = x - hi.astype(F32)
    mid = r1.astype(BF16)
    lo = (r1 - mid.astype(F32)).astype(BF16)
    return hi, mid, lo


def _dot3(x, m, kind="nn"):
    f = {"nn": _nn, "nt": _nt, "tn": _tn}[kind]
    hi, mid, lo = _split3(x)
    return f(hi, m) + f(mid, m) + f(lo, m)


def _dot3r(m, x, kind="nn"):
    f = {"nn": _nn, "nt": _nt, "tn": _tn}[kind]
    hi, mid, lo = _split3(x)
    return f(m, hi) + f(m, mid) + f(m, lo)


def _sigmoid(x):
    return 1.0 / (1.0 + jnp.exp(-x))


def _softplus(x):
    return jnp.maximum(x, 0.0) + jnp.log1p(jnp.exp(-jnp.abs(x)))


def _log_sigmoid(x):
    return -_softplus(-x)


def _silu(x):
    return x * _sigmoid(x)


def _iota(shape, axis):
    return lax.broadcasted_iota(jnp.int32, shape, axis)


def _shift_rows(x, prev8, j):
    rows = x.shape[0]
    xr = pltpu.roll(x, j, 0)
    pr = pltpu.roll(prev8, j, 0)
    first = jnp.where(_iota((SUBLANES, x.shape[1]), 0) < j, pr, xr[0:SUBLANES])
    if rows == SUBLANES:
        return first
    return jnp.concatenate([first, xr[SUBLANES:]], axis=0)


def _pick(n, cands):
    return next(c for c in cands if n % c == 0)


def _cparams(n_axes):
    return pltpu.CompilerParams(dimension_semantics=("arbitrary",) * n_axes, vmem_limit_bytes=VMEM_LIMIT)


def _mm_kernel(n_in, *refs):
    xs, ws, o_ref = refs[:n_in], refs[n_in:2 * n_in], refs[2 * n_in]
    acc = _nn(xs[0][...], ws[0][...])
    for x_ref, w_ref in zip(xs[1:], ws[1:]):
        acc = acc + _nn(x_ref[...], w_ref[...])
    o_ref[...] = acc


def _mm(xs, w, layer, tn, name):
    M, N = xs[0].shape[0], w.shape[2]
    n_in = len(xs)
    tm = _pick(M, (1024, 512, 256, 128))
    in_specs = ([pl.BlockSpec((tm, x.shape[1]), lambda j, i: (i, 0)) for x in xs]
                + [pl.BlockSpec((None, x.shape[1], tn), functools.partial(lambda g, j, i: (layer, g, j), g))
                   for g, x in enumerate(xs)])
    return pl.pallas_call(
        functools.partial(_mm_kernel, n_in),
        grid=(N // tn, M // tm),
        in_specs=in_specs,
        out_specs=pl.BlockSpec((tm, tn), lambda j, i: (i, j)),
        out_shape=jax.ShapeDtypeStruct((M, N), F32),
        compiler_params=_cparams(2),
        name=name,
    )(*xs, *([w] * n_in))


def _mm_ln_kernel(n_in, *refs):
    xs, ws = refs[:n_in], refs[n_in:2 * n_in]
    r_ref, w_ref, b_ref, o_ref, ob_ref = refs[2 * n_in:]
    acc = _nn(xs[0][...], ws[0][...])
    for x_ref, wt_ref in zip(xs[1:], ws[1:]):
        acc = acc + _nn(x_ref[...], wt_ref[...])
    out = _layer_norm(ALPHA * r_ref[...] + acc, w_ref[...], b_ref[...])
    o_ref[...] = out
    ob_ref[...] = out.astype(BF16)


def _mm_ln(xs, w, layer, resid, ln_w, ln_b):
    M, N = resid.shape
    n_in = len(xs)
    tm = _pick(M, (512, 256, 128))
    row = pl.BlockSpec((tm, N), lambda i: (i, 0))
    in_specs = ([pl.BlockSpec((tm, x.shape[1]), lambda i: (i, 0)) for x in xs]
                + [pl.BlockSpec((None, x.shape[1], N), functools.partial(lambda g, i: (layer, g, 0), g))
                   for g, x in enumerate(xs)]
                + [row] + [pl.BlockSpec((1, N), lambda i: (0, 0))] * 2)
    return pl.pallas_call(
        functools.partial(_mm_ln_kernel, n_in),
        grid=(M // tm,),
        in_specs=in_specs,
        out_specs=[row, row],
        out_shape=[jax.ShapeDtypeStruct((M, N), F32), jax.ShapeDtypeStruct((M, N), BF16)],
        compiler_params=_cparams(1),
        name="mm_out_ln",
    )(*xs, *([w] * n_in), resid, ln_w.reshape(1, N), ln_b.reshape(1, N))


def _layer_norm(z, w, b):
    mu = jnp.mean(z, axis=-1, keepdims=True)
    zc = z - mu
    var = jnp.mean(zc * zc, axis=-1, keepdims=True)
    return zc * lax.rsqrt(var + LN_EPS) * w + b


def _add_ln_kernel(n_y, gated, x_ref, *refs):
    ys = refs[:n_y]
    gs = refs[n_y:2 * n_y] if gated else ()
    w_ref, b_ref, o_ref, ob_ref = refs[-4:]
    z = ALPHA * x_ref[...]
    for i, y_ref in enumerate(ys):
        y = y_ref[...]
        if gated:
            y = gs[i][...] * y
        z = z + y
    out = _layer_norm(z, w_ref[...], b_ref[...])
    o_ref[...] = out
    ob_ref[...] = out.astype(BF16)


def _add_ln(x, ys, gates, w, b):
    M = x.shape[0]
    tm = _pick(M, (256, 128))
    gated = gates is not None
    row = pl.BlockSpec((tm, D_MODEL), lambda i: (i, 0))
    in_specs = [row] + [row] * len(ys)
    args = [x] + list(ys)
    if gated:
        in_specs += [pl.BlockSpec((tm, 1), lambda i: (i, 0))] * len(ys)
        args += list(gates)
    in_specs += [pl.BlockSpec((1, D_MODEL), lambda i: (0, 0))] * 2
    args += [w.reshape(1, D_MODEL), b.reshape(1, D_MODEL)]
    return pl.pallas_call(
        functools.partial(_add_ln_kernel, len(ys), gated),
        grid=(M // tm,),
        in_specs=in_specs,
        out_specs=[row, row],
        out_shape=[jax.ShapeDtypeStruct((M, D_MODEL), F32), jax.ShapeDtypeStruct((M, D_MODEL), BF16)],
        compiler_params=_cparams(1),
        name="add_ln",
    )(*args)


def _ffn_up_kernel(te_ref, nu_ref, x_ref, w1_ref, w3_ref, h_ref):
    @pl.when(pl.program_id(0) < nu_ref[0])
    def _():
        x = x_ref[...]
        h_ref[...] = (_silu(_nn(x, w1_ref[...])) * _nn(x, w3_ref[...])).astype(BF16)

    @pl.when(pl.program_id(0) >= nu_ref[0])
    def _():
        h_ref[...] = jnp.zeros(h_ref.shape, BF16)


def _ffn_up_gather_kernel(te_ref, nu_ref, src_ref, x_hbm, w1_ref, w3_ref, h_ref, xbuf, sem):
    i, f = pl.program_id(0), pl.program_id(1)
    tm = xbuf.shape[1]
    n_used = nu_ref[0]

    def row_copy(tile, slot, r):
        src = x_hbm.at[pl.ds(src_ref[tile * tm + r], 1)]
        return pltpu.make_async_copy(src, xbuf.at[slot, pl.ds(r, 1)], sem.at[slot])

    def start_tile(tile, slot):
        def body(r, c):
            row_copy(tile, slot, r).start()
            return c
        lax.fori_loop(0, tm, body, 0, unroll=8)

    def wait_tile(slot):
        pltpu.make_async_copy(x_hbm.at[pl.ds(0, tm)], xbuf.at[slot], sem.at[slot]).wait()

    slot = i % 2

    @pl.when((f == 0) & (i == 0) & (n_used > 0))
    def _():
        start_tile(0, 0)

    @pl.when((f == 0) & (i < n_used))
    def _():
        wait_tile(slot)

        @pl.when(i + 1 < n_used)
        def _():
            start_tile(i + 1, 1 - slot)

    @pl.when(i < n_used)
    def _():
        x = xbuf[slot].astype(BF16)
        h_ref[...] = (_silu(_nn(x, w1_ref[...])) * _nn(x, w3_ref[...])).astype(BF16)

    @pl.when(i >= n_used)
    def _():
        h_ref[...] = jnp.zeros(h_ref.shape, BF16)


def _ffn_down_kernel(te_ref, nu_ref, h_ref, w2_ref, o_ref):
    @pl.when(pl.program_id(0) < nu_ref[0])
    def _():
        o_ref[...] = _nn(h_ref[...], w2_ref[...])

    @pl.when(pl.program_id(0) >= nu_ref[0])
    def _():
        o_ref[...] = jnp.zeros(o_ref.shape, F32)


FFN_TF = 1408
FFN_TN = 512


def _ffn(xs, tile_expert, n_used, w1, w3, w2, tm, row_src=None):
    R = xs.shape[0] if row_src is None else row_src.shape[0]
    F = w1.shape[2]
    nf, nn = F // FFN_TF, D_MODEL // FFN_TN

    def last(i, j, nu, n):
        return jnp.where(i < nu[0], j, n - 1)

    if row_src is None:
        up_spec = pltpu.PrefetchScalarGridSpec(
            num_scalar_prefetch=2,
            grid=(R // tm, nf),
            in_specs=[
                pl.BlockSpec((tm, D_MODEL), lambda i, f, te, nu: (i, 0)),
                pl.BlockSpec((None, D_MODEL, FFN_TF), lambda i, f, te, nu: (te[i], 0, last(i, f, nu, nf))),
                pl.BlockSpec((None, D_MODEL, FFN_TF), lambda i, f, te, nu: (te[i], 0, last(i, f, nu, nf))),
            ],
            out_specs=pl.BlockSpec((tm, FFN_TF), lambda i, f, te, nu: (i, f)),
        )
        h = pl.pallas_call(
            _ffn_up_kernel,
            grid_spec=up_spec,
            out_shape=jax.ShapeDtypeStruct((R, F), BF16),
            compiler_params=_cparams(2),
            name="ffn_up",
        )(tile_expert, n_used, xs, w1, w3)
    else:
        up_spec = pltpu.PrefetchScalarGridSpec(
            num_scalar_prefetch=3,
            grid=(R // tm, nf),
            in_specs=[
                pl.BlockSpec(memory_space=pl.ANY),
                pl.BlockSpec((None, D_MODEL, FFN_TF), lambda i, f, te, nu, src: (te[i], 0, last(i, f, nu, nf))),
                pl.BlockSpec((None, D_MODEL, FFN_TF), lambda i, f, te, nu, src: (te[i], 0, last(i, f, nu, nf))),
            ],
            out_specs=pl.BlockSpec((tm, FFN_TF), lambda i, f, te, nu, src: (i, f)),
            scratch_shapes=[pltpu.VMEM((2, tm, D_MODEL), F32), pltpu.SemaphoreType.DMA((2,))],
        )
        h = pl.pallas_call(
            _ffn_up_gather_kernel,
            grid_spec=up_spec,
            out_shape=jax.ShapeDtypeStruct((R, F), BF16),
            compiler_params=_cparams(2),
            name="ffn_up_gather",
        )(tile_expert, n_used, row_src, xs, w1, w3)
    down_spec = pltpu.PrefetchScalarGridSpec(
        num_scalar_prefetch=2,
        grid=(R // tm, nn),
        in_specs=[
            pl.BlockSpec((tm, F), lambda i, n, te, nu: (i, 0)),
            pl.BlockSpec((None, F, FFN_TN), lambda i, n, te, nu: (te[i], 0, last(i, n, nu, nn))),
        ],
        out_specs=pl.BlockSpec((tm, FFN_TN), lambda i, n, te, nu: (i, n)),
    )
    return pl.pallas_call(
        _ffn_down_kernel,
        grid_spec=down_spec,
        out_shape=jax.ShapeDtypeStruct((R, D_MODEL), F32),
        compiler_params=_cparams(2),
        name="ffn_down",
    )(tile_expert, n_used, h, w2)


def _router_kernel(x_ref, r_ref, g_ref, i_ref):
    xh, xm, xl = _split3(x_ref[...])
    rh, rm, rl = _split3(r_ref[...])
    logits = (_nn(xh, rh) + _nn(xh, rm) + _nn(xm, rh)) + (_nn(xh, rl) + _nn(xl, rh) + _nn(xm, rm))
    shape = logits.shape
    lane = _iota(shape, 1).astype(F32)
    neg = jnp.float32(-jnp.inf)
    l1 = jnp.where(lane < N_EXPERTS, logits, neg)
    m1 = jnp.max(l1, axis=1, keepdims=True)
    i1 = jnp.min(jnp.where(l1 == m1, lane, float(LANES)), axis=1, keepdims=True)
    l2 = jnp.where(lane == i1, neg, l1)
    m2 = jnp.max(l2, axis=1, keepdims=True)
    i2 = jnp.min(jnp.where(l2 == m2, lane, float(LANES)), axis=1, keepdims=True)
    e = jnp.exp(m2 - m1)
    g1 = 1.0 / (1.0 + e)
    g2 = e / (1.0 + e)
    g_ref[...] = jnp.where(lane == 0.0, g1, jnp.where(lane == 1.0, g2, 0.0))
    i_ref[...] = jnp.where(lane == 0.0, i1, jnp.where(lane == 1.0, i2, 0.0)).astype(jnp.int32)


def _router(x, router_pad):
    M = x.shape[0]
    tm = _pick(M, (512, 256, 128))
    return pl.pallas_call(
        _router_kernel,
        grid=(M // tm,),
        in_specs=[pl.BlockSpec((tm, D_MODEL), lambda i: (i, 0)),
                  pl.BlockSpec((D_MODEL, LANES), lambda i: (0, 0))],
        out_specs=[pl.BlockSpec((tm, LANES), lambda i: (i, 0))] * 2,
        out_shape=[jax.ShapeDtypeStruct((M, LANES), F32), jax.ShapeDtypeStruct((M, LANES), jnp.int32)],
        compiler_params=_cparams(1),
        name="router",
    )(x, router_pad)


def _rw_prep(pm, prm):
    (w0, a0, w2p, a2p, g2p, k_k, k_a, _, _, _, g64, _) = prm
    r = pm[:, 0:512]
    k = pm[:, 512:1024]
    v = pm[:, 1024:1536]
    sm = pm[:, 1536:1792]
    logw = -_softplus(-(w0 + _nn(_bf(jnp.tanh(sm)), w2p))) - 0.5
    lw = -jnp.exp(logw)
    a = _sigmoid(a0 + _nn(_bf(sm), a2p))
    g = _nn(_bf(_sigmoid(sm)), g2p)
    kk = k * k_k
    ss = jnp.concatenate([_dot3((kk * kk)[:, LANES * p:LANES * (p + 1)], g64) for p in range(4)], axis=1)
    kk = kk / jnp.maximum(jnp.sqrt(ss), 1e-12)
    kmod = k * (1.0 + (a - 1.0) * k_a)
    return r, lw, kmod, kk, kk * a, v, g


def _rw_post(y, r, kmod, v, g, prm):
    (_, _, _, _, _, _, _, r_k, ln_w, ln_b, g64, g64m) = prm

    def per_head(x, m):
        return jnp.concatenate([_dot3(x[:, LANES * p:LANES * (p + 1)], m) for p in range(4)], axis=1)

    yc = y - per_head(y, g64m)
    var = per_head(yc * yc, g64m)
    yn = yc * lax.rsqrt(var + RW_GN_EPS) * ln_w + ln_b
    bonus = per_head(r * kmod * r_k, g64) * v
    return (yn + bonus) * g


def _rw_chunks(L, seqs, scr, s_ref):
    q_s, rr_s, kt_s, bt_s, kp_s, bp_s, v_s, wl_s, y_s = scr
    units = [(si, r0, h) for (si, r0) in seqs for h in range(RW_H)]

    def fetch(ref, r0, h):
        return ref[pl.ds(r0, L), RW_HD * h:RW_HD * (h + 1)]

    row = _iota((2 * L, 2 * L), 0)
    col = _iota((2 * L, 2 * L), 1)
    colm = jnp.where(col >= L, col - L, col)
    mask4 = colm <= jnp.where(row < L, row - 1, row - L)
    qf = [fetch(q_s, r0, h) for (_, r0, h) in units]
    rf = [fetch(rr_s, r0, h) for (_, r0, h) in units]
    vb = [_bf(fetch(v_s, r0, h)) for (_, r0, h) in units]
    g4 = [jnp.where(mask4, _nt(_bf(jnp.concatenate([qf[i], rf[i]], axis=0)),
                               _bf(jnp.concatenate([fetch(kt_s, r0, h), fetch(bt_s, r0, h)], axis=0))), 0.0)
          for i, (_, r0, h) in enumerate(units)]
    av = [_nn(_bf(g4[i][:, :L]), vb[i]) for i in range(len(units))]
    x = [jnp.concatenate([qf[i], av[i][:L]], axis=1) for i in range(len(units))]
    pw = [g4[i][:L, L:] for i in range(len(units))]
    x = [x[i] - _nn(_bf(pw[i]), _bf(x[i])) for i in range(len(units))]
    for _ in range(int(math.log2(L)) - 1):
        pw = [_nn(_bf(m), _bf(m)) for m in pw]
        x = [x[i] + _nn(_bf(pw[i]), _bf(x[i])) for i in range(len(units))]
    ry = [jnp.concatenate([rf[i], av[i][L:]], axis=1) - _nn(_bf(g4[i][L:, L:]), _bf(x[i])) for i in range(len(units))]
    gh = [_tn(_bf(x[i]), _bf(fetch(bp_s, r0, h))) for i, (_, r0, h) in enumerate(units)]
    hk = [_tn(vb[i], _bf(fetch(kp_s, r0, h))) for i, (_, r0, h) in enumerate(units)]
    ys = []
    for i, (si, r0, h) in enumerate(units):
        S = s_ref[si, h]
        Sb = _bf(S)
        ys.append(_nt(_bf(ry[i][:, :RW_HD]), Sb) + ry[i][:, RW_HD:])
        w_last = fetch(wl_s, r0, h)[0:1, :]
        s_ref[si, h] = S * w_last - _nn(Sb, _bf(gh[i][:RW_HD])) + hk[i] - gh[i][RW_HD:]
    for i in range(0, len(units), 2):
        _, r0, h = units[i]
        y_s[pl.ds(r0, L), RW_HD * h:RW_HD * (h + 2)] = jnp.concatenate([ys[i], ys[i + 1]], axis=1)


N_RW_PRM = 13


def _rwkv_kernel(is_prompt, nsq, L, grp, n_alias, *refs):
    n_p = nsq if is_prompt else 1
    p_refs = refs[:n_p]
    sh0_ref, s0_ref = refs[n_p:n_p + 2]
    pos = n_p + 2
    prm_refs = refs[pos:pos + N_RW_PRM]
    pos += N_RW_PRM
    tri_ref = refs[pos]
    pos += 1 + n_alias
    o_ref, sh_out_ref, s_out_ref = refs[pos:pos + 3]
    scr = refs[pos + 3:]
    g_s, r_s, k_s = scr[9:12]
    mu = prm_refs[0][...]
    prm = tuple(r[...] for r in prm_refs[1:])
    if is_prompt:
        prev_scr = scr[12]

        @pl.when(pl.program_id(0) == 0)
        def _():
            s_out_ref[...] = s0_ref[...]
            prev_scr[...] = sh0_ref[...]

        ps, pps = [], []
        for s in range(nsq):
            ps.append(p_refs[s][...])
            pps.append(_shift_rows(ps[s], prev_scr[s], 1))
            prev_scr[s] = ps[s][L - SUBLANES:L]
        p = jnp.concatenate(ps, axis=0)
        pp = jnp.concatenate(pps, axis=0)
        sh_out_ref[...] = prev_scr[...]
    else:
        p = p_refs[0][...]
        p3 = p.reshape(nsq, L, p.shape[1])
        pp = jnp.where(_iota(p3.shape, 1) == 0, sh0_ref[...], pltpu.roll(p3, 1, 1)).reshape(p.shape)
        sh_out_ref[...] = p3[:, L - 1:L, :]
        s_out_ref[...] = s0_ref[...]

    r, lw, kmod, kk, bb, v, g = _rw_prep(p + (pp - p) * mu, prm)
    cum = _dot3r(tri_ref[...], lw)
    cum3 = cum.reshape(nsq, L, cum.shape[1])
    tot = jnp.broadcast_to(cum3[:, L - 1:L, :], cum3.shape).reshape(cum.shape)
    e_neg = jnp.exp(-cum)
    e_rem = jnp.exp(tot - cum)
    q_s, rr_s, kt_s, bt_s, kp_s, bp_s, v_s, wl_s, y_s = scr[:9]
    q_s[...] = kk * jnp.exp(cum - lw)
    rr_s[...] = r * jnp.exp(cum)
    kt_s[...] = kmod * e_neg
    bt_s[...] = bb * e_neg
    kp_s[...] = kmod * e_rem
    bp_s[...] = bb * e_rem
    v_s[...] = v
    wl_s[...] = jnp.exp(tot)
    g_s[...], r_s[...], k_s[...] = g, r, kmod

    if is_prompt:
        _rw_chunks(L, [(s, s * L) for s in range(nsq)], scr[:9], s_out_ref)
    else:
        def body(i, c):
            _rw_chunks(L, [(i * grp + j, pl.multiple_of((i * grp + j) * L, SUBLANES)) for j in range(grp)],
                       scr[:9], s_out_ref)
            return c
        lax.fori_loop(0, nsq // grp, body, 0)

    out = _rw_post(y_s[...], r_s[...], k_s[...], v_s[...], g_s[...], prm).astype(BF16)
    if is_prompt:
        for s in range(nsq):
            o_ref[s] = out[s * L:(s + 1) * L]
    else:
        o_ref[...] = out


def _rwkv(P, row0, B, T, sh0, s0, prm_list, s0_layer=None, stacked=None):
    is_prompt = T % CHUNK == 0
    wide = 4 * GROUP_W
    s_shape = (B, RW_H, RW_HD, RW_HD)
    if is_prompt:
        L, nsq, grid = CHUNK, B, (T // CHUNK,)
        rows = nsq * L
        in_specs = [pl.BlockSpec((L, wide), functools.partial(lambda s, c: ((row0 + s * T) // L + c, 0), s))
                    for s in range(nsq)]
        in_specs += [pl.BlockSpec(sh0.shape, lambda c: (0, 0, 0)), pl.BlockSpec(s0.shape, lambda c: (0, 0, 0, 0))]
        out_specs = [pl.BlockSpec((nsq, L, GROUP_W), lambda c: (0, c, 0)),
                     pl.BlockSpec(sh0.shape, lambda c: (0, 0, 0)), pl.BlockSpec(s0.shape, lambda c: (0, 0, 0, 0))]
        out_shape = [jax.ShapeDtypeStruct((nsq, T, GROUP_W), BF16)]
        args = [P] * nsq
    else:
        L, nsq, grid = T, 16, (B // 16,)
        rows = nsq * L
        s_spec = (pl.BlockSpec((nsq, RW_H, RW_HD, RW_HD), lambda i: (i, 0, 0, 0)) if s0_layer is None else
                  pl.BlockSpec((None, nsq, RW_H, RW_HD, RW_HD), lambda i: (s0_layer, i, 0, 0, 0)))
        in_specs = [pl.BlockSpec((rows, wide), lambda i: (row0 // rows + i, 0)),
                    pl.BlockSpec((nsq, 1, wide), lambda i: (i, 0, 0)), s_spec]
        out_specs = [pl.BlockSpec((rows, GROUP_W), lambda i: (i, 0)),
                     pl.BlockSpec((nsq, 1, wide), lambda i: (i, 0, 0)),
                     pl.BlockSpec((nsq, RW_H, RW_HD, RW_HD), lambda i: (i, 0, 0, 0))]
        out_shape = [jax.ShapeDtypeStruct((B * T, GROUP_W), BF16)]
        args = [P]
    seq_of_row = jnp.arange(rows) // L
    same = seq_of_row[:, None] == seq_of_row[None, :]
    tri = (same & (jnp.arange(rows)[:, None] >= jnp.arange(rows)[None, :])).astype(BF16)
    consts = list(prm_list) + [tri]
    in_specs += [pl.BlockSpec(x.shape, lambda *a: (0, 0)) for x in consts]
    out_shape += [jax.ShapeDtypeStruct(sh0.shape, F32), jax.ShapeDtypeStruct(s_shape, F32)]
    args += [sh0, s0] + consts
    aliases = {}
    if stacked is not None:
        depth, layer, prev = stacked
        blk = out_specs[2].block_shape
        out_specs[2] = pl.BlockSpec((None,) + tuple(blk), (lambda c: (layer, 0, 0, 0, 0)) if is_prompt
                                    else (lambda i: (layer, i, 0, 0, 0)))
        out_shape[2] = jax.ShapeDtypeStruct((depth,) + s_shape, F32)
        if prev is not None:
            aliases[len(args)] = 2
            in_specs.append(pl.BlockSpec(memory_space=pl.ANY))
            args.append(prev)
    scratch = [pltpu.VMEM((rows, GROUP_W), F32)] * 12
    if is_prompt:
        scratch.append(pltpu.VMEM(sh0.shape, F32))
    outs = pl.pallas_call(
        functools.partial(_rwkv_kernel, is_prompt, nsq, L, 2, len(aliases)),
        grid=grid,
        in_specs=in_specs,
        out_specs=out_specs,
        out_shape=out_shape,
        input_output_aliases=aliases,
        scratch_shapes=scratch,
        compiler_params=_cparams(1),
        name="rwkv_prompt" if is_prompt else "rwkv_sample",
    )(*args)
    return outs[0].reshape(B * T, GROUP_W), outs[1], outs[2]


SEQ_GROUP = 4
GL_SAFE_SPREAD = 40.0


def _causal(L):
    return _iota((L, L), 0) >= _iota((L, L), 1)


def _heads(units):
    return [(u, h) for u in range(len(units)) for h in range(HEADS)]


def _hs(h):
    return slice(h * HEAD_D, (h + 1) * HEAD_D)


def _ret_units(L, units, S_ref, consts):
    cosv, sinv, dm_ref, qd_ref, kd_ref, gn, c_dec = consts
    uh = _heads(units)

    def rot(x):
        return x * cosv + pltpu.roll(x, HEAD_D // 2, 1) * sinv

    qr = [rot(units[u][0](h * HEAD_D, HEAD_D)) for u, h in uh]
    kr = [rot(units[u][0](GROUP_W + h * HEAD_D, HEAD_D)) * (HEAD_D ** -0.5) for u, h in uh]
    vb = [_bf(units[u][0](2 * GROUP_W + h * HEAD_D, HEAD_D)) for u, h in uh]
    S = [S_ref[units[u][3], h] for u, h in uh]
    A = [_nt(_bf(qr[i]), _bf(kr[i])) * dm_ref[h] for i, (u, h) in enumerate(uh)]
    o = [_nn(_bf(A[i]), vb[i]) + _nn(_bf(qr[i] * qd_ref[h]), _bf(S[i])) for i, (u, h) in enumerate(uh)]
    for i, (u, h) in enumerate(uh):
        S_ref[units[u][3], h] = c_dec[h] * S[i] + _tn(_bf(kr[i] * kd_ref[h]), vb[i])
    for i, (u, h) in enumerate(uh):
        oc = o[i] - jnp.mean(o[i], axis=1, keepdims=True)
        on = oc * lax.rsqrt(jnp.mean(oc * oc, axis=1, keepdims=True) + GN_EPS) * gn[:, _hs(h)]
        g = units[u][0](3 * GROUP_W + h * HEAD_D, HEAD_D)
        units[u][2](h * HEAD_D, _bf(_silu(g) * on))


def _ml_units(L, units, st_refs, consts):
    C_ref, n_ref, m_ref, cv_ref = st_refs
    ltri, ones_l, conv_w, conv_b, bias_sm, gn = consts
    uh = _heads(units)
    conv, xs, bc_all = [], [], []
    for get, get_sm, _, sidx in units:
        x = get(0, 2 * GROUP_W)
        prev8 = cv_ref[sidx]
        c = (conv_b + _shift_rows(x, prev8, 3) * conv_w[0:1] + _shift_rows(x, prev8, 2) * conv_w[1:2]
             + _shift_rows(x, prev8, 1) * conv_w[2:3] + x * conv_w[3:4])
        cv_ref[sidx] = x[L - SUBLANES:L]
        conv.append(_silu(c))
        xs.append(get_sm(LANES, LANES) + bias_sm)
        bc_all.append(_dot3r(ltri, _log_sigmoid(xs[-1])))
    causal = _causal(L)
    lane0 = _iota((L, LANES), 1) == 0
    q = [conv[u][:, _hs(h)] for u, h in uh]
    k = [conv[u][:, GROUP_W + h * HEAD_D:GROUP_W + (h + 1) * HEAD_D] * (HEAD_D ** -0.5) for u, h in uh]
    vb = [_bf(units[u][0](2 * GROUP_W + h * HEAD_D, HEAD_D)) for u, h in uh]
    ig = [xs[u][:, SM_MLI - LANES + h:SM_MLI - LANES + h + 1] for u, h in uh]
    bc = [bc_all[u][:, SM_MLF - LANES + h:SM_MLF - LANES + h + 1] for u, h in uh]
    m_prev = [m_ref[units[u][3], h][:, 0:1] for u, h in uh]
    row = [_dot3r(ones_l, jnp.where(lane0, ig[i] - bc[i], 0.0), "nt") for i in range(len(uh))]
    dlog = [jnp.where(causal, bc[i] + row[i], -jnp.inf) for i in range(len(uh))]
    inter = [bc[i] + m_prev[i] for i in range(len(uh))]
    m_t = [jnp.maximum(inter[i], jnp.max(dlog[i], axis=1, keepdims=True)) for i in range(len(uh))]
    dmat = [jnp.exp(dlog[i] - m_t[i]) for i in range(len(uh))]
    s_in = [jnp.exp(inter[i] - m_t[i]) for i in range(len(uh))]
    C = [C_ref[units[u][3], h] for u, h in uh]
    n = [n_ref[units[u][3], h] for u, h in uh]
    qb = [_bf(x) for x in q]
    A = [_nt(qb[i], _bf(k[i])) * dmat[i] for i in range(len(uh))]
    num = [_nn(_bf(A[i]), vb[i]) + s_in[i] * _nt(qb[i], _bf(C[i])) for i in range(len(uh))]
    den = [jnp.sum(A[i], axis=1, keepdims=True) + s_in[i] * jnp.sum(q[i] * n[i], axis=1, keepdims=True)
           for i in range(len(uh))]
    hh = [num[i] / jnp.maximum(jnp.abs(den[i]), jnp.exp(-m_t[i])) for i in range(len(uh))]
    for i, (u, h) in enumerate(uh):
        sidx = units[u][3]
        m_new = m_t[i][L - 1:L, :]
        b_last = bc[i][L - 1:L, :]
        carry = jnp.exp(b_last + m_prev[i] - m_new)
        wk = jnp.exp(b_last - bc[i] + ig[i] - m_new) * k[i]
        C_ref[sidx, h] = carry * C[i] + _tn(vb[i], _bf(wk))
        n_ref[sidx, h] = carry * n[i] + jnp.sum(wk, axis=0, keepdims=True)
        m_ref[sidx, h] = jnp.broadcast_to(m_new, (1, LANES))
    for i, (u, h) in enumerate(uh):
        hc = hh[i] - jnp.mean(hh[i], axis=1, keepdims=True)
        hn = hc * lax.rsqrt(jnp.mean(hc * hc, axis=1, keepdims=True) + GN_EPS) * gn[:, _hs(h)]
        o_pre = units[u][0](3 * GROUP_W + h * HEAD_D, HEAD_D)
        units[u][2](h * HEAD_D, _bf(_sigmoid(o_pre) * hn))


def _gl_units(L, units, S_ref, consts):
    ltri, a2p, ab, gn = consts
    uh = _heads(units)
    mid = L // 2 - 1
    qs, kall, bcs, q_st, k_st, e_last = [], [], [], [], [], []
    spread = jnp.float32(0.0)
    for get, get_sm, _, _ in units:
        la = _log_sigmoid(_nn(_bf(get_sm(LANES, LANES)), a2p) + ab) / GL_TAU
        bc = _dot3r(ltri, la)
        b_last = bc[L - 1:L, :]
        qs.append(get(0, GL_DK * HEADS) * (GL_DK ** -0.5))
        kall.append(get(GL_DK * HEADS, GL_DK * HEADS))
        bcs.append(bc)
        spread = jnp.maximum(spread, jnp.max(jnp.abs(bc - bc[mid:mid + 1, :])))
        q_st.append(_bf(qs[-1] * jnp.exp(bc)))
        k_st.append(_bf(kall[-1] * jnp.exp(b_last - bc)))
        e_last.append(jnp.exp(b_last))
    causal = _causal(L)
    eye = _iota((GL_DK, GL_DK), 0) == _iota((GL_DK, GL_DK), 1)

    def ks(h):
        return slice(h * GL_DK, (h + 1) * GL_DK)

    def scores_factored():
        q_in = [_bf(qs[u] * jnp.exp(bcs[u] - bcs[u][mid:mid + 1, :])) for u in range(len(units))]
        k_in = [_bf(kall[u] * jnp.exp(bcs[u][mid:mid + 1, :] - bcs[u])) for u in range(len(units))]
        return [jnp.where(causal, _nt(q_in[u][:, ks(h)], k_in[u][:, ks(h)]), 0.0) for u, h in uh]

    def scores_direct():
        rows = _iota((L, GL_DK * HEADS), 0)
        cols = _iota((L, L), 1)

        def body(s, acc):
            out = []
            for u in range(len(units)):
                pick = rows == s
                k_s = jnp.sum(jnp.where(pick, kall[u], 0.0), axis=0, keepdims=True)
                b_s = jnp.sum(jnp.where(pick, bcs[u], 0.0), axis=0, keepdims=True)
                t = qs[u] * k_s * jnp.exp(jnp.where(rows >= s, bcs[u] - b_s, -jnp.inf))
                for h in range(HEADS):
                    col = jnp.sum(t[:, ks(h)], axis=1, keepdims=True)
                    out.append(jnp.where(cols == s, col, acc[u * HEADS + h]))
            return out

        return lax.fori_loop(0, L, body, [jnp.zeros((L, L), F32) for _ in uh])

    def rest(scores):
        A = scores()
        vb = [_bf(units[u][0](GROUP_W + h * HEAD_D, HEAD_D)) for u, h in uh]
        S = [S_ref[units[u][3], h] for u, h in uh]
        o = [_nn(_bf(A[i]), vb[i]) + _nn(q_st[u][:, ks(h)], _bf(S[i])) for i, (u, h) in enumerate(uh)]
        for i, (u, h) in enumerate(uh):
            e_col = jnp.sum(jnp.where(eye, e_last[u][:, ks(h)], 0.0), axis=1, keepdims=True)
            S_ref[units[u][3], h] = e_col * S[i] + _tn(k_st[u][:, ks(h)], vb[i])
        for i, (u, h) in enumerate(uh):
            on = o[i] * lax.rsqrt(jnp.mean(o[i] * o[i], axis=1, keepdims=True) + GN_EPS) * gn[:, _hs(h)]
            g = units[u][0](2 * GROUP_W + h * HEAD_D, HEAD_D)
            units[u][2](h * HEAD_D, _bf(_silu(g) * on))

    lax.cond(spread < GL_SAFE_SPREAD, lambda: rest(scores_factored), lambda: rest(scores_direct))


def _chunk_kernel(kind, nslab, nunit, L, n_state, n_const, n_alias, has_sm, *refs):
    pos = 0
    p_refs = refs[pos:pos + nslab]; pos += nslab
    sm_refs = refs[pos:pos + (nslab if has_sm else 0)]; pos += (nslab if has_sm else 0)
    st_in = refs[pos:pos + n_state]; pos += n_state
    c_refs = refs[pos:pos + n_const]; pos += n_const + n_alias
    o_ref = refs[pos]; pos += 1
    st_out = refs[pos:pos + n_state]; pos += n_state
    c = pl.program_id(1)

    @pl.when(c == 0)
    def _():
        for a, b in zip(st_in, st_out):
            b[...] = a[...]

    def unit(slab, seq, sidx):
        r0 = seq * L if isinstance(seq, int) else pl.multiple_of(seq * L, SUBLANES)

        def get(col, width):
            return p_refs[slab][pl.ds(r0, L), col:col + width]

        def get_sm(col, width):
            return sm_refs[slab][pl.ds(r0, L), col:col + width]

        def put(col, val):
            o_ref[slab, pl.ds(r0, L), col:col + val.shape[1]] = val

        return (get, get_sm, put, sidx)

    def run(units):
        if kind == "ret":
            cosv, sinv = c_refs[0][...], c_refs[1][...]
            consts = (cosv, sinv, c_refs[2], c_refs[3], c_refs[4], c_refs[5][...], _RT_CDEC[L])
            _ret_units(L, units, st_out[0], consts)
        elif kind == "ml":
            _ml_units(L, units, st_out, tuple(r[...] for r in c_refs))
        else:
            _gl_units(L, units, st_out[0], tuple(r[...] for r in c_refs))

    if nunit == 1:
        run([unit(s, 0, s) for s in range(nslab)])
    else:
        def body(i, carry):
            run([unit(0, i * SEQ_GROUP + j, i * SEQ_GROUP + j) for j in range(SEQ_GROUP)])
            return carry
        lax.fori_loop(0, nunit // SEQ_GROUP, body, 0)


def _rt_log_gamma():
    return [math.log1p(-(2.0 ** (-5.0 - h))) for h in range(HEADS)]


_RT_CDEC = {L: [math.exp(L * lg) for lg in _rt_log_gamma()] for L in (8, CHUNK)}


def _chunk_call(kind, P, col_blk, col_w, sm, row0, nseq, T, states, consts, const_chunked, name, stacked=None):
    has_sm = sm
    if T % CHUNK == 0:
        L, nslab, nunit = CHUNK, nseq, 1
        grid = (1, T // L)
        rows = L
        def pmap(s, i, c):
            return ((row0 + s * T) // L + c, col_blk)
        def smap(s, i, c):
            return ((row0 + s * T) // L + c, 3)
        def omap(i, c):
            return (0, c, 0)
        sblk = nseq
    else:
        L, nslab, nunit = T, 1, 16
        grid = (nseq // nunit, 1)
        rows = L * nunit
        def pmap(s, i, c):
            return (row0 // rows + i, col_blk)
        def smap(s, i, c):
            return (row0 // rows + i, 3)
        def omap(i, c):
            return (0, i, 0)
        sblk = nunit
    in_specs = [pl.BlockSpec((rows, col_w), functools.partial(pmap, s)) for s in range(nslab)]
    args = [P] * nslab
    if has_sm:
        in_specs += [pl.BlockSpec((rows, GROUP_W), functools.partial(smap, s)) for s in range(nslab)]
        args += [P] * nslab
    st_specs, st_shapes = [], []
    for st in states:
        arr, layer = st if isinstance(st, tuple) else (st, None)
        tail = arr.shape[1:] if layer is None else arr.shape[2:]
        nd = len(tail)
        st_specs.append(pl.BlockSpec((sblk,) + tail, lambda i, c, nd=nd: (i,) + (0,) * nd))
        st_shapes.append(jax.ShapeDtypeStruct((nseq,) + tail, F32))
        if layer is None:
            in_specs.append(st_specs[-1])
        else:
            in_specs.append(pl.BlockSpec((None, sblk) + tail, lambda i, c, nd=nd, layer=layer: (layer, i) + (0,) * nd))
        args.append(arr)
    for cst, chunked in zip(consts, const_chunked):
        if chunked:
            in_specs.append(pl.BlockSpec((None,) + cst.shape[1:], lambda i, c: (c, 0, 0)))
        else:
            in_specs.append(pl.BlockSpec(cst.shape, lambda i, c, nd=cst.ndim: (0,) * nd))
        args.append(cst)
    aliases = {}
    if stacked is not None:
        depth, layer, prev = stacked
        tail = st_shapes[0].shape[1:]
        st_specs[0] = pl.BlockSpec((None, sblk) + tail, lambda i, c, nd=len(tail): (layer, i) + (0,) * nd)
        st_shapes[0] = jax.ShapeDtypeStruct((depth, nseq) + tail, F32)
        if prev is not None:
            aliases[len(args)] = 1
            in_specs.append(pl.BlockSpec(memory_space=pl.ANY))
            args.append(prev)
    out_specs = [pl.BlockSpec((nslab, rows, GROUP_W), omap)] + st_specs
    out_shape = [jax.ShapeDtypeStruct((nslab, nseq * T // nslab, GROUP_W), BF16)] + st_shapes
    outs = pl.pallas_call(
        functools.partial(_chunk_kernel, kind, nslab, nunit, L, len(states), len(consts), len(aliases), has_sm),
        grid=grid,
        in_specs=in_specs,
        out_specs=out_specs,
        out_shape=out_shape,
        input_output_aliases=aliases,
        compiler_params=_cparams(2),
        name=name,
    )(*args)
    return outs[0].reshape(nseq * T, GROUP_W), outs[1:]


def _pack_w_in(w_in):
    z = jnp.zeros(w_in.shape[:2] + (GROUP_W - 184,), w_in.dtype)
    parts = [w_in[..., 0:1696], w_in[..., 3744:3752], w_in[..., 6824:6840], z,
             w_in[..., 1696:3744], w_in[..., 3752:5800], w_in[..., 5800:6824], w_in[..., 6840:7352]]
    return jnp.concatenate(parts, axis=-1).astype(BF16)


def _row(v):
    return v.reshape(1, -1).astype(F32)


def _rows_at(mat, r0, rows_total):
    z = jnp.zeros((rows_total, mat.shape[1]), F32).at[r0:r0 + mat.shape[0]].set(mat)
    return z.astype(BF16)


def _ret_consts(pos0, T, L, gn):
    pos = pos0 + jnp.arange(T, dtype=F32)
    inv = 1.0 / (10000.0 ** jnp.linspace(0.0, 1.0, HEAD_D // 2, dtype=F32))
    ang = pos[:, None] * inv[None, :]
    cos, sin = jnp.cos(ang), jnp.sin(ang)
    cos2 = jnp.concatenate([cos, cos], axis=1).reshape(T // L, L, HEAD_D)
    sin2 = jnp.concatenate([-sin, sin], axis=1).reshape(T // L, L, HEAD_D)
    lg = jnp.log1p(-jnp.exp2(-5.0 - jnp.arange(HEADS, dtype=F32)))
    idx = jnp.arange(L, dtype=F32)
    rel = idx[:, None] - idx[None, :]
    dmat = jnp.where(rel >= 0, jnp.exp(jnp.maximum(rel, 0.0) * lg[:, None, None]), 0.0)
    q_dec = jnp.broadcast_to(jnp.exp((idx + 1.0) * lg[:, None])[..., None], (HEADS, L, HEAD_D))
    k_dec = jnp.broadcast_to(jnp.exp((L - 1.0 - idx) * lg[:, None])[..., None], (HEADS, L, HEAD_D))
    return [cos2, sin2, dmat, q_dec, k_dec, _row(gn)], [True, True, False, False, False, False]


def _tri(L):
    return jnp.tril(jnp.ones((L, L), F32)).astype(BF16)


def _moe_plan(ids, tm, n_tiles):
    e = ids.reshape(-1)
    n = e.shape[0]
    onehot = (e[:, None] == jnp.arange(N_EXPERTS, dtype=jnp.int32)[None, :]).astype(jnp.int32)
    rank = jnp.sum((jnp.cumsum(onehot, axis=0) - 1) * onehot, axis=1)
    counts = jnp.sum(onehot, axis=0)
    padded = ((counts + tm - 1) // tm) * tm
    ends = jnp.cumsum(padded)
    starts = ends - padded
    dest = jnp.sum(starts[None, :] * onehot, axis=1) + rank
    row_src = jnp.zeros((n_tiles * tm,), jnp.int32).at[dest].set(jnp.arange(n, dtype=jnp.int32) // 2,
                                                                 unique_indices=True)
    tile_start = jnp.arange(n_tiles, dtype=jnp.int32) * tm
    tile_expert = jnp.minimum(jnp.sum((tile_start[:, None] >= ends[None, :]).astype(jnp.int32), axis=1),
                              N_EXPERTS - 1)
    return row_src, dest.reshape(-1, 2), tile_expert, (ends[-1:] // tm).astype(jnp.int32)


def kernel(x_prompt, x_sample, state_rw_shift, state_rw_wkv, state_ml_conv, state_ml_C, state_ml_n, state_ml_m, state_rt_S, state_gl_S, w_in, rw_mu, rw_w0, rw_w2, rw_a0, rw_a2, rw_g2, rw_kk, rw_ka, rw_rk, rw_ln_w, rw_ln_b, ml_conv_w, ml_conv_b, ml_bi, ml_bf, ml_gn_w, rt_gn_w, gl_a2, gl_ab, gl_gn_w, w_out, ln1_w, ln1_b, ln2_w, ln2_b, ffn_w1, ffn_w3, ffn_w2, moe_router, moe_w1, moe_w3, moe_w2):
    Bp, Tp = x_prompt.shape[:2]
    Bs, Ts = x_sample.shape[:2]
    Mp, Ms = Bp * Tp, Bs * Ts
    M = Mp + Ms
    depth = w_in.shape[0]
    x = [x_prompt.reshape(Mp, D_MODEL), x_sample.reshape(Ms, D_MODEL)]
    xb = [v.astype(BF16) for v in x]
    w_in_p = _pack_w_in(w_in)
    w_out_b = w_out.astype(BF16)
    blockdiag = (jnp.arange(LANES)[:, None] // RW_HD) == (jnp.arange(LANES)[None, :] // RW_HD)
    g64 = blockdiag.astype(BF16)
    g64m = (blockdiag.astype(F32) / RW_HD).astype(BF16)
    pad_sh = 4 * GROUP_W - RW_COLS
    outs = {k: [[], []] for k in ("sh", "conv", "n", "m")}
    big = {"wkv": (RW_H, RW_HD, RW_HD), "C": (HEADS, HEAD_D, HEAD_D), "rt": (HEADS, HEAD_D, HEAD_D),
           "gl": (HEADS, GL_DK, HEAD_D)}
    big = {k: [jnp.zeros((depth, b) + tail, F32) for b in (Bp, Bs)] for k, tail in big.items()}

    for l in range(depth):
        P = [_mm([v], w_in_p, l, 3 * GROUP_W, "mm_in") for v in xb]

        prm = [_row(jnp.pad(rw_mu[l], (0, pad_sh))), _row(rw_w0[l]), _row(rw_a0[l]),
               _rows_at(rw_w2[l], 0, 2 * LANES), _rows_at(rw_a2[l], 32, 2 * LANES), _rows_at(rw_g2[l], 64, 2 * LANES),
               _row(rw_kk[l]), _row(rw_ka[l]), _row(rw_rk[l]), _row(rw_ln_w[l]), _row(rw_ln_b[l]), g64, g64m]
        o_rw_p, sh_p, big["wkv"][0] = _rwkv(P[0], 0, Bp, Tp, jnp.zeros((Bp, SUBLANES, 4 * GROUP_W), F32),
                                             jnp.zeros((Bp, RW_H, RW_HD, RW_HD), F32), prm,
                                             stacked=(depth, l, big["wkv"][0]))
        sh_in = jnp.pad(state_rw_shift[l], ((0, 0), (0, pad_sh)))[:, None, :]
        o_rw_s, sh_s, big["wkv"][1] = _rwkv(P[1], 0, Bs, Ts, sh_in, state_rw_wkv, prm, s0_layer=l,
                                             stacked=(depth, l, big["wkv"][1]))
        outs["sh"][0].append(sh_p[:, SUBLANES - 1, :RW_COLS])
        outs["sh"][1].append(sh_s[:, 0, :RW_COLS])

        bias_sm = jnp.zeros((LANES,), F32).at[SM_MLI - LANES:SM_MLI - LANES + HEADS].set(ml_bi[l])
        bias_sm = bias_sm.at[SM_MLF - LANES:SM_MLF - LANES + HEADS].set(ml_bf[l])

        def ml_consts(L):
            return [_tri(L), jnp.ones((L, LANES), BF16), ml_conv_w[l], _row(ml_conv_b[l]), _row(bias_sm), _row(ml_gn_w[l])]

        def ml_states(C, n, m, conv):
            b = n.shape[0]
            return [C, n[:, :, None, :], jnp.broadcast_to(m[:, :, None, None], (b, HEADS, 1, LANES)),
                    jnp.pad(conv, ((0, 0), (SUBLANES - conv.shape[1], 0), (0, 0)))]

        zp = lambda *s: jnp.zeros((Bp,) + s, F32)
        o_ml_p, st_p = _chunk_call("ml", P[0], 1, 4 * GROUP_W, True, 0, Bp, Tp,
                                   ml_states(zp(HEADS, HEAD_D, HEAD_D), zp(HEADS, HEAD_D), zp(HEADS), zp(3, 2 * GROUP_W)),
                                   ml_consts(CHUNK), [False] * 6, "mlstm_prompt", stacked=(depth, l, big["C"][0]))
        o_ml_s, st_s = _chunk_call("ml", P[1], 1, 4 * GROUP_W, True, 0, Bs, Ts,
                                   ml_states((state_ml_C, l), state_ml_n[l], state_ml_m[l], state_ml_conv[l]),
                                   ml_consts(Ts), [False] * 6, "mlstm_sample", stacked=(depth, l, big["C"][1]))
        for g, st in enumerate((st_p, st_s)):
            big["C"][g] = st[0]
            outs["n"][g].append(st[1][:, :, 0, :])
            outs["m"][g].append(st[2][:, :, 0, 0])
            outs["conv"][g].append(st[3][:, SUBLANES - 3:, :])

        c_p, ch_p = _ret_consts(0.0, Tp, CHUNK, rt_gn_w[l])
        c_s, ch_s = _ret_consts(float(PAST_LEN), Ts, Ts, rt_gn_w[l])
        o_rt_p, (big["rt"][0],) = _chunk_call("ret", P[0], 2, 4 * GROUP_W, False, 0, Bp, Tp, [zp(HEADS, HEAD_D, HEAD_D)],
                                              c_p, ch_p, "ret_prompt", stacked=(depth, l, big["rt"][0]))
        o_rt_s, (big["rt"][1],) = _chunk_call("ret", P[1], 2, 4 * GROUP_W, False, 0, Bs, Ts, [(state_rt_S, l)],
                                              c_s, ch_s, "ret_sample", stacked=(depth, l, big["rt"][1]))

        def gl_consts(L):
            return [_tri(L), _rows_at(gl_a2[l], SM_GLA - LANES, LANES), _row(gl_ab[l]), _row(gl_gn_w[l])]

        o_gl_p, (big["gl"][0],) = _chunk_call("gl", P[0], 4, 3 * GROUP_W, True, 0, Bp, Tp, [zp(HEADS, GL_DK, HEAD_D)],
                                              gl_consts(CHUNK), [False] * 4, "gla_prompt",
                                              stacked=(depth, l, big["gl"][0]))
        o_gl_s, (big["gl"][1],) = _chunk_call("gl", P[1], 4, 3 * GROUP_W, True, 0, Bs, Ts, [(state_gl_S, l)],
                                              gl_consts(Ts), [False] * 4, "gla_sample",
                                              stacked=(depth, l, big["gl"][1]))

        mixed = ((o_rw_p, o_ml_p, o_rt_p, o_gl_p), (o_rw_s, o_ml_s, o_rt_s, o_gl_s))
        for g in range(2):
            x[g], xb[g] = _mm_ln(list(mixed[g]), w_out_b, l, x[g], ln1_w[l], ln1_b[l])

        if l % 2 == 0:
            w1, w3, w2 = (w[l // 2][None].astype(BF16) for w in (ffn_w1, ffn_w3, ffn_w2))
            for g in range(2):
                rows = x[g].shape[0]
                tm = _pick(rows, (512, 256, 128))
                y = _ffn(xb[g], jnp.zeros((rows // tm,), jnp.int32), jnp.full((1,), rows // tm, jnp.int32), w1, w3, w2, tm)
                x[g], xb[g] = _add_ln(x[g], [y], None, ln2_w[l], ln2_b[l])
        else:
            tm = _pick(M, (512, 256, 128))
            n_tiles = 2 * M // tm + N_EXPERTS
            router_pad = jnp.pad(moe_router[l // 2], ((0, 0), (0, LANES - N_EXPERTS)))
            routed = [_router(v, router_pad) for v in x]
            gates = [r[0] for r in routed]
            ids = jnp.concatenate([r[1][:, :2] for r in routed], axis=0)
            row_src, dest, tile_expert, n_used = _moe_plan(ids, tm, n_tiles)
            ysorted = _ffn(jnp.concatenate(x, axis=0), tile_expert, n_used, moe_w1[l // 2].astype(BF16),
                           moe_w3[l // 2].astype(BF16), moe_w2[l // 2].astype(BF16), tm, row_src=row_src)
            for g, (r0, r1) in enumerate(((0, Mp), (Mp, M))):
                ys = [jnp.take(ysorted, dest[r0:r1, k], axis=0, mode="clip") for k in range(2)]
                x[g], xb[g] = _add_ln(x[g], ys, [gates[g][:, 0:1], gates[g][:, 1:2]], ln2_w[l], ln2_b[l])

    res = [x[0].reshape(Bp, Tp, D_MODEL), x[1].reshape(Bs, Ts, D_MODEL)]
    for key in ("sh", "wkv", "conv", "C", "n", "m", "rt", "gl"):
        res += list(big[key]) if key in big else [jnp.stack(outs[key][0]), jnp.stack(outs[key][1])]
    return tuple(res)
```

```python
import functools
import math

import jax
import jax.numpy as jnp
from jax import lax
from jax.experimental import pallas as pl
from jax.experimental.pallas import tpu as pltpu

F32 = jnp.float32
BF16 = jnp.bfloat16

D_MODEL = 2048
GROUP_W = 512
RW_COLS = 1696
RW_H, RW_HD = 8, 64
HEADS, HEAD_D = 4, 128
GL_DK = 64
CHUNK = 64
ALPHA = 4.0 ** 0.25
LN_EPS = 1e-5
GN_EPS = 1e-5
RW_GN_EPS = 64e-5
GL_TAU = 16.0
PAST_LEN = 16384
N_EXPERTS = 8

LANES = 128
SUBLANES = 8
VMEM_LIMIT = 48 * 1024 * 1024

N_PACKED = 15 * GROUP_W
SM_MLI, SM_MLF, SM_GLA = 160, 164, 168


def _nn(a, b):
    return jnp.dot(a, b, preferred_element_type=F32)


def _nt(a, b):
    return lax.dot_general(a, b, (((1,), (1,)), ((), ())), preferred_element_type=F32)


def _tn(a, b):
    return lax.dot_general(a, b, (((0,), (0,)), ((), ())), preferred_element_type=F32)


def _bf(x):
    return x.astype(BF16)


def _split3(x):
    hi = x.astype(BF16)
    r1 = x - hi.astype(F32)
    mid = r1.astype(BF16)
    lo = (r1 - mid.astype(F32)).astype(BF16)
    return hi, mid, lo


def _dot3(x, m, kind="nn"):
    f = {"nn": _nn, "nt": _nt, "tn": _tn}[kind]
    hi, mid, lo = _split3(x)
    return f(hi, m) + f(mid, m) + f(lo, m)


def _dot3r(m, x, kind="nn"):
    f = {"nn": _nn, "nt": _nt, "tn": _tn}[kind]
    hi, mid, lo = _split3(x)
    return f(m, hi) + f(m, mid) + f(m, lo)


def _sigmoid(x):
    return 1.0 / (1.0 + jnp.exp(-x))


def _softplus(x):
    return jnp.maximum(x, 0.0) + jnp.log1p(jnp.exp(-jnp.abs(x)))


def _log_sigmoid(x):
    return -_softplus(-x)


def _silu(x):
    return x * _sigmoid(x)


def _iota(shape, axis):
    return lax.broadcasted_iota(jnp.int32, shape, axis)


def _shift_rows(x, prev8, j):
    rows = x.shape[0]
    xr = pltpu.roll(x, j, 0)
    pr = pltpu.roll(prev8, j, 0)
    first = jnp.where(_iota((SUBLANES, x.shape[1]), 0) < j, pr, xr[0:SUBLANES])
    if rows == SUBLANES:
        return first
    return jnp.concatenate([first, xr[SUBLANES:]], axis=0)


def _pick(n, cands):
    return next(c for c in cands if n % c == 0)


def _cparams(n_axes):
    return pltpu.CompilerParams(dimension_semantics=("arbitrary",) * n_axes, vmem_limit_bytes=VMEM_LIMIT)


def _mm_kernel(n_in, *refs):
    xs, ws, o_ref = refs[:n_in], refs[n_in:2 * n_in], refs[2 * n_in]
    acc = _nn(xs[0][...], ws[0][...])
    for x_ref, w_ref in zip(xs[1:], ws[1:]):
        acc = acc + _nn(x_ref[...], w_ref[...])
    o_ref[...] = acc


def _mm(xs, w, layer, tn, name):
    M, N = xs[0].shape[0], w.shape[2]
    n_in = len(xs)
    tm = _pick(M, (1024, 512, 256, 128))
    in_specs = ([pl.BlockSpec((tm, x.shape[1]), lambda j, i: (i, 0)) for x in xs]
                + [pl.BlockSpec((None, x.shape[1], tn), functools.partial(lambda g, j, i: (layer, g, j), g))
                   for g, x in enumerate(xs)])
    return pl.pallas_call(
        functools.partial(_mm_kernel, n_in),
        grid=(N // tn, M // tm),
        in_specs=in_specs,
        out_specs=pl.BlockSpec((tm, tn), lambda j, i: (i, j)),
        out_shape=jax.ShapeDtypeStruct((M, N), F32),
        compiler_params=_cparams(2),
        name=name,
    )(*xs, *([w] * n_in))


def _mm_ln_kernel(n_in, *refs):
    xs, ws = refs[:n_in], refs[n_in:2 * n_in]
    r_ref, w_ref, b_ref, o_ref, ob_ref = refs[2 * n_in:]
    acc = _nn(xs[0][...], ws[0][...])
    for x_ref, wt_ref in zip(xs[1:], ws[1:]):
        acc = acc + _nn(x_ref[...], wt_ref[...])
    out = _layer_norm(ALPHA * r_ref[...] + acc, w_ref[...], b_ref[...])
    o_ref[...] = out
    ob_ref[...] = out.astype(BF16)


def _mm_ln(xs, w, layer, resid, ln_w, ln_b):
    M, N = resid.shape
    n_in = len(xs)
    tm = _pick(M, (512, 256, 128))
    row = pl.BlockSpec((tm, N), lambda i: (i, 0))
    in_specs = ([pl.BlockSpec((tm, x.shape[1]), lambda i: (i, 0)) for x in xs]
                + [pl.BlockSpec((None, x.shape[1], N), functools.partial(lambda g, i: (layer, g, 0), g))
                   for g, x in enumerate(xs)]
                + [row] + [pl.BlockSpec((1, N), lambda i: (0, 0))] * 2)
    return pl.pallas_call(
        functools.partial(_mm_ln_kernel, n_in),
        grid=(M // tm,),
        in_specs=in_specs,
        out_specs=[row, row],
        out_shape=[jax.ShapeDtypeStruct((M, N), F32), jax.ShapeDtypeStruct((M, N), BF16)],
        compiler_params=_cparams(1),
        name="mm_out_ln",
    )(*xs, *([w] * n_in), resid, ln_w.reshape(1, N), ln_b.reshape(1, N))


def _layer_norm(z, w, b):
    mu = jnp.mean(z, axis=-1, keepdims=True)
    zc = z - mu
    var = jnp.mean(zc * zc, axis=-1, keepdims=True)
    return zc * lax.rsqrt(var + LN_EPS) * w + b


def _add_ln_kernel(n_y, gated, x_ref, *refs):
    ys = refs[:n_y]
    gs = refs[n_y:2 * n_y] if gated else ()
    w_ref, b_ref, o_ref, ob_ref = refs[-4:]
    z = ALPHA * x_ref[...]
    for i, y_ref in enumerate(ys):
        y = y_ref[...]
        if gated:
            y = gs[i][...] * y
        z = z + y
    out = _layer_norm(z, w_ref[...], b_ref[...])
    o_ref[...] = out
    ob_ref[...] = out.astype(BF16)


def _add_ln(x, ys, gates, w, b):
    M = x.shape[0]
    tm = _pick(M, (256, 128))
    gated = gates is not None
    row = pl.BlockSpec((tm, D_MODEL), lambda i: (i, 0))
    in_specs = [row] + [row] * len(ys)
    args = [x] + list(ys)
    if gated:
        in_specs += [pl.BlockSpec((tm, 1), lambda i: (i, 0))] * len(ys)
        args += list(gates)
    in_specs += [pl.BlockSpec((1, D_MODEL), lambda i: (0, 0))] * 2
    args += [w.reshape(1, D_MODEL), b.reshape(1, D_MODEL)]
    return pl.pallas_call(
        functools.partial(_add_ln_kernel, len(ys), gated),
        grid=(M // tm,),
        in_specs=in_specs,
        out_specs=[row, row],
        out_shape=[jax.ShapeDtypeStruct((M, D_MODEL), F32), jax.ShapeDtypeStruct((M, D_MODEL), BF16)],
        compiler_params=_cparams(1),
        name="add_ln",
    )(*args)


def _ffn_up_kernel(te_ref, nu_ref, x_ref, w1_ref, w3_ref, h_ref):
    @pl.when(pl.program_id(0) < nu_ref[0])
    def _():
        x = x_ref[...]
        h_ref[...] = (_silu(_nn(x, w1_ref[...])) * _nn(x, w3_ref[...])).astype(BF16)

    @pl.when(pl.program_id(0) >= nu_ref[0])
    def _():
        h_ref[...] = jnp.zeros(h_ref.shape, BF16)


def _ffn_up_gather_kernel(te_ref, nu_ref, src_ref, x_hbm, w1_ref, w3_ref, h_ref, xbuf, sem):
    i, f = pl.program_id(0), pl.program_id(1)
    tm = xbuf.shape[1]
    n_used = nu_ref[0]

    def row_copy(tile, slot, r):
        src = x_hbm.at[pl.ds(src_ref[tile * tm + r], 1)]
        return pltpu.make_async_copy(src, xbuf.at[slot, pl.ds(r, 1)], sem.at[slot])

    def start_tile(tile, slot):
        def body(r, c):
            row_copy(tile, slot, r).start()
            return c
        lax.fori_loop(0, tm, body, 0, unroll=8)

    def wait_tile(slot):
        pltpu.make_async_copy(x_hbm.at[pl.ds(0, tm)], xbuf.at[slot], sem.at[slot]).wait()

    slot = i % 2

    @pl.when((f == 0) & (i == 0) & (n_used > 0))
    def _():
        start_tile(0, 0)

    @pl.when((f == 0) & (i < n_used))
    def _():
        wait_tile(slot)

        @pl.when(i + 1 < n_used)
        def _():
            start_tile(i + 1, 1 - slot)

    @pl.when(i < n_used)
    def _():
        x = xbuf[slot].astype(BF16)
        h_ref[...] = (_silu(_nn(x, w1_ref[...])) * _nn(x, w3_ref[...])).astype(BF16)

    @pl.when(i >= n_used)
    def _():
        h_ref[...] = jnp.zeros(h_ref.shape, BF16)


def _ffn_down_kernel(te_ref, nu_ref, h_ref, w2_ref, o_ref):
    @pl.when(pl.program_id(0) < nu_ref[0])
    def _():
        o_ref[...] = _nn(h_ref[...], w2_ref[...])

    @pl.when(pl.program_id(0) >= nu_ref[0])
    def _():
        o_ref[...] = jnp.zeros(o_ref.shape, F32)


FFN_TF = 1408
FFN_TN = 512
FFN_DOWN_TILE_ELEMS = 2816 * 2048


def _ffn(xs, tile_expert, n_used, w1, w3, w2, tm, row_src=None):
    R = xs.shape[0] if row_src is None else row_src.shape[0]
    F = w1.shape[2]
    tn = FFN_TN * (FFN_DOWN_TILE_ELEMS // (F * FFN_TN))
    nf, nn = F // FFN_TF, D_MODEL // tn

    def last(i, j, nu, n):
        return jnp.where(i < nu[0], j, n - 1)

    if row_src is None:
        up_spec = pltpu.PrefetchScalarGridSpec(
            num_scalar_prefetch=2,
            grid=(R // tm, nf),
            in_specs=[
                pl.BlockSpec((tm, D_MODEL), lambda i, f, te, nu: (i, 0)),
                pl.BlockSpec((None, D_MODEL, FFN_TF), lambda i, f, te, nu: (te[i], 0, last(i, f, nu, nf))),
                pl.BlockSpec((None, D_MODEL, FFN_TF), lambda i, f, te, nu: (te[i], 0, last(i, f, nu, nf))),
            ],
            out_specs=pl.BlockSpec((tm, FFN_TF), lambda i, f, te, nu: (i, f)),
        )
        h = pl.pallas_call(
            _ffn_up_kernel,
            grid_spec=up_spec,
            out_shape=jax.ShapeDtypeStruct((R, F), BF16),
            compiler_params=_cparams(2),
            name="ffn_up",
        )(tile_expert, n_used, xs, w1, w3)
    else:
        up_spec = pltpu.PrefetchScalarGridSpec(
            num_scalar_prefetch=3,
            grid=(R // tm, nf),
            in_specs=[
                pl.BlockSpec(memory_space=pl.ANY),
                pl.BlockSpec((None, D_MODEL, FFN_TF), lambda i, f, te, nu, src: (te[i], 0, last(i, f, nu, nf))),
                pl.BlockSpec((None, D_MODEL, FFN_TF), lambda i, f, te, nu, src: (te[i], 0, last(i, f, nu, nf))),
            ],
            out_specs=pl.BlockSpec((tm, FFN_TF), lambda i, f, te, nu, src: (i, f)),
            scratch_shapes=[pltpu.VMEM((2, tm, D_MODEL), F32), pltpu.SemaphoreType.DMA((2,))],
        )
        h = pl.pallas_call(
            _ffn_up_gather_kernel,
            grid_spec=up_spec,
            out_shape=jax.ShapeDtypeStruct((R, F), BF16),
            compiler_params=_cparams(2),
            name="ffn_up_gather",
        )(tile_expert, n_used, row_src, xs, w1, w3)
    down_spec = pltpu.PrefetchScalarGridSpec(
        num_scalar_prefetch=2,
        grid=(R // tm, nn),
        in_specs=[
            pl.BlockSpec((tm, F), lambda i, n, te, nu: (i, 0)),
            pl.BlockSpec((None, F, tn), lambda i, n, te, nu: (te[i], 0, last(i, n, nu, nn))),
        ],
        out_specs=pl.BlockSpec((tm, tn), lambda i, n, te, nu: (i, n)),
    )
    return pl.pallas_call(
        _ffn_down_kernel,
        grid_spec=down_spec,
        out_shape=jax.ShapeDtypeStruct((R, D_MODEL), F32),
        compiler_params=_cparams(2),
        name="ffn_down",
    )(tile_expert, n_used, h, w2)


def _router_kernel(x_ref, r_ref, g_ref, i_ref):
    xh, xm, xl = _split3(x_ref[...])
    rh, rm, rl = _split3(r_ref[...])
    logits = (_nn(xh, rh) + _nn(xh, rm) + _nn(xm, rh)) + (_nn(xh, rl) + _nn(xl, rh) + _nn(xm, rm))
    shape = logits.shape
    lane = _iota(shape, 1).astype(F32)
    neg = jnp.float32(-jnp.inf)
    l1 = jnp.where(lane < N_EXPERTS, logits, neg)
    m1 = jnp.max(l1, axis=1, keepdims=True)
    i1 = jnp.min(jnp.where(l1 == m1, lane, float(LANES)), axis=1, keepdims=True)
    l2 = jnp.where(lane == i1, neg, l1)
    m2 = jnp.max(l2, axis=1, keepdims=True)
    i2 = jnp.min(jnp.where(l2 == m2, lane, float(LANES)), axis=1, keepdims=True)
    e = jnp.exp(m2 - m1)
    g1 = 1.0 / (1.0 + e)
    g2 = e / (1.0 + e)
    g_ref[...] = jnp.where(lane == 0.0, g1, jnp.where(lane == 1.0, g2, 0.0))
    i_ref[...] = jnp.where(lane == 0.0, i1, jnp.where(lane == 1.0, i2, 0.0)).astype(jnp.int32)


def _router(x, router_pad):
    M = x.shape[0]
    tm = _pick(M, (512, 256, 128))
    return pl.pallas_call(
        _router_kernel,
        grid=(M // tm,),
        in_specs=[pl.BlockSpec((tm, D_MODEL), lambda i: (i, 0)),
                  pl.BlockSpec((D_MODEL, LANES), lambda i: (0, 0))],
        out_specs=[pl.BlockSpec((tm, LANES), lambda i: (i, 0))] * 2,
        out_shape=[jax.ShapeDtypeStruct((M, LANES), F32), jax.ShapeDtypeStruct((M, LANES), jnp.int32)],
        compiler_params=_cparams(1),
        name="router",
    )(x, router_pad)


def _rw_prep(pm, prm):
    (w0, a0, w2p, a2p, g2p, k_k, k_a, _, _, _, g64, _) = prm
    r = pm[:, 0:512]
    k = pm[:, 512:1024]
    v = pm[:, 1024:1536]
    sm = pm[:, 1536:1792]
    logw = -_softplus(-(w0 + _nn(_bf(jnp.tanh(sm)), w2p))) - 0.5
    lw = -jnp.exp(logw)
    a = _sigmoid(a0 + _nn(_bf(sm), a2p))
    g = _nn(_bf(_sigmoid(sm)), g2p)
    kk = k * k_k
    ss = jnp.concatenate([_dot3((kk * kk)[:, LANES * p:LANES * (p + 1)], g64) for p in range(4)], axis=1)
    kk = kk / jnp.maximum(jnp.sqrt(ss), 1e-12)
    kmod = k * (1.0 + (a - 1.0) * k_a)
    return r, lw, kmod, kk, kk * a, v, g


def _rw_post(y, r, kmod, v, g, prm):
    (_, _, _, _, _, _, _, r_k, ln_w, ln_b, g64, g64m) = prm

    def per_head(x, m):
        return jnp.concatenate([_dot3(x[:, LANES * p:LANES * (p + 1)], m) for p in range(4)], axis=1)

    yc = y - per_head(y, g64m)
    var = per_head(yc * yc, g64m)
    yn = yc * lax.rsqrt(var + RW_GN_EPS) * ln_w + ln_b
    bonus = per_head(r * kmod * r_k, g64) * v
    return (yn + bonus) * g


def _rw_chunks(L, seqs, scr, s_ref):
    q_s, rr_s, kt_s, bt_s, kp_s, bp_s, v_s, wl_s, y_s = scr
    units = [(si, r0, h) for (si, r0) in seqs for h in range(RW_H)]

    def fetch(ref, r0, h):
        return ref[pl.ds(r0, L), RW_HD * h:RW_HD * (h + 1)]

    row = _iota((2 * L, 2 * L), 0)
    col = _iota((2 * L, 2 * L), 1)
    colm = jnp.where(col >= L, col - L, col)
    mask4 = colm <= jnp.where(row < L, row - 1, row - L)
    qf = [fetch(q_s, r0, h) for (_, r0, h) in units]
    rf = [fetch(rr_s, r0, h) for (_, r0, h) in units]
    vb = [_bf(fetch(v_s, r0, h)) for (_, r0, h) in units]
    g4 = [jnp.where(mask4, _nt(_bf(jnp.concatenate([qf[i], rf[i]], axis=0)),
                               _bf(jnp.concatenate([fetch(kt_s, r0, h), fetch(bt_s, r0, h)], axis=0))), 0.0)
          for i, (_, r0, h) in enumerate(units)]
    av = [_nn(_bf(g4[i][:, :L]), vb[i]) for i in range(len(units))]
    x = [jnp.concatenate([qf[i], av[i][:L]], axis=1) for i in range(len(units))]
    pw = [g4[i][:L, L:] for i in range(len(units))]
    x = [x[i] - _nn(_bf(pw[i]), _bf(x[i])) for i in range(len(units))]
    for _ in range(int(math.log2(L)) - 1):
        pw = [_nn(_bf(m), _bf(m)) for m in pw]
        x = [x[i] + _nn(_bf(pw[i]), _bf(x[i])) for i in range(len(units))]
    ry = [jnp.concatenate([rf[i], av[i][L:]], axis=1) - _nn(_bf(g4[i][L:, L:]), _bf(x[i])) for i in range(len(units))]
    gh = [_tn(_bf(x[i]), _bf(fetch(bp_s, r0, h))) for i, (_, r0, h) in enumerate(units)]
    hk = [_tn(vb[i], _bf(fetch(kp_s, r0, h))) for i, (_, r0, h) in enumerate(units)]
    ys = []
    for i, (si, r0, h) in enumerate(units):
        S = s_ref[si, h]
        Sb = _bf(S)
        ys.append(_nt(_bf(ry[i][:, :RW_HD]), Sb) + ry[i][:, RW_HD:])
        w_last = fetch(wl_s, r0, h)[0:1, :]
        s_ref[si, h] = S * w_last - _nn(Sb, _bf(gh[i][:RW_HD])) + hk[i] - gh[i][RW_HD:]
    for i in range(0, len(units), 2):
        _, r0, h = units[i]
        y_s[pl.ds(r0, L), RW_HD * h:RW_HD * (h + 2)] = jnp.concatenate([ys[i], ys[i + 1]], axis=1)


N_RW_PRM = 13


def _rwkv_kernel(is_prompt, nsq, L, grp, n_alias, *refs):
    n_p = nsq if is_prompt else 1
    p_refs = refs[:n_p]
    sh0_ref, s0_ref = refs[n_p:n_p + 2]
    pos = n_p + 2
    prm_refs = refs[pos:pos + N_RW_PRM]
    pos += N_RW_PRM
    tri_ref = refs[pos]
    pos += 1 + n_alias
    o_ref, sh_out_ref, s_out_ref = refs[pos:pos + 3]
    scr = refs[pos + 3:]
    g_s, r_s, k_s = scr[9:12]
    mu = prm_refs[0][...]
    prm = tuple(r[...] for r in prm_refs[1:])
    if is_prompt:
        prev_scr = scr[12]

        @pl.when(pl.program_id(0) == 0)
        def _():
            s_out_ref[...] = s0_ref[...]
            prev_scr[...] = sh0_ref[...]

        ps, pps = [], []
        for s in range(nsq):
            ps.append(p_refs[s][...])
            pps.append(_shift_rows(ps[s], prev_scr[s], 1))
            prev_scr[s] = ps[s][L - SUBLANES:L]
        p = jnp.concatenate(ps, axis=0)
        pp = jnp.concatenate(pps, axis=0)
        sh_out_ref[...] = prev_scr[...]
    else:
        p = p_refs[0][...]
        p3 = p.reshape(nsq, L, p.shape[1])
        pp = jnp.where(_iota(p3.shape, 1) == 0, sh0_ref[...], pltpu.roll(p3, 1, 1)).reshape(p.shape)
        sh_out_ref[...] = p3[:, L - 1:L, :]
        s_out_ref[...] = s0_ref[...]

    r, lw, kmod, kk, bb, v, g = _rw_prep(p + (pp - p) * mu, prm)
    cum = _dot3r(tri_ref[...], lw)
    cum3 = cum.reshape(nsq, L, cum.shape[1])
    tot = jnp.broadcast_to(cum3[:, L - 1:L, :], cum3.shape).reshape(cum.shape)
    e_neg = jnp.exp(-cum)
    e_rem = jnp.exp(tot - cum)
    q_s, rr_s, kt_s, bt_s, kp_s, bp_s, v_s, wl_s, y_s = scr[:9]
    q_s[...] = kk * jnp.exp(cum - lw)
    rr_s[...] = r * jnp.exp(cum)
    kt_s[...] = kmod * e_neg
    bt_s[...] = bb * e_neg
    kp_s[...] = kmod * e_rem
    bp_s[...] = bb * e_rem
    v_s[...] = v
    wl_s[...] = jnp.exp(tot)
    g_s[...], r_s[...], k_s[...] = g, r, kmod

    if is_prompt:
        _rw_chunks(L, [(s, s * L) for s in range(nsq)], scr[:9], s_out_ref)
    else:
        def body(i, c):
            _rw_chunks(L, [(i * grp + j, pl.multiple_of((i * grp + j) * L, SUBLANES)) for j in range(grp)],
                       scr[:9], s_out_ref)
            return c
        lax.fori_loop(0, nsq // grp, body, 0)

    out = _rw_post(y_s[...], r_s[...], k_s[...], v_s[...], g_s[...], prm).astype(BF16)
    if is_prompt:
        for s in range(nsq):
            o_ref[s] = out[s * L:(s + 1) * L]
    else:
        o_ref[...] = out


def _rwkv(P, row0, B, T, sh0, s0, prm_list, s0_layer=None, stacked=None):
    is_prompt = T % CHUNK == 0
    wide = 4 * GROUP_W
    s_shape = (B, RW_H, RW_HD, RW_HD)
    if is_prompt:
        L, nsq, grid = CHUNK, B, (T // CHUNK,)
        rows = nsq * L
        in_specs = [pl.BlockSpec((L, wide), functools.partial(lambda s, c: ((row0 + s * T) // L + c, 0), s))
                    for s in range(nsq)]
        in_specs += [pl.BlockSpec(sh0.shape, lambda c: (0, 0, 0)), pl.BlockSpec(s0.shape, lambda c: (0, 0, 0, 0))]
        out_specs = [pl.BlockSpec((nsq, L, GROUP_W), lambda c: (0, c, 0)),
                     pl.BlockSpec(sh0.shape, lambda c: (0, 0, 0)), pl.BlockSpec(s0.shape, lambda c: (0, 0, 0, 0))]
        out_shape = [jax.ShapeDtypeStruct((nsq, T, GROUP_W), BF16)]
        args = [P] * nsq
    else:
        L, nsq, grid = T, 16, (B // 16,)
        rows = nsq * L
        s_spec = (pl.BlockSpec((nsq, RW_H, RW_HD, RW_HD), lambda i: (i, 0, 0, 0)) if s0_layer is None else
                  pl.BlockSpec((None, nsq, RW_H, RW_HD, RW_HD), lambda i: (s0_layer, i, 0, 0, 0)))
        in_specs = [pl.BlockSpec((rows, wide), lambda i: (row0 // rows + i, 0)),
                    pl.BlockSpec((nsq, 1, wide), lambda i: (i, 0, 0)), s_spec]
        out_specs = [pl.BlockSpec((rows, GROUP_W), lambda i: (i, 0)),
                     pl.BlockSpec((nsq, 1, wide), lambda i: (i, 0, 0)),
                     pl.BlockSpec((nsq, RW_H, RW_HD, RW_HD), lambda i: (i, 0, 0, 0))]
        out_shape = [jax.ShapeDtypeStruct((B * T, GROUP_W), BF16)]
        args = [P]
    seq_of_row = jnp.arange(rows) // L
    same = seq_of_row[:, None] == seq_of_row[None, :]
    tri = (same & (jnp.arange(rows)[:, None] >= jnp.arange(rows)[None, :])).astype(BF16)
    consts = list(prm_list) + [tri]
    in_specs += [pl.BlockSpec(x.shape, lambda *a: (0, 0)) for x in consts]
    out_shape += [jax.ShapeDtypeStruct(sh0.shape, F32), jax.ShapeDtypeStruct(s_shape, F32)]
    args += [sh0, s0] + consts
    aliases = {}
    if stacked is not None:
        depth, layer, prev = stacked
        blk = out_specs[2].block_shape
        out_specs[2] = pl.BlockSpec((None,) + tuple(blk), (lambda c: (layer, 0, 0, 0, 0)) if is_prompt
                                    else (lambda i: (layer, i, 0, 0, 0)))
        out_shape[2] = jax.ShapeDtypeStruct((depth,) + s_shape, F32)
        if prev is not None:
            aliases[len(args)] = 2
            in_specs.append(pl.BlockSpec(memory_space=pl.ANY))
            args.append(prev)
    scratch = [pltpu.VMEM((rows, GROUP_W), F32)] * 12
    if is_prompt:
        scratch.append(pltpu.VMEM(sh0.shape, F32))
    outs = pl.pallas_call(
        functools.partial(_rwkv_kernel, is_prompt, nsq, L, RW_SEQ_GROUP, len(aliases)),
        grid=grid,
        in_specs=in_specs,
        out_specs=out_specs,
        out_shape=out_shape,
        input_output_aliases=aliases,
        scratch_shapes=scratch,
        compiler_params=_cparams(1),
        name="rwkv_prompt" if is_prompt else "rwkv_sample",
    )(*args)
    return outs[0].reshape(B * T, GROUP_W), outs[1], outs[2]


SEQ_GROUP = 4
RW_SEQ_GROUP = 8
GL_SAFE_SPREAD = 40.0


def _causal(L):
    return _iota((L, L), 0) >= _iota((L, L), 1)


def _heads(units):
    return [(u, h) for u in range(len(units)) for h in range(HEADS)]


def _hs(h):
    return slice(h * HEAD_D, (h + 1) * HEAD_D)


def _ret_units(L, units, S_ref, consts):
    cosv, sinv, dm_ref, qd_ref, kd_ref, gn, c_dec = consts
    uh = _heads(units)

    def rot(x):
        return x * cosv + pltpu.roll(x, HEAD_D // 2, 1) * sinv

    qr = [rot(units[u][0](h * HEAD_D, HEAD_D)) for u, h in uh]
    kr = [rot(units[u][0](GROUP_W + h * HEAD_D, HEAD_D)) * (HEAD_D ** -0.5) for u, h in uh]
    vb = [_bf(units[u][0](2 * GROUP_W + h * HEAD_D, HEAD_D)) for u, h in uh]
    S = [S_ref[units[u][3], h] for u, h in uh]
    A = [_nt(_bf(qr[i]), _bf(kr[i])) * dm_ref[h] for i, (u, h) in enumerate(uh)]
    o = [_nn(_bf(A[i]), vb[i]) + _nn(_bf(qr[i] * qd_ref[h]), _bf(S[i])) for i, (u, h) in enumerate(uh)]
    for i, (u, h) in enumerate(uh):
        S_ref[units[u][3], h] = c_dec[h] * S[i] + _tn(_bf(kr[i] * kd_ref[h]), vb[i])
    for i, (u, h) in enumerate(uh):
        oc = o[i] - jnp.mean(o[i], axis=1, keepdims=True)
        on = oc * lax.rsqrt(jnp.mean(oc * oc, axis=1, keepdims=True) + GN_EPS) * gn[:, _hs(h)]
        g = units[u][0](3 * GROUP_W + h * HEAD_D, HEAD_D)
        units[u][2](h * HEAD_D, _bf(_silu(g) * on))


def _ml_units(L, units, st_refs, consts):
    C_ref, n_ref, m_ref, cv_ref = st_refs
    ltri, ones_l, conv_w, conv_b, bias_sm, gn = consts
    uh = _heads(units)
    conv, xs, bc_all = [], [], []
    for get, get_sm, _, sidx in units:
        x = get(0, 2 * GROUP_W)
        prev8 = cv_ref[sidx]
        c = (conv_b + _shift_rows(x, prev8, 3) * conv_w[0:1] + _shift_rows(x, prev8, 2) * conv_w[1:2]
             + _shift_rows(x, prev8, 1) * conv_w[2:3] + x * conv_w[3:4])
        cv_ref[sidx] = x[L - SUBLANES:L]
        conv.append(_silu(c))
        xs.append(get_sm(LANES, LANES) + bias_sm)
        bc_all.append(_dot3r(ltri, _log_sigmoid(xs[-1])))
    causal = _causal(L)
    lane0 = _iota((L, LANES), 1) == 0
    q = [conv[u][:, _hs(h)] for u, h in uh]
    k = [conv[u][:, GROUP_W + h * HEAD_D:GROUP_W + (h + 1) * HEAD_D] * (HEAD_D ** -0.5) for u, h in uh]
    vb = [_bf(units[u][0](2 * GROUP_W + h * HEAD_D, HEAD_D)) for u, h in uh]
    ig = [xs[u][:, SM_MLI - LANES + h:SM_MLI - LANES + h + 1] for u, h in uh]
    bc = [bc_all[u][:, SM_MLF - LANES + h:SM_MLF - LANES + h + 1] for u, h in uh]
    m_prev = [m_ref[units[u][3], h][:, 0:1] for u, h in uh]
    row = [_dot3r(ones_l, jnp.where(lane0, ig[i] - bc[i], 0.0), "nt") for i in range(len(uh))]
    dlog = [jnp.where(causal, bc[i] + row[i], -jnp.inf) for i in range(len(uh))]
    inter = [bc[i] + m_prev[i] for i in range(len(uh))]
    m_t = [jnp.maximum(inter[i], jnp.max(dlog[i], axis=1, keepdims=True)) for i in range(len(uh))]
    dmat = [jnp.exp(dlog[i] - m_t[i]) for i in range(len(uh))]
    s_in = [jnp.exp(inter[i] - m_t[i]) for i in range(len(uh))]
    C = [C_ref[units[u][3], h] for u, h in uh]
    n = [n_ref[units[u][3], h] for u, h in uh]
    qb = [_bf(x) for x in q]
    A = [_nt(qb[i], _bf(k[i])) * dmat[i] for i in range(len(uh))]
    num = [_nn(_bf(A[i]), vb[i]) + s_in[i] * _nt(qb[i], _bf(C[i])) for i in range(len(uh))]
    den = [jnp.sum(A[i], axis=1, keepdims=True) + s_in[i] * jnp.sum(q[i] * n[i], axis=1, keepdims=True)
           for i in range(len(uh))]
    hh = [num[i] / jnp.maximum(jnp.abs(den[i]), jnp.exp(-m_t[i])) for i in range(len(uh))]
    for i, (u, h) in enumerate(uh):
        sidx = units[u][3]
        m_new = m_t[i][L - 1:L, :]
        b_last = bc[i][L - 1:L, :]
        carry = jnp.exp(b_last + m_prev[i] - m_new)
        wk = jnp.exp(b_last - bc[i] + ig[i] - m_new) * k[i]
        C_ref[sidx, h] = carry * C[i] + _tn(vb[i], _bf(wk))
        n_ref[sidx, h] = carry * n[i] + jnp.sum(wk, axis=0, keepdims=True)
        m_ref[sidx, h] = jnp.broadcast_to(m_new, (1, LANES))
    for i, (u, h) in enumerate(uh):
        hc = hh[i] - jnp.mean(hh[i], axis=1, keepdims=True)
        hn = hc * lax.rsqrt(jnp.mean(hc * hc, axis=1, keepdims=True) + GN_EPS) * gn[:, _hs(h)]
        o_pre = units[u][0](3 * GROUP_W + h * HEAD_D, HEAD_D)
        units[u][2](h * HEAD_D, _bf(_sigmoid(o_pre) * hn))


def _gl_units(L, units, S_ref, consts):
    ltri, a2p, ab, gn = consts
    uh = _heads(units)
    mid = L // 2 - 1
    qs, kall, bcs, q_st, k_st, e_last = [], [], [], [], [], []
    spread = jnp.float32(0.0)
    for get, get_sm, _, _ in units:
        la = _log_sigmoid(_nn(_bf(get_sm(LANES, LANES)), a2p) + ab) / GL_TAU
        bc = _dot3r(ltri, la)
        b_last = bc[L - 1:L, :]
        qs.append(get(0, GL_DK * HEADS) * (GL_DK ** -0.5))
        kall.append(get(GL_DK * HEADS, GL_DK * HEADS))
        bcs.append(bc)
        spread = jnp.maximum(spread, jnp.max(jnp.abs(bc - bc[mid:mid + 1, :])))
        q_st.append(_bf(qs[-1] * jnp.exp(bc)))
        k_st.append(_bf(kall[-1] * jnp.exp(b_last - bc)))
        e_last.append(jnp.exp(b_last))
    causal = _causal(L)
    eye = _iota((GL_DK, GL_DK), 0) == _iota((GL_DK, GL_DK), 1)

    def ks(h):
        return slice(h * GL_DK, (h + 1) * GL_DK)

    def scores_factored():
        q_in = [_bf(qs[u] * jnp.exp(bcs[u] - bcs[u][mid:mid + 1, :])) for u in range(len(units))]
        k_in = [_bf(kall[u] * jnp.exp(bcs[u][mid:mid + 1, :] - bcs[u])) for u in range(len(units))]
        return [jnp.where(causal, _nt(q_in[u][:, ks(h)], k_in[u][:, ks(h)]), 0.0) for u, h in uh]

    def scores_direct():
        rows = _iota((L, GL_DK * HEADS), 0)
        cols = _iota((L, L), 1)

        def body(s, acc):
            out = []
            for u in range(len(units)):
                pick = rows == s
                k_s = jnp.sum(jnp.where(pick, kall[u], 0.0), axis=0, keepdims=True)
                b_s = jnp.sum(jnp.where(pick, bcs[u], 0.0), axis=0, keepdims=True)
                t = qs[u] * k_s * jnp.exp(jnp.where(rows >= s, bcs[u] - b_s, -jnp.inf))
                for h in range(HEADS):
                    col = jnp.sum(t[:, ks(h)], axis=1, keepdims=True)
                    out.append(jnp.where(cols == s, col, acc[u * HEADS + h]))
            return out

        return lax.fori_loop(0, L, body, [jnp.zeros((L, L), F32) for _ in uh])

    def rest(scores):
        A = scores()
        vb = [_bf(units[u][0](GROUP_W + h * HEAD_D, HEAD_D)) for u, h in uh]
        S = [S_ref[units[u][3], h] for u, h in uh]
        o = [_nn(_bf(A[i]), vb[i]) + _nn(q_st[u][:, ks(h)], _bf(S[i])) for i, (u, h) in enumerate(uh)]
        for i, (u, h) in enumerate(uh):
            e_col = jnp.sum(jnp.where(eye, e_last[u][:, ks(h)], 0.0), axis=1, keepdims=True)
            S_ref[units[u][3], h] = e_col * S[i] + _tn(k_st[u][:, ks(h)], vb[i])
        for i, (u, h) in enumerate(uh):
            on = o[i] * lax.rsqrt(jnp.mean(o[i] * o[i], axis=1, keepdims=True) + GN_EPS) * gn[:, _hs(h)]
            g = units[u][0](2 * GROUP_W + h * HEAD_D, HEAD_D)
            units[u][2](h * HEAD_D, _bf(_silu(g) * on))

    lax.cond(spread < GL_SAFE_SPREAD, lambda: rest(scores_factored), lambda: rest(scores_direct))


def _chunk_kernel(kind, nslab, nunit, L, n_state, n_const, n_alias, has_sm, *refs):
    pos = 0
    p_refs = refs[pos:pos + nslab]; pos += nslab
    sm_refs = refs[pos:pos + (nslab if has_sm else 0)]; pos += (nslab if has_sm else 0)
    st_in = refs[pos:pos + n_state]; pos += n_state
    c_refs = refs[pos:pos + n_const]; pos += n_const + n_alias
    o_ref = refs[pos]; pos += 1
    st_out = refs[pos:pos + n_state]; pos += n_state
    c = pl.program_id(1)

    @pl.when(c == 0)
    def _():
        for a, b in zip(st_in, st_out):
            b[...] = a[...]

    def unit(slab, seq, sidx):
        r0 = seq * L if isinstance(seq, int) else pl.multiple_of(seq * L, SUBLANES)

        def get(col, width):
            return p_refs[slab][pl.ds(r0, L), col:col + width]

        def get_sm(col, width):
            return sm_refs[slab][pl.ds(r0, L), col:col + width]

        def put(col, val):
            o_ref[slab, pl.ds(r0, L), col:col + val.shape[1]] = val

        return (get, get_sm, put, sidx)

    def run(units):
        if kind == "ret":
            cosv, sinv = c_refs[0][...], c_refs[1][...]
            consts = (cosv, sinv, c_refs[2], c_refs[3], c_refs[4], c_refs[5][...], _RT_CDEC[L])
            _ret_units(L, units, st_out[0], consts)
        elif kind == "ml":
            _ml_units(L, units, st_out, tuple(r[...] for r in c_refs))
        else:
            _gl_units(L, units, st_out[0], tuple(r[...] for r in c_refs))

    if nunit == 1:
        run([unit(s, 0, s) for s in range(nslab)])
    else:
        def body(i, carry):
            run([unit(0, i * SEQ_GROUP + j, i * SEQ_GROUP + j) for j in range(SEQ_GROUP)])
            return carry
        lax.fori_loop(0, nunit // SEQ_GROUP, body, 0)


def _rt_log_gamma():
    return [math.log1p(-(2.0 ** (-5.0 - h))) for h in range(HEADS)]


_RT_CDEC = {L: [math.exp(L * lg) for lg in _rt_log_gamma()] for L in (8, CHUNK)}


def _chunk_call(kind, P, col_blk, col_w, sm, row0, nseq, T, states, consts, const_chunked, name, stacked=None):
    has_sm = sm
    if T % CHUNK == 0:
        L, nslab, nunit = CHUNK, nseq, 1
        grid = (1, T // L)
        rows = L
        def pmap(s, i, c):
            return ((row0 + s * T) // L + c, col_blk)
        def smap(s, i, c):
            return ((row0 + s * T) // L + c, 3)
        def omap(i, c):
            return (0, c, 0)
        sblk = nseq
    else:
        L, nslab, nunit = T, 1, 16
        grid = (nseq // nunit, 1)
        rows = L * nunit
        def pmap(s, i, c):
            return (row0 // rows + i, col_blk)
        def smap(s, i, c):
            return (row0 // rows + i, 3)
        def omap(i, c):
            return (0, i, 0)
        sblk = nunit
    in_specs = [pl.BlockSpec((rows, col_w), functools.partial(pmap, s)) for s in range(nslab)]
    args = [P] * nslab
    if has_sm:
        in_specs += [pl.BlockSpec((rows, GROUP_W), functools.partial(smap, s)) for s in range(nslab)]
        args += [P] * nslab
    st_specs, st_shapes = [], []
    for st in states:
        arr, layer = st if isinstance(st, tuple) else (st, None)
        tail = arr.shape[1:] if layer is None else arr.shape[2:]
        nd = len(tail)
        st_specs.append(pl.BlockSpec((sblk,) + tail, lambda i, c, nd=nd: (i,) + (0,) * nd))
        st_shapes.append(jax.ShapeDtypeStruct((nseq,) + tail, F32))
        if layer is None:
            in_specs.append(st_specs[-1])
        else:
            in_specs.append(pl.BlockSpec((None, sblk) + tail, lambda i, c, nd=nd, layer=layer: (layer, i) + (0,) * nd))
        args.append(arr)
    for cst, chunked in zip(consts, const_chunked):
        if chunked:
            in_specs.append(pl.BlockSpec((None,) + cst.shape[1:], lambda i, c: (c, 0, 0)))
        else:
            in_specs.append(pl.BlockSpec(cst.shape, lambda i, c, nd=cst.ndim: (0,) * nd))
        args.append(cst)
    aliases = {}
    if stacked is not None:
        depth, layer, prev = stacked
        tail = st_shapes[0].shape[1:]
        st_specs[0] = pl.BlockSpec((None, sblk) + tail, lambda i, c, nd=len(tail): (layer, i) + (0,) * nd)
        st_shapes[0] = jax.ShapeDtypeStruct((depth, nseq) + tail, F32)
        if prev is not None:
            aliases[len(args)] = 1
            in_specs.append(pl.BlockSpec(memory_space=pl.ANY))
            args.append(prev)
    out_specs = [pl.BlockSpec((nslab, rows, GROUP_W), omap)] + st_specs
    out_shape = [jax.ShapeDtypeStruct((nslab, nseq * T // nslab, GROUP_W), BF16)] + st_shapes
    outs = pl.pallas_call(
        functools.partial(_chunk_kernel, kind, nslab, nunit, L, len(states), len(consts), len(aliases), has_sm),
        grid=grid,
        in_specs=in_specs,
        out_specs=out_specs,
        out_shape=out_shape,
        input_output_aliases=aliases,
        compiler_params=_cparams(2),
        name=name,
    )(*args)
    return outs[0].reshape(nseq * T, GROUP_W), outs[1:]


def _pack_w_in(w_in):
    z = jnp.zeros(w_in.shape[:2] + (GROUP_W - 184,), w_in.dtype)
    parts = [w_in[..., 0:1696], w_in[..., 3744:3752], w_in[..., 6824:6840], z,
             w_in[..., 1696:3744], w_in[..., 3752:5800], w_in[..., 5800:6824], w_in[..., 6840:7352]]
    return jnp.concatenate(parts, axis=-1).astype(BF16)


def _row(v):
    return v.reshape(1, -1).astype(F32)


def _rows_at(mat, r0, rows_total):
    z = jnp.zeros((rows_total, mat.shape[1]), F32).at[r0:r0 + mat.shape[0]].set(mat)
    return z.astype(BF16)


def _ret_consts(pos0, T, L, gn):
    pos = pos0 + jnp.arange(T, dtype=F32)
    inv = 1.0 / (10000.0 ** jnp.linspace(0.0, 1.0, HEAD_D // 2, dtype=F32))
    ang = pos[:, None] * inv[None, :]
    cos, sin = jnp.cos(ang), jnp.sin(ang)
    cos2 = jnp.concatenate([cos, cos], axis=1).reshape(T // L, L, HEAD_D)
    sin2 = jnp.concatenate([-sin, sin], axis=1).reshape(T // L, L, HEAD_D)
    lg = jnp.log1p(-jnp.exp2(-5.0 - jnp.arange(HEADS, dtype=F32)))
    idx = jnp.arange(L, dtype=F32)
    rel = idx[:, None] - idx[None, :]
    dmat = jnp.where(rel >= 0, jnp.exp(jnp.maximum(rel, 0.0) * lg[:, None, None]), 0.0)
    q_dec = jnp.broadcast_to(jnp.exp((idx + 1.0) * lg[:, None])[..., None], (HEADS, L, HEAD_D))
    k_dec = jnp.broadcast_to(jnp.exp((L - 1.0 - idx) * lg[:, None])[..., None], (HEADS, L, HEAD_D))
    return [cos2, sin2, dmat, q_dec, k_dec, _row(gn)], [True, True, False, False, False, False]


def _tri(L):
    return jnp.tril(jnp.ones((L, L), F32)).astype(BF16)


def _moe_plan(ids, tm, n_tiles):
    e = ids.reshape(-1)
    n = e.shape[0]
    onehot = (e[:, None] == jnp.arange(N_EXPERTS, dtype=jnp.int32)[None, :]).astype(jnp.int32)
    rank = jnp.sum((jnp.cumsum(onehot, axis=0) - 1) * onehot, axis=1)
    counts = jnp.sum(onehot, axis=0)
    padded = ((counts + tm - 1) // tm) * tm
    ends = jnp.cumsum(padded)
    starts = ends - padded
    dest = jnp.sum(starts[None, :] * onehot, axis=1) + rank
    row_src = jnp.zeros((n_tiles * tm,), jnp.int32).at[dest].set(jnp.arange(n, dtype=jnp.int32) // 2,
                                                                 unique_indices=True)
    tile_start = jnp.arange(n_tiles, dtype=jnp.int32) * tm
    tile_expert = jnp.minimum(jnp.sum((tile_start[:, None] >= ends[None, :]).astype(jnp.int32), axis=1),
                              N_EXPERTS - 1)
    return row_src, dest.reshape(-1, 2), tile_expert, (ends[-1:] // tm).astype(jnp.int32)


def kernel(x_prompt, x_sample, state_rw_shift, state_rw_wkv, state_ml_conv, state_ml_C, state_ml_n, state_ml_m, state_rt_S, state_gl_S, w_in, rw_mu, rw_w0, rw_w2, rw_a0, rw_a2, rw_g2, rw_kk, rw_ka, rw_rk, rw_ln_w, rw_ln_b, ml_conv_w, ml_conv_b, ml_bi, ml_bf, ml_gn_w, rt_gn_w, gl_a2, gl_ab, gl_gn_w, w_out, ln1_w, ln1_b, ln2_w, ln2_b, ffn_w1, ffn_w3, ffn_w2, moe_router, moe_w1, moe_w3, moe_w2):
    Bp, Tp = x_prompt.shape[:2]
    Bs, Ts = x_sample.shape[:2]
    Mp, Ms = Bp * Tp, Bs * Ts
    M = Mp + Ms
    depth = w_in.shape[0]
    x = [x_prompt.reshape(Mp, D_MODEL), x_sample.reshape(Ms, D_MODEL)]
    xb = [v.astype(BF16) for v in x]
    w_in_p = _pack_w_in(w_in)
    w_out_b = w_out.astype(BF16)
    blockdiag = (jnp.arange(LANES)[:, None] // RW_HD) == (jnp.arange(LANES)[None, :] // RW_HD)
    g64 = blockdiag.astype(BF16)
    g64m = (blockdiag.astype(F32) / RW_HD).astype(BF16)
    pad_sh = 4 * GROUP_W - RW_COLS
    outs = {k: [[], []] for k in ("sh", "conv", "n", "m")}
    big = {"wkv": (RW_H, RW_HD, RW_HD), "C": (HEADS, HEAD_D, HEAD_D), "rt": (HEADS, HEAD_D, HEAD_D),
           "gl": (HEADS, GL_DK, HEAD_D)}
    big = {k: [jnp.zeros((depth, b) + tail, F32) for b in (Bp, Bs)] for k, tail in big.items()}

    for l in range(depth):
        P = [_mm([v], w_in_p, l, 3 * GROUP_W, "mm_in") for v in xb]

        prm = [_row(jnp.pad(rw_mu[l], (0, pad_sh))), _row(rw_w0[l]), _row(rw_a0[l]),
               _rows_at(rw_w2[l], 0, 2 * LANES), _rows_at(rw_a2[l], 32, 2 * LANES), _rows_at(rw_g2[l], 64, 2 * LANES),
               _row(rw_kk[l]), _row(rw_ka[l]), _row(rw_rk[l]), _row(rw_ln_w[l]), _row(rw_ln_b[l]), g64, g64m]
        o_rw_p, sh_p, big["wkv"][0] = _rwkv(P[0], 0, Bp, Tp, jnp.zeros((Bp, SUBLANES, 4 * GROUP_W), F32),
                                             jnp.zeros((Bp, RW_H, RW_HD, RW_HD), F32), prm,
                                             stacked=(depth, l, big["wkv"][0]))
        sh_in = jnp.pad(state_rw_shift[l], ((0, 0), (0, pad_sh)))[:, None, :]
        o_rw_s, sh_s, big["wkv"][1] = _rwkv(P[1], 0, Bs, Ts, sh_in, state_rw_wkv, prm, s0_layer=l,
                                             stacked=(depth, l, big["wkv"][1]))
        outs["sh"][0].append(sh_p[:, SUBLANES - 1, :RW_COLS])
        outs["sh"][1].append(sh_s[:, 0, :RW_COLS])

        bias_sm = jnp.zeros((LANES,), F32).at[SM_MLI - LANES:SM_MLI - LANES + HEADS].set(ml_bi[l])
        bias_sm = bias_sm.at[SM_MLF - LANES:SM_MLF - LANES + HEADS].set(ml_bf[l])

        def ml_consts(L):
            return [_tri(L), jnp.ones((L, LANES), BF16), ml_conv_w[l], _row(ml_conv_b[l]), _row(bias_sm), _row(ml_gn_w[l])]

        def ml_states(C, n, m, conv):
            b = n.shape[0]
            return [C, n[:, :, None, :], jnp.broadcast_to(m[:, :, None, None], (b, HEADS, 1, LANES)),
                    jnp.pad(conv, ((0, 0), (SUBLANES - conv.shape[1], 0), (0, 0)))]

        zp = lambda *s: jnp.zeros((Bp,) + s, F32)
        o_ml_p, st_p = _chunk_call("ml", P[0], 1, 4 * GROUP_W, True, 0, Bp, Tp,
                                   ml_states(zp(HEADS, HEAD_D, HEAD_D), zp(HEADS, HEAD_D), zp(HEADS), zp(3, 2 * GROUP_W)),
                                   ml_consts(CHUNK), [False] * 6, "mlstm_prompt", stacked=(depth, l, big["C"][0]))
        o_ml_s, st_s = _chunk_call("ml", P[1], 1, 4 * GROUP_W, True, 0, Bs, Ts,
                                   ml_states((state_ml_C, l), state_ml_n[l], state_ml_m[l], state_ml_conv[l]),
                                   ml_consts(Ts), [False] * 6, "mlstm_sample", stacked=(depth, l, big["C"][1]))
        for g, st in enumerate((st_p, st_s)):
            big["C"][g] = st[0]
            outs["n"][g].append(st[1][:, :, 0, :])
            outs["m"][g].append(st[2][:, :, 0, 0])
            outs["conv"][g].append(st[3][:, SUBLANES - 3:, :])

        c_p, ch_p = _ret_consts(0.0, Tp, CHUNK, rt_gn_w[l])
        c_s, ch_s = _ret_consts(float(PAST_LEN), Ts, Ts, rt_gn_w[l])
        o_rt_p, (big["rt"][0],) = _chunk_call("ret", P[0], 2, 4 * GROUP_W, False, 0, Bp, Tp, [zp(HEADS, HEAD_D, HEAD_D)],
                                              c_p, ch_p, "ret_prompt", stacked=(depth, l, big["rt"][0]))
        o_rt_s, (big["rt"][1],) = _chunk_call("ret", P[1], 2, 4 * GROUP_W, False, 0, Bs, Ts, [(state_rt_S, l)],
                                              c_s, ch_s, "ret_sample", stacked=(depth, l, big["rt"][1]))

        def gl_consts(L):
            return [_tri(L), _rows_at(gl_a2[l], SM_GLA - LANES, LANES), _row(gl_ab[l]), _row(gl_gn_w[l])]

        o_gl_p, (big["gl"][0],) = _chunk_call("gl", P[0], 4, 3 * GROUP_W, True, 0, Bp, Tp, [zp(HEADS, GL_DK, HEAD_D)],
                                              gl_consts(CHUNK), [False] * 4, "gla_prompt",
                                              stacked=(depth, l, big["gl"][0]))
        o_gl_s, (big["gl"][1],) = _chunk_call("gl", P[1], 4, 3 * GROUP_W, True, 0, Bs, Ts, [(state_gl_S, l)],
                                              gl_consts(Ts), [False] * 4, "gla_sample",
                                              stacked=(depth, l, big["gl"][1]))

        mixed = ((o_rw_p, o_ml_p, o_rt_p, o_gl_p), (o_rw_s, o_ml_s, o_rt_s, o_gl_s))
        for g in range(2):
            x[g], xb[g] = _mm_ln(list(mixed[g]), w_out_b, l, x[g], ln1_w[l], ln1_b[l])

        if l % 2 == 0:
            w1, w3, w2 = (w[l // 2][None].astype(BF16) for w in (ffn_w1, ffn_w3, ffn_w2))
            for g in range(2):
                rows = x[g].shape[0]
                tm = _pick(rows, (512, 256, 128))
                y = _ffn(xb[g], jnp.zeros((rows // tm,), jnp.int32), jnp.full((1,), rows // tm, jnp.int32), w1, w3, w2, tm)
                x[g], xb[g] = _add_ln(x[g], [y], None, ln2_w[l], ln2_b[l])
        else:
            tm = _pick(M, (512, 256, 128))
            n_tiles = 2 * M // tm + N_EXPERTS
            router_pad = jnp.pad(moe_router[l // 2], ((0, 0), (0, LANES - N_EXPERTS)))
            routed = [_router(v, router_pad) for v in x]
            gates = [r[0] for r in routed]
            ids = jnp.concatenate([r[1][:, :2] for r in routed], axis=0)
            row_src, dest, tile_expert, n_used = _moe_plan(ids, tm, n_tiles)
            ysorted = _ffn(jnp.concatenate(x, axis=0), tile_expert, n_used, moe_w1[l // 2].astype(BF16),
                           moe_w3[l // 2].astype(BF16), moe_w2[l // 2].astype(BF16), tm, row_src=row_src)
            for g, (r0, r1) in enumerate(((0, Mp), (Mp, M))):
                ys = [jnp.take(ysorted, dest[r0:r1, k], axis=0, mode="clip") for k in range(2)]
                x[g], xb[g] = _add_ln(x[g], ys, [gates[g][:, 0:1], gates[g][:, 1:2]], ln2_w[l], ln2_b[l])

    res = [x[0].reshape(Bp, Tp, D_MODEL), x[1].reshape(Bs, Ts, D_MODEL)]
    for key in ("sh", "wkv", "conv", "C", "n", "m", "rt", "gl"):
        res += list(big[key]) if key in big else [jnp.stack(outs[key][0]), jnp.stack(outs[key][1])]
    return tuple(res)
```

```python
import functools
import math

import jax
import jax.numpy as jnp
from jax import lax
from jax.experimental import pallas as pl
from jax.experimental.pallas import tpu as pltpu

F32 = jnp.float32
BF16 = jnp.bfloat16

D_MODEL = 2048
GROUP_W = 512
RW_COLS = 1696
RW_H, RW_HD = 8, 64
HEADS, HEAD_D = 4, 128
GL_DK = 64
CHUNK = 64
ALPHA = 4.0 ** 0.25
LN_EPS = 1e-5
GN_EPS = 1e-5
RW_GN_EPS = 64e-5
GL_TAU = 16.0
PAST_LEN = 16384
N_EXPERTS = 8

LANES = 128
SUBLANES = 8
VMEM_LIMIT = 48 * 1024 * 1024

N_PACKED = 15 * GROUP_W
SM_MLI, SM_MLF, SM_GLA = 160, 164, 168


def _nn(a, b):
    return jnp.dot(a, b, preferred_element_type=F32)


def _nt(a, b):
    return lax.dot_general(a, b, (((1,), (1,)), ((), ())), preferred_element_type=F32)


def _tn(a, b):
    return lax.dot_general(a, b, (((0,), (0,)), ((), ())), preferred_element_type=F32)


def _bf(x):
    return x.astype(BF16)


def _split3(x):
    hi = x.astype(BF16)
    r1 = x - hi.astype(F32)
    mid = r1.astype(BF16)
    lo = (r1 - mid.astype(F32)).astype(BF16)
    return hi, mid, lo


def _dot3(x, m, kind="nn"):
    f = {"nn": _nn, "nt": _nt, "tn": _tn}[kind]
    hi, mid, lo = _split3(x)
    return f(hi, m) + f(mid, m) + f(lo, m)


def _dot3r(m, x, kind="nn"):
    f = {"nn": _nn, "nt": _nt, "tn": _tn}[kind]
    hi, mid, lo = _split3(x)
    return f(m, hi) + f(m, mid) + f(m, lo)


def _sigmoid(x):
    return 1.0 / (1.0 + jnp.exp(-x))


def _softplus(x):
    return jnp.maximum(x, 0.0) + jnp.log1p(jnp.exp(-jnp.abs(x)))


def _log_sigmoid(x):
    return -_softplus(-x)


def _silu(x):
    return x * _sigmoid(x)


def _iota(shape, axis):
    return lax.broadcasted_iota(jnp.int32, shape, axis)


def _shift_rows(x, prev8, j):
    rows = x.shape[0]
    xr = pltpu.roll(x, j, 0)
    pr = pltpu.roll(prev8, j, 0)
    first = jnp.where(_iota((SUBLANES, x.shape[1]), 0) < j, pr, xr[0:SUBLANES])
    if rows == SUBLANES:
        return first
    return jnp.concatenate([first, xr[SUBLANES:]], axis=0)


def _pick(n, cands):
    return next(c for c in cands if n % c == 0)


def _cparams(n_axes):
    return pltpu.CompilerParams(dimension_semantics=("arbitrary",) * n_axes, vmem_limit_bytes=VMEM_LIMIT)


def _mm_kernel(n_in, *refs):
    xs, ws, o_ref = refs[:n_in], refs[n_in:2 * n_in], refs[2 * n_in]
    acc = _nn(xs[0][...], ws[0][...])
    for x_ref, w_ref in zip(xs[1:], ws[1:]):
        acc = acc + _nn(x_ref[...], w_ref[...])
    o_ref[...] = acc


def _mm(xs, w, layer, tn, name):
    M, N = xs[0].shape[0], w.shape[2]
    n_in = len(xs)
    tm = _pick(M, (1024, 512, 256, 128))
    in_specs = ([pl.BlockSpec((tm, x.shape[1]), lambda j, i: (i, 0)) for x in xs]
                + [pl.BlockSpec((None, x.shape[1], tn), functools.partial(lambda g, j, i: (layer, g, j), g))
                   for g, x in enumerate(xs)])
    return pl.pallas_call(
        functools.partial(_mm_kernel, n_in),
        grid=(N // tn, M // tm),
        in_specs=in_specs,
        out_specs=pl.BlockSpec((tm, tn), lambda j, i: (i, j)),
        out_shape=jax.ShapeDtypeStruct((M, N), F32),
        compiler_params=_cparams(2),
        name=name,
    )(*xs, *([w] * n_in))


def _mm_ln_kernel(n_in, *refs):
    xs, ws = refs[:n_in], refs[n_in:2 * n_in]
    r_ref, w_ref, b_ref, o_ref, ob_ref = refs[2 * n_in:]
    acc = _nn(xs[0][...], ws[0][...])
    for x_ref, wt_ref in zip(xs[1:], ws[1:]):
        acc = acc + _nn(x_ref[...], wt_ref[...])
    out = _layer_norm(ALPHA * r_ref[...] + acc, w_ref[...], b_ref[...])
    o_ref[...] = out
    ob_ref[...] = out.astype(BF16)


def _mm_ln(xs, w, layer, resid, ln_w, ln_b):
    M, N = resid.shape
    n_in = len(xs)
    tm = _pick(M, (512, 256, 128))
    row = pl.BlockSpec((tm, N), lambda i: (i, 0))
    in_specs = ([pl.BlockSpec((tm, x.shape[1]), lambda i: (i, 0)) for x in xs]
                + [pl.BlockSpec((None, x.shape[1], N), functools.partial(lambda g, i: (layer, g, 0), g))
                   for g, x in enumerate(xs)]
                + [row] + [pl.BlockSpec((1, N), lambda i: (0, 0))] * 2)
    return pl.pallas_call(
        functools.partial(_mm_ln_kernel, n_in),
        grid=(M // tm,),
        in_specs=in_specs,
        out_specs=[row, row],
        out_shape=[jax.ShapeDtypeStruct((M, N), F32), jax.ShapeDtypeStruct((M, N), BF16)],
        compiler_params=_cparams(1),
        name="mm_out_ln",
    )(*xs, *([w] * n_in), resid, ln_w.reshape(1, N), ln_b.reshape(1, N))


def _layer_norm(z, w, b):
    mu = jnp.mean(z, axis=-1, keepdims=True)
    zc = z - mu
    var = jnp.mean(zc * zc, axis=-1, keepdims=True)
    return zc * lax.rsqrt(var + LN_EPS) * w + b


def _add_ln_kernel(n_y, gated, x_ref, *refs):
    ys = refs[:n_y]
    gs = refs[n_y:2 * n_y] if gated else ()
    w_ref, b_ref, o_ref, ob_ref = refs[-4:]
    z = ALPHA * x_ref[...]
    for i, y_ref in enumerate(ys):
        y = y_ref[...]
        if gated:
            y = gs[i][...] * y
        z = z + y
    out = _layer_norm(z, w_ref[...], b_ref[...])
    o_ref[...] = out
    ob_ref[...] = out.astype(BF16)


def _add_ln(x, ys, gates, w, b):
    M = x.shape[0]
    tm = _pick(M, (256, 128))
    gated = gates is not None
    row = pl.BlockSpec((tm, D_MODEL), lambda i: (i, 0))
    in_specs = [row] + [row] * len(ys)
    args = [x] + list(ys)
    if gated:
        in_specs += [pl.BlockSpec((tm, 1), lambda i: (i, 0))] * len(ys)
        args += list(gates)
    in_specs += [pl.BlockSpec((1, D_MODEL), lambda i: (0, 0))] * 2
    args += [w.reshape(1, D_MODEL), b.reshape(1, D_MODEL)]
    return pl.pallas_call(
        functools.partial(_add_ln_kernel, len(ys), gated),
        grid=(M // tm,),
        in_specs=in_specs,
        out_specs=[row, row],
        out_shape=[jax.ShapeDtypeStruct((M, D_MODEL), F32), jax.ShapeDtypeStruct((M, D_MODEL), BF16)],
        compiler_params=_cparams(1),
        name="add_ln",
    )(*args)


def _ffn_up_kernel(te_ref, nu_ref, x_ref, w1_ref, w3_ref, h_ref):
    @pl.when(pl.program_id(0) < nu_ref[0])
    def _():
        x = x_ref[...]
        h_ref[...] = (_silu(_nn(x, w1_ref[...])) * _nn(x, w3_ref[...])).astype(BF16)

    @pl.when(pl.program_id(0) >= nu_ref[0])
    def _():
        h_ref[...] = jnp.zeros(h_ref.shape, BF16)


def _ffn_up_gather_kernel(te_ref, nu_ref, src_ref, x_hbm, w1_ref, w3_ref, h_ref, xbuf, xb16, sem):
    i, f = pl.program_id(0), pl.program_id(1)
    tm = xbuf.shape[0]
    n_used = nu_ref[0]

    def start_tile(tile):
        def body(r, c):
            src = x_hbm.at[pl.ds(src_ref[tile * tm + r], 1)]
            pltpu.make_async_copy(src, xbuf.at[pl.ds(r, 1)], sem.at[0]).start()
            return c
        lax.fori_loop(0, tm, body, 0, unroll=8)

    @pl.when((f == 0) & (i == 0) & (n_used > 0))
    def _():
        start_tile(0)

    @pl.when((f == 0) & (i < n_used))
    def _():
        pltpu.make_async_copy(x_hbm.at[pl.ds(0, tm)], xbuf, sem.at[0]).wait()
        xb16[...] = xbuf[...].astype(BF16)

        @pl.when(i + 1 < n_used)
        def _():
            start_tile(i + 1)

    @pl.when(i < n_used)
    def _():
        x = xb16[...]
        h_ref[...] = (_silu(_nn(x, w1_ref[...])) * _nn(x, w3_ref[...])).astype(BF16)

    @pl.when(i >= n_used)
    def _():
        h_ref[...] = jnp.zeros(h_ref.shape, BF16)


def _ffn_down_kernel(te_ref, nu_ref, h_ref, w2_ref, o_ref):
    @pl.when(pl.program_id(0) < nu_ref[0])
    def _():
        o_ref[...] = _nn(h_ref[...], w2_ref[...])

    @pl.when(pl.program_id(0) >= nu_ref[0])
    def _():
        o_ref[...] = jnp.zeros(o_ref.shape, F32)


FFN_TF = 1408
FFN_TN = 512
FFN_DOWN_TILE_ELEMS = 2816 * 2048


def _ffn(xs, tile_expert, n_used, w1, w3, w2, tm, row_src=None):
    R = xs.shape[0] if row_src is None else row_src.shape[0]
    F = w1.shape[2]
    tn = FFN_TN * (FFN_DOWN_TILE_ELEMS // (F * FFN_TN))
    nf, nn = F // FFN_TF, D_MODEL // tn

    def last(i, j, nu, n):
        return jnp.where(i < nu[0], j, n - 1)

    if row_src is None:
        up_spec = pltpu.PrefetchScalarGridSpec(
            num_scalar_prefetch=2,
            grid=(R // tm, nf),
            in_specs=[
                pl.BlockSpec((tm, D_MODEL), lambda i, f, te, nu: (i, 0)),
                pl.BlockSpec((None, D_MODEL, FFN_TF), lambda i, f, te, nu: (te[i], 0, last(i, f, nu, nf))),
                pl.BlockSpec((None, D_MODEL, FFN_TF), lambda i, f, te, nu: (te[i], 0, last(i, f, nu, nf))),
            ],
            out_specs=pl.BlockSpec((tm, FFN_TF), lambda i, f, te, nu: (i, f)),
        )
        h = pl.pallas_call(
            _ffn_up_kernel,
            grid_spec=up_spec,
            out_shape=jax.ShapeDtypeStruct((R, F), BF16),
            compiler_params=_cparams(2),
            name="ffn_up",
        )(tile_expert, n_used, xs, w1, w3)
    else:
        up_spec = pltpu.PrefetchScalarGridSpec(
            num_scalar_prefetch=3,
            grid=(R // tm, nf),
            in_specs=[
                pl.BlockSpec(memory_space=pl.ANY),
                pl.BlockSpec((None, D_MODEL, FFN_TF), lambda i, f, te, nu, src: (te[i], 0, last(i, f, nu, nf))),
                pl.BlockSpec((None, D_MODEL, FFN_TF), lambda i, f, te, nu, src: (te[i], 0, last(i, f, nu, nf))),
            ],
            out_specs=pl.BlockSpec((tm, FFN_TF), lambda i, f, te, nu, src: (i, f)),
            scratch_shapes=[pltpu.VMEM((tm, D_MODEL), F32), pltpu.VMEM((tm, D_MODEL), BF16),
                            pltpu.SemaphoreType.DMA((1,))],
        )
        h = pl.pallas_call(
            _ffn_up_gather_kernel,
            grid_spec=up_spec,
            out_shape=jax.ShapeDtypeStruct((R, F), BF16),
            compiler_params=_cparams(2),
            name="ffn_up_gather",
        )(tile_expert, n_used, row_src, xs, w1, w3)
    down_spec = pltpu.PrefetchScalarGridSpec(
        num_scalar_prefetch=2,
        grid=(R // tm, nn),
        in_specs=[
            pl.BlockSpec((tm, F), lambda i, n, te, nu: (i, 0)),
            pl.BlockSpec((None, F, tn), lambda i, n, te, nu: (te[i], 0, last(i, n, nu, nn))),
        ],
        out_specs=pl.BlockSpec((tm, tn), lambda i, n, te, nu: (i, n)),
    )
    return pl.pallas_call(
        _ffn_down_kernel,
        grid_spec=down_spec,
        out_shape=jax.ShapeDtypeStruct((R, D_MODEL), F32),
        compiler_params=_cparams(2),
        name="ffn_down",
    )(tile_expert, n_used, h, w2)


def _router_kernel(x_ref, r_ref, g_ref, i_ref):
    xh, xm, xl = _split3(x_ref[...])
    rh, rm, rl = _split3(r_ref[...])
    logits = (_nn(xh, rh) + _nn(xh, rm) + _nn(xm, rh)) + (_nn(xh, rl) + _nn(xl, rh) + _nn(xm, rm))
    shape = logits.shape
    lane = _iota(shape, 1).astype(F32)
    neg = jnp.float32(-jnp.inf)
    l1 = jnp.where(lane < N_EXPERTS, logits, neg)
    m1 = jnp.max(l1, axis=1, keepdims=True)
    i1 = jnp.min(jnp.where(l1 == m1, lane, float(LANES)), axis=1, keepdims=True)
    l2 = jnp.where(lane == i1, neg, l1)
    m2 = jnp.max(l2, axis=1, keepdims=True)
    i2 = jnp.min(jnp.where(l2 == m2, lane, float(LANES)), axis=1, keepdims=True)
    e = jnp.exp(m2 - m1)
    g1 = 1.0 / (1.0 + e)
    g2 = e / (1.0 + e)
    g_ref[...] = jnp.where(lane == 0.0, g1, jnp.where(lane == 1.0, g2, 0.0))
    i_ref[...] = jnp.where(lane == 0.0, i1, jnp.where(lane == 1.0, i2, 0.0)).astype(jnp.int32)


def _router(x, router_pad):
    M = x.shape[0]
    tm = _pick(M, (512, 256, 128))
    return pl.pallas_call(
        _router_kernel,
        grid=(M // tm,),
        in_specs=[pl.BlockSpec((tm, D_MODEL), lambda i: (i, 0)),
                  pl.BlockSpec((D_MODEL, LANES), lambda i: (0, 0))],
        out_specs=[pl.BlockSpec((tm, LANES), lambda i: (i, 0))] * 2,
        out_shape=[jax.ShapeDtypeStruct((M, LANES), F32), jax.ShapeDtypeStruct((M, LANES), jnp.int32)],
        compiler_params=_cparams(1),
        name="router",
    )(x, router_pad)


def _rw_prep(pm, prm):
    (w0, a0, w2p, a2p, g2p, k_k, k_a, _, _, _, g64, _) = prm
    r = pm[:, 0:512]
    k = pm[:, 512:1024]
    v = pm[:, 1024:1536]
    sm = pm[:, 1536:1792]
    logw = -_softplus(-(w0 + _nn(_bf(jnp.tanh(sm)), w2p))) - 0.5
    lw = -jnp.exp(logw)
    a = _sigmoid(a0 + _nn(_bf(sm), a2p))
    g = _nn(_bf(_sigmoid(sm)), g2p)
    kk = k * k_k
    ss = jnp.concatenate([_dot3((kk * kk)[:, LANES * p:LANES * (p + 1)], g64) for p in range(4)], axis=1)
    kk = kk / jnp.maximum(jnp.sqrt(ss), 1e-12)
    kmod = k * (1.0 + (a - 1.0) * k_a)
    return r, lw, kmod, kk, kk * a, v, g


def _rw_post(y, r, kmod, v, g, prm):
    (_, _, _, _, _, _, _, r_k, ln_w, ln_b, g64, g64m) = prm

    def per_head(x, m):
        return jnp.concatenate([_dot3(x[:, LANES * p:LANES * (p + 1)], m) for p in range(4)], axis=1)

    yc = y - per_head(y, g64m)
    var = per_head(yc * yc, g64m)
    yn = yc * lax.rsqrt(var + RW_GN_EPS) * ln_w + ln_b
    bonus = per_head(r * kmod * r_k, g64) * v
    return (yn + bonus) * g


def _rw_chunks(L, seqs, scr, s_ref):
    q_s, rr_s, kt_s, bt_s, kp_s, bp_s, v_s, wl_s, y_s = scr
    units = [(si, r0, h) for (si, r0) in seqs for h in range(RW_H)]

    def fetch(ref, r0, h):
        return ref[pl.ds(r0, L), RW_HD * h:RW_HD * (h + 1)]

    row = _iota((2 * L, 2 * L), 0)
    col = _iota((2 * L, 2 * L), 1)
    colm = jnp.where(col >= L, col - L, col)
    mask4 = colm <= jnp.where(row < L, row - 1, row - L)
    qf = [fetch(q_s, r0, h) for (_, r0, h) in units]
    rf = [fetch(rr_s, r0, h) for (_, r0, h) in units]
    vb = [_bf(fetch(v_s, r0, h)) for (_, r0, h) in units]
    g4 = [jnp.where(mask4, _nt(_bf(jnp.concatenate([qf[i], rf[i]], axis=0)),
                               _bf(jnp.concatenate([fetch(kt_s, r0, h), fetch(bt_s, r0, h)], axis=0))), 0.0)
          for i, (_, r0, h) in enumerate(units)]
    av = [_nn(_bf(g4[i][:, :L]), vb[i]) for i in range(len(units))]
    x = [jnp.concatenate([qf[i], av[i][:L]], axis=1) for i in range(len(units))]
    pw = [g4[i][:L, L:] for i in range(len(units))]
    x = [x[i] - _nn(_bf(pw[i]), _bf(x[i])) for i in range(len(units))]
    for _ in range(int(math.log2(L)) - 1):
        pw = [_nn(_bf(m), _bf(m)) for m in pw]
        x = [x[i] + _nn(_bf(pw[i]), _bf(x[i])) for i in range(len(units))]
    ry = [jnp.concatenate([rf[i], av[i][L:]], axis=1) - _nn(_bf(g4[i][L:, L:]), _bf(x[i])) for i in range(len(units))]
    gh = [_tn(_bf(x[i]), _bf(fetch(bp_s, r0, h))) for i, (_, r0, h) in enumerate(units)]
    hk = [_tn(vb[i], _bf(fetch(kp_s, r0, h))) for i, (_, r0, h) in enumerate(units)]
    ys = []
    for i, (si, r0, h) in enumerate(units):
        S = s_ref[si, h]
        Sb = _bf(S)
        ys.append(_nt(_bf(ry[i][:, :RW_HD]), Sb) + ry[i][:, RW_HD:])
        w_last = fetch(wl_s, r0, h)[0:1, :]
        s_ref[si, h] = S * w_last - _nn(Sb, _bf(gh[i][:RW_HD])) + hk[i] - gh[i][RW_HD:]
    for i in range(0, len(units), 2):
        _, r0, h = units[i]
        y_s[pl.ds(r0, L), RW_HD * h:RW_HD * (h + 2)] = jnp.concatenate([ys[i], ys[i + 1]], axis=1)


N_RW_PRM = 13


def _rwkv_kernel(is_prompt, nsq, L, grp, n_alias, own_layer, *refs):
    n_p = nsq if is_prompt else 1
    p_refs = refs[:n_p]
    sh0_ref, s0_ref = refs[n_p:n_p + 2]
    pos = n_p + 2
    prm_refs = refs[pos:pos + N_RW_PRM]
    pos += N_RW_PRM
    tri_ref = refs[pos]
    pos += 1 + n_alias
    o_ref, sh_out_ref, s_out_ref = refs[pos:pos + 3]
    scr = refs[pos + 3:]
    g_s, r_s, k_s = scr[9:12]
    mu = prm_refs[0][...]
    prm = tuple(r[...] for r in prm_refs[1:])

    stacked_ref = s_out_ref if own_layer is not None else None
    if stacked_ref is not None:
        s_out_ref = stacked_ref.at[own_layer]

    def zero_other_layers():
        if stacked_ref is not None:
            for d in range(stacked_ref.shape[0]):
                if d != own_layer:
                    stacked_ref[d] = jnp.zeros(stacked_ref.shape[1:], F32)

    if is_prompt:
        prev_scr = scr[12]

        @pl.when(pl.program_id(0) == 0)
        def _():
            s_out_ref[...] = s0_ref[...]
            prev_scr[...] = sh0_ref[...]
            zero_other_layers()

        ps, pps = [], []
        for s in range(nsq):
            ps.append(p_refs[s][...])
            pps.append(_shift_rows(ps[s], prev_scr[s], 1))
            prev_scr[s] = ps[s][L - SUBLANES:L]
        p = jnp.concatenate(ps, axis=0)
        pp = jnp.concatenate(pps, axis=0)
        sh_out_ref[...] = prev_scr[...]
    else:
        p = p_refs[0][...]
        p3 = p.reshape(nsq, L, p.shape[1])
        pp = jnp.where(_iota(p3.shape, 1) == 0, sh0_ref[...], pltpu.roll(p3, 1, 1)).reshape(p.shape)
        sh_out_ref[...] = p3[:, L - 1:L, :]
        s_out_ref[...] = s0_ref[...]
        zero_other_layers()

    r, lw, kmod, kk, bb, v, g = _rw_prep(p + (pp - p) * mu, prm)
    cum = _dot3r(tri_ref[...], lw)
    cum3 = cum.reshape(nsq, L, cum.shape[1])
    tot = jnp.broadcast_to(cum3[:, L - 1:L, :], cum3.shape).reshape(cum.shape)
    e_neg = jnp.exp(-cum)
    e_rem = jnp.exp(tot - cum)
    q_s, rr_s, kt_s, bt_s, kp_s, bp_s, v_s, wl_s, y_s = scr[:9]
    q_s[...] = kk * jnp.exp(cum - lw)
    rr_s[...] = r * jnp.exp(cum)
    kt_s[...] = kmod * e_neg
    bt_s[...] = bb * e_neg
    kp_s[...] = kmod * e_rem
    bp_s[...] = bb * e_rem
    v_s[...] = v
    wl_s[...] = jnp.exp(tot)
    g_s[...], r_s[...], k_s[...] = g, r, kmod

    if is_prompt:
        _rw_chunks(L, [(s, s * L) for s in range(nsq)], scr[:9], s_out_ref)
    else:
        def body(i, c):
            _rw_chunks(L, [(i * grp + j, pl.multiple_of((i * grp + j) * L, SUBLANES)) for j in range(grp)],
                       scr[:9], s_out_ref)
            return c
        lax.fori_loop(0, nsq // grp, body, 0)

    out = _rw_post(y_s[...], r_s[...], k_s[...], v_s[...], g_s[...], prm).astype(BF16)
    if is_prompt:
        for s in range(nsq):
            o_ref[s] = out[s * L:(s + 1) * L]
    else:
        o_ref[...] = out


def _rwkv(P, row0, B, T, sh0, s0, prm_list, s0_layer=None, stacked=None):
    is_prompt = T % CHUNK == 0
    wide = 4 * GROUP_W
    s_shape = (B, RW_H, RW_HD, RW_HD)
    if is_prompt:
        L, nsq, grid = CHUNK, B, (T // CHUNK,)
        rows = nsq * L
        in_specs = [pl.BlockSpec((L, wide), functools.partial(lambda s, c: ((row0 + s * T) // L + c, 0), s))
                    for s in range(nsq)]
        in_specs += [pl.BlockSpec(sh0.shape, lambda c: (0, 0, 0)), pl.BlockSpec(s0.shape, lambda c: (0, 0, 0, 0))]
        out_specs = [pl.BlockSpec((nsq, L, GROUP_W), lambda c: (0, c, 0)),
                     pl.BlockSpec(sh0.shape, lambda c: (0, 0, 0)), pl.BlockSpec(s0.shape, lambda c: (0, 0, 0, 0))]
        out_shape = [jax.ShapeDtypeStruct((nsq, T, GROUP_W), BF16)]
        args = [P] * nsq
    else:
        L, nsq, grid = T, 16, (B // 16,)
        rows = nsq * L
        s_spec = (pl.BlockSpec((nsq, RW_H, RW_HD, RW_HD), lambda i: (i, 0, 0, 0)) if s0_layer is None else
                  pl.BlockSpec((None, nsq, RW_H, RW_HD, RW_HD), lambda i: (s0_layer, i, 0, 0, 0)))
        in_specs = [pl.BlockSpec((rows, wide), lambda i: (row0 // rows + i, 0)),
                    pl.BlockSpec((nsq, 1, wide), lambda i: (i, 0, 0)), s_spec]
        out_specs = [pl.BlockSpec((rows, GROUP_W), lambda i: (i, 0)),
                     pl.BlockSpec((nsq, 1, wide), lambda i: (i, 0, 0)),
                     pl.BlockSpec((nsq, RW_H, RW_HD, RW_HD), lambda i: (i, 0, 0, 0))]
        out_shape = [jax.ShapeDtypeStruct((B * T, GROUP_W), BF16)]
        args = [P]
    seq_of_row = jnp.arange(rows) // L
    same = seq_of_row[:, None] == seq_of_row[None, :]
    tri = (same & (jnp.arange(rows)[:, None] >= jnp.arange(rows)[None, :])).astype(BF16)
    consts = list(prm_list) + [tri]
    in_specs += [pl.BlockSpec(x.shape, lambda *a: (0, 0)) for x in consts]
    out_shape += [jax.ShapeDtypeStruct(sh0.shape, F32), jax.ShapeDtypeStruct(s_shape, F32)]
    args += [sh0, s0] + consts
    aliases = {}
    own_layer = None
    if stacked is not None:
        depth, layer, prev = stacked
        blk = tuple(out_specs[2].block_shape)
        out_shape[2] = jax.ShapeDtypeStruct((depth,) + s_shape, F32)
        if prev is None:
            own_layer = layer
            out_specs[2] = pl.BlockSpec((depth,) + blk, (lambda c: (0, 0, 0, 0, 0)) if is_prompt
                                        else (lambda i: (0, i, 0, 0, 0)))
        else:
            out_specs[2] = pl.BlockSpec((None,) + blk, (lambda c: (layer, 0, 0, 0, 0)) if is_prompt
                                        else (lambda i: (layer, i, 0, 0, 0)))
            aliases[len(args)] = 2
            in_specs.append(pl.BlockSpec(memory_space=pl.ANY))
            args.append(prev)
    scratch = [pltpu.VMEM((rows, GROUP_W), F32)] * 12
    if is_prompt:
        scratch.append(pltpu.VMEM(sh0.shape, F32))
    outs = pl.pallas_call(
        functools.partial(_rwkv_kernel, is_prompt, nsq, L, RW_SEQ_GROUP, len(aliases), own_layer),
        grid=grid,
        in_specs=in_specs,
        out_specs=out_specs,
        out_shape=out_shape,
        input_output_aliases=aliases,
        scratch_shapes=scratch,
        compiler_params=_cparams(1),
        name="rwkv_prompt" if is_prompt else "rwkv_sample",
    )(*args)
    return outs[0].reshape(B * T, GROUP_W), outs[1], outs[2]


SEQ_GROUP = 4
RW_SEQ_GROUP = 8
GL_SAFE_SPREAD = 40.0


def _causal(L):
    return _iota((L, L), 0) >= _iota((L, L), 1)


def _heads(units):
    return [(u, h) for u in range(len(units)) for h in range(HEADS)]


def _hs(h):
    return slice(h * HEAD_D, (h + 1) * HEAD_D)


def _ret_units(L, units, S_ref, consts):
    cosv, sinv, dm_ref, qd_ref, kd_ref, gn, c_dec = consts
    uh = _heads(units)

    def rot(x):
        return x * cosv + pltpu.roll(x, HEAD_D // 2, 1) * sinv

    qr = [rot(units[u][0](h * HEAD_D, HEAD_D)) for u, h in uh]
    kr = [rot(units[u][0](GROUP_W + h * HEAD_D, HEAD_D)) * (HEAD_D ** -0.5) for u, h in uh]
    vb = [_bf(units[u][0](2 * GROUP_W + h * HEAD_D, HEAD_D)) for u, h in uh]
    S = [S_ref[units[u][3], h] for u, h in uh]
    A = [_nt(_bf(qr[i]), _bf(kr[i])) * dm_ref[h] for i, (u, h) in enumerate(uh)]
    o = [_nn(_bf(A[i]), vb[i]) + _nn(_bf(qr[i] * qd_ref[h]), _bf(S[i])) for i, (u, h) in enumerate(uh)]
    for i, (u, h) in enumerate(uh):
        S_ref[units[u][3], h] = c_dec[h] * S[i] + _tn(_bf(kr[i] * kd_ref[h]), vb[i])
    for i, (u, h) in enumerate(uh):
        oc = o[i] - jnp.mean(o[i], axis=1, keepdims=True)
        on = oc * lax.rsqrt(jnp.mean(oc * oc, axis=1, keepdims=True) + GN_EPS) * gn[:, _hs(h)]
        g = units[u][0](3 * GROUP_W + h * HEAD_D, HEAD_D)
        units[u][2](h * HEAD_D, _bf(_silu(g) * on))


def _ml_units(L, units, st_refs, consts):
    C_ref, n_ref, m_ref, cv_ref = st_refs
    ltri, ones_l, conv_w, conv_b, bias_sm, gn = consts
    uh = _heads(units)
    conv, xs, bc_all = [], [], []
    for get, get_sm, _, sidx in units:
        x = get(0, 2 * GROUP_W)
        prev8 = cv_ref[sidx]
        c = (conv_b + _shift_rows(x, prev8, 3) * conv_w[0:1] + _shift_rows(x, prev8, 2) * conv_w[1:2]
             + _shift_rows(x, prev8, 1) * conv_w[2:3] + x * conv_w[3:4])
        cv_ref[sidx] = x[L - SUBLANES:L]
        conv.append(_silu(c))
        xs.append(get_sm(LANES, LANES) + bias_sm)
        bc_all.append(_dot3r(ltri, _log_sigmoid(xs[-1])))
    causal = _causal(L)
    lane0 = _iota((L, LANES), 1) == 0
    q = [conv[u][:, _hs(h)] for u, h in uh]
    k = [conv[u][:, GROUP_W + h * HEAD_D:GROUP_W + (h + 1) * HEAD_D] * (HEAD_D ** -0.5) for u, h in uh]
    vb = [_bf(units[u][0](2 * GROUP_W + h * HEAD_D, HEAD_D)) for u, h in uh]
    ig = [xs[u][:, SM_MLI - LANES + h:SM_MLI - LANES + h + 1] for u, h in uh]
    bc = [bc_all[u][:, SM_MLF - LANES + h:SM_MLF - LANES + h + 1] for u, h in uh]
    m_prev = [m_ref[units[u][3], h][:, 0:1] for u, h in uh]
    row = [_dot3r(ones_l, jnp.where(lane0, ig[i] - bc[i], 0.0), "nt") for i in range(len(uh))]
    dlog = [jnp.where(causal, bc[i] + row[i], -jnp.inf) for i in range(len(uh))]
    inter = [bc[i] + m_prev[i] for i in range(len(uh))]
    m_t = [jnp.maximum(inter[i], jnp.max(dlog[i], axis=1, keepdims=True)) for i in range(len(uh))]
    dmat = [jnp.exp(dlog[i] - m_t[i]) for i in range(len(uh))]
    s_in = [jnp.exp(inter[i] - m_t[i]) for i in range(len(uh))]
    C = [C_ref[units[u][3], h] for u, h in uh]
    n = [n_ref[units[u][3], h] for u, h in uh]
    qb = [_bf(x) for x in q]
    A = [_nt(qb[i], _bf(k[i])) * dmat[i] for i in range(len(uh))]
    num = [_nn(_bf(A[i]), vb[i]) + s_in[i] * _nt(qb[i], _bf(C[i])) for i in range(len(uh))]
    den = [jnp.sum(A[i], axis=1, keepdims=True) + s_in[i] * jnp.sum(q[i] * n[i], axis=1, keepdims=True)
           for i in range(len(uh))]
    hh = [num[i] / jnp.maximum(jnp.abs(den[i]), jnp.exp(-m_t[i])) for i in range(len(uh))]
    for i, (u, h) in enumerate(uh):
        sidx = units[u][3]
        m_new = m_t[i][L - 1:L, :]
        b_last = bc[i][L - 1:L, :]
        carry = jnp.exp(b_last + m_prev[i] - m_new)
        wk = jnp.exp(b_last - bc[i] + ig[i] - m_new) * k[i]
        C_ref[sidx, h] = carry * C[i] + _tn(vb[i], _bf(wk))
        n_ref[sidx, h] = carry * n[i] + jnp.sum(wk, axis=0, keepdims=True)
        m_ref[sidx, h] = jnp.broadcast_to(m_new, (1, LANES))
    for i, (u, h) in enumerate(uh):
        hc = hh[i] - jnp.mean(hh[i], axis=1, keepdims=True)
        hn = hc * lax.rsqrt(jnp.mean(hc * hc, axis=1, keepdims=True) + GN_EPS) * gn[:, _hs(h)]
        o_pre = units[u][0](3 * GROUP_W + h * HEAD_D, HEAD_D)
        units[u][2](h * HEAD_D, _bf(_sigmoid(o_pre) * hn))


def _gl_units(L, units, S_ref, consts):
    ltri, a2p, ab, gn = consts
    uh = _heads(units)
    mid = L // 2 - 1
    qs, kall, bcs, q_st, k_st, e_last = [], [], [], [], [], []
    spread = jnp.float32(0.0)
    for get, get_sm, _, _ in units:
        la = _log_sigmoid(_nn(_bf(get_sm(LANES, LANES)), a2p) + ab) / GL_TAU
        bc = _dot3r(ltri, la)
        b_last = bc[L - 1:L, :]
        qs.append(get(0, GL_DK * HEADS) * (GL_DK ** -0.5))
        kall.append(get(GL_DK * HEADS, GL_DK * HEADS))
        bcs.append(bc)
        spread = jnp.maximum(spread, jnp.max(jnp.abs(bc - bc[mid:mid + 1, :])))
        q_st.append(_bf(qs[-1] * jnp.exp(bc)))
        k_st.append(_bf(kall[-1] * jnp.exp(b_last - bc)))
        e_last.append(jnp.exp(b_last))
    causal = _causal(L)
    eye = _iota((GL_DK, GL_DK), 0) == _iota((GL_DK, GL_DK), 1)

    def ks(h):
        return slice(h * GL_DK, (h + 1) * GL_DK)

    def scores_factored():
        q_in = [_bf(qs[u] * jnp.exp(bcs[u] - bcs[u][mid:mid + 1, :])) for u in range(len(units))]
        k_in = [_bf(kall[u] * jnp.exp(bcs[u][mid:mid + 1, :] - bcs[u])) for u in range(len(units))]
        return [jnp.where(causal, _nt(q_in[u][:, ks(h)], k_in[u][:, ks(h)]), 0.0) for u, h in uh]

    def scores_direct():
        rows = _iota((L, GL_DK * HEADS), 0)
        cols = _iota((L, L), 1)

        def body(s, acc):
            out = []
            for u in range(len(units)):
                pick = rows == s
                k_s = jnp.sum(jnp.where(pick, kall[u], 0.0), axis=0, keepdims=True)
                b_s = jnp.sum(jnp.where(pick, bcs[u], 0.0), axis=0, keepdims=True)
                t = qs[u] * k_s * jnp.exp(jnp.where(rows >= s, bcs[u] - b_s, -jnp.inf))
                for h in range(HEADS):
                    col = jnp.sum(t[:, ks(h)], axis=1, keepdims=True)
                    out.append(jnp.where(cols == s, col, acc[u * HEADS + h]))
            return out

        return lax.fori_loop(0, L, body, [jnp.zeros((L, L), F32) for _ in uh])

    def rest(scores):
        A = scores()
        vb = [_bf(units[u][0](GROUP_W + h * HEAD_D, HEAD_D)) for u, h in uh]
        S = [S_ref[units[u][3], h] for u, h in uh]
        o = [_nn(_bf(A[i]), vb[i]) + _nn(q_st[u][:, ks(h)], _bf(S[i])) for i, (u, h) in enumerate(uh)]
        for i, (u, h) in enumerate(uh):
            e_col = jnp.sum(jnp.where(eye, e_last[u][:, ks(h)], 0.0), axis=1, keepdims=True)
            S_ref[units[u][3], h] = e_col * S[i] + _tn(k_st[u][:, ks(h)], vb[i])
        for i, (u, h) in enumerate(uh):
            on = o[i] * lax.rsqrt(jnp.mean(o[i] * o[i], axis=1, keepdims=True) + GN_EPS) * gn[:, _hs(h)]
            g = units[u][0](2 * GROUP_W + h * HEAD_D, HEAD_D)
            units[u][2](h * HEAD_D, _bf(_silu(g) * on))

    lax.cond(spread < GL_SAFE_SPREAD, lambda: rest(scores_factored), lambda: rest(scores_direct))


def _chunk_kernel(kind, nslab, nunit, L, n_state, n_const, n_alias, has_sm, own_layer, *refs):
    pos = 0
    p_refs = refs[pos:pos + nslab]; pos += nslab
    sm_refs = refs[pos:pos + (nslab if has_sm else 0)]; pos += (nslab if has_sm else 0)
    st_in = refs[pos:pos + n_state]; pos += n_state
    c_refs = refs[pos:pos + n_const]; pos += n_const + n_alias
    o_ref = refs[pos]; pos += 1
    st_out = refs[pos:pos + n_state]; pos += n_state
    c = pl.program_id(1)
    if own_layer is not None:
        stacked_ref = st_out[0]
        st_out = (stacked_ref.at[own_layer],) + tuple(st_out[1:])

    @pl.when(c == 0)
    def _():
        for a, b in zip(st_in, st_out):
            b[...] = a[...]
        if own_layer is not None:
            for d in range(stacked_ref.shape[0]):
                if d != own_layer:
                    stacked_ref[d] = jnp.zeros(stacked_ref.shape[1:], F32)

    def unit(slab, seq, sidx):
        r0 = seq * L if isinstance(seq, int) else pl.multiple_of(seq * L, SUBLANES)

        def get(col, width):
            return p_refs[slab][pl.ds(r0, L), col:col + width]

        def get_sm(col, width):
            return sm_refs[slab][pl.ds(r0, L), col:col + width]

        def put(col, val):
            o_ref[slab, pl.ds(r0, L), col:col + val.shape[1]] = val

        return (get, get_sm, put, sidx)

    def run(units):
        if kind == "ret":
            cosv, sinv = c_refs[0][...], c_refs[1][...]
            consts = (cosv, sinv, c_refs[2], c_refs[3], c_refs[4], c_refs[5][...], _RT_CDEC[L])
            _ret_units(L, units, st_out[0], consts)
        elif kind == "ml":
            _ml_units(L, units, st_out, tuple(r[...] for r in c_refs))
        else:
            _gl_units(L, units, st_out[0], tuple(r[...] for r in c_refs))

    if nunit == 1:
        run([unit(s, 0, s) for s in range(nslab)])
    else:
        def body(i, carry):
            run([unit(0, i * SEQ_GROUP + j, i * SEQ_GROUP + j) for j in range(SEQ_GROUP)])
            return carry
        lax.fori_loop(0, nunit // SEQ_GROUP, body, 0)


def _rt_log_gamma():
    return [math.log1p(-(2.0 ** (-5.0 - h))) for h in range(HEADS)]


_RT_CDEC = {L: [math.exp(L * lg) for lg in _rt_log_gamma()] for L in (8, CHUNK)}


def _chunk_call(kind, P, col_blk, col_w, sm, row0, nseq, T, states, consts, const_chunked, name, stacked=None):
    has_sm = sm
    if T % CHUNK == 0:
        L, nslab, nunit = CHUNK, nseq, 1
        grid = (1, T // L)
        rows = L
        def pmap(s, i, c):
            return ((row0 + s * T) // L + c, col_blk)
        def smap(s, i, c):
            return ((row0 + s * T) // L + c, 3)
        def omap(i, c):
            return (0, c, 0)
        sblk = nseq
    else:
        L, nslab, nunit = T, 1, 16
        grid = (nseq // nunit, 1)
        rows = L * nunit
        def pmap(s, i, c):
            return (row0 // rows + i, col_blk)
        def smap(s, i, c):
            return (row0 // rows + i, 3)
        def omap(i, c):
            return (0, i, 0)
        sblk = nunit
    in_specs = [pl.BlockSpec((rows, col_w), functools.partial(pmap, s)) for s in range(nslab)]
    args = [P] * nslab
    if has_sm:
        in_specs += [pl.BlockSpec((rows, GROUP_W), functools.partial(smap, s)) for s in range(nslab)]
        args += [P] * nslab
    st_specs, st_shapes = [], []
    for st in states:
        arr, layer = st if isinstance(st, tuple) else (st, None)
        tail = arr.shape[1:] if layer is None else arr.shape[2:]
        nd = len(tail)
        st_specs.append(pl.BlockSpec((sblk,) + tail, lambda i, c, nd=nd: (i,) + (0,) * nd))
        st_shapes.append(jax.ShapeDtypeStruct((nseq,) + tail, F32))
        if layer is None:
            in_specs.append(st_specs[-1])
        else:
            in_specs.append(pl.BlockSpec((None, sblk) + tail, lambda i, c, nd=nd, layer=layer: (layer, i) + (0,) * nd))
        args.append(arr)
    for cst, chunked in zip(consts, const_chunked):
        if chunked:
            in_specs.append(pl.BlockSpec((None,) + cst.shape[1:], lambda i, c: (c, 0, 0)))
        else:
            in_specs.append(pl.BlockSpec(cst.shape, lambda i, c, nd=cst.ndim: (0,) * nd))
        args.append(cst)
    aliases = {}
    own_layer = None
    if stacked is not None:
        depth, layer, prev = stacked
        tail = st_shapes[0].shape[1:]
        st_shapes[0] = jax.ShapeDtypeStruct((depth, nseq) + tail, F32)
        if prev is None:
            own_layer = layer
            st_specs[0] = pl.BlockSpec((depth, sblk) + tail, lambda i, c, nd=len(tail): (0, i) + (0,) * nd)
        else:
            st_specs[0] = pl.BlockSpec((None, sblk) + tail, lambda i, c, nd=len(tail): (layer, i) + (0,) * nd)
            aliases[len(args)] = 1
            in_specs.append(pl.BlockSpec(memory_space=pl.ANY))
            args.append(prev)
    out_specs = [pl.BlockSpec((nslab, rows, GROUP_W), omap)] + st_specs
    out_shape = [jax.ShapeDtypeStruct((nslab, nseq * T // nslab, GROUP_W), BF16)] + st_shapes
    outs = pl.pallas_call(
        functools.partial(_chunk_kernel, kind, nslab, nunit, L, len(states), len(consts), len(aliases), has_sm,
                          own_layer),
        grid=grid,
        in_specs=in_specs,
        out_specs=out_specs,
        out_shape=out_shape,
        input_output_aliases=aliases,
        compiler_params=_cparams(2),
        name=name,
    )(*args)
    return outs[0].reshape(nseq * T, GROUP_W), outs[1:]


def _pack_w_in(w_in):
    w_in = w_in.astype(BF16)
    z = jnp.zeros(w_in.shape[:2] + (GROUP_W - 184,), BF16)
    parts = [w_in[..., 0:1696], w_in[..., 3744:3752], w_in[..., 6824:6840], z,
             w_in[..., 1696:3744], w_in[..., 3752:5800], w_in[..., 5800:6824], w_in[..., 6840:7352]]
    return jnp.concatenate(parts, axis=-1)


def _row(v):
    return v.reshape(1, -1).astype(F32)


def _rows_at(mat, r0, rows_total):
    z = jnp.zeros((rows_total, mat.shape[1]), F32).at[r0:r0 + mat.shape[0]].set(mat)
    return z.astype(BF16)


def _ret_consts(pos0, T, L, gn):
    pos = pos0 + jnp.arange(T, dtype=F32)
    inv = 1.0 / (10000.0 ** jnp.linspace(0.0, 1.0, HEAD_D // 2, dtype=F32))
    ang = pos[:, None] * inv[None, :]
    cos, sin = jnp.cos(ang), jnp.sin(ang)
    cos2 = jnp.concatenate([cos, cos], axis=1).reshape(T // L, L, HEAD_D)
    sin2 = jnp.concatenate([-sin, sin], axis=1).reshape(T // L, L, HEAD_D)
    lg = jnp.log1p(-jnp.exp2(-5.0 - jnp.arange(HEADS, dtype=F32)))
    idx = jnp.arange(L, dtype=F32)
    rel = idx[:, None] - idx[None, :]
    dmat = jnp.where(rel >= 0, jnp.exp(jnp.maximum(rel, 0.0) * lg[:, None, None]), 0.0)
    q_dec = jnp.broadcast_to(jnp.exp((idx + 1.0) * lg[:, None])[..., None], (HEADS, L, HEAD_D))
    k_dec = jnp.broadcast_to(jnp.exp((L - 1.0 - idx) * lg[:, None])[..., None], (HEADS, L, HEAD_D))
    return [cos2, sin2, dmat, q_dec, k_dec, _row(gn)], [True, True, False, False, False, False]


def _tri(L):
    return jnp.tril(jnp.ones((L, L), F32)).astype(BF16)


def _moe_plan(ids, tm, n_tiles):
    e = ids.reshape(-1)
    n = e.shape[0]
    onehot = (e[:, None] == jnp.arange(N_EXPERTS, dtype=jnp.int32)[None, :]).astype(jnp.int32)
    rank = jnp.sum((jnp.cumsum(onehot, axis=0) - 1) * onehot, axis=1)
    counts = jnp.sum(onehot, axis=0)
    padded = ((counts + tm - 1) // tm) * tm
    ends = jnp.cumsum(padded)
    starts = ends - padded
    dest = jnp.sum(starts[None, :] * onehot, axis=1) + rank
    row_src = jnp.zeros((n_tiles * tm,), jnp.int32).at[dest].set(jnp.arange(n, dtype=jnp.int32) // 2,
                                                                 unique_indices=True)
    tile_start = jnp.arange(n_tiles, dtype=jnp.int32) * tm
    tile_expert = jnp.minimum(jnp.sum((tile_start[:, None] >= ends[None, :]).astype(jnp.int32), axis=1),
                              N_EXPERTS - 1)
    return row_src, dest.reshape(-1, 2), tile_expert, (ends[-1:] // tm).astype(jnp.int32)


def kernel(x_prompt, x_sample, state_rw_shift, state_rw_wkv, state_ml_conv, state_ml_C, state_ml_n, state_ml_m, state_rt_S, state_gl_S, w_in, rw_mu, rw_w0, rw_w2, rw_a0, rw_a2, rw_g2, rw_kk, rw_ka, rw_rk, rw_ln_w, rw_ln_b, ml_conv_w, ml_conv_b, ml_bi, ml_bf, ml_gn_w, rt_gn_w, gl_a2, gl_ab, gl_gn_w, w_out, ln1_w, ln1_b, ln2_w, ln2_b, ffn_w1, ffn_w3, ffn_w2, moe_router, moe_w1, moe_w3, moe_w2):
    Bp, Tp = x_prompt.shape[:2]
    Bs, Ts = x_sample.shape[:2]
    Mp, Ms = Bp * Tp, Bs * Ts
    M = Mp + Ms
    depth = w_in.shape[0]
    x = [x_prompt.reshape(Mp, D_MODEL), x_sample.reshape(Ms, D_MODEL)]
    xb = [v.astype(BF16) for v in x]
    w_in_p = _pack_w_in(w_in)
    w_out_b = w_out.astype(BF16)
    blockdiag = (jnp.arange(LANES)[:, None] // RW_HD) == (jnp.arange(LANES)[None, :] // RW_HD)
    g64 = blockdiag.astype(BF16)
    g64m = (blockdiag.astype(F32) / RW_HD).astype(BF16)
    pad_sh = 4 * GROUP_W - RW_COLS
    outs = {k: [[], []] for k in ("sh", "conv", "n", "m")}
    big = {k: [None, None] for k in ("wkv", "C", "rt", "gl")}

    for l in range(depth):
        P = [_mm([v], w_in_p, l, 3 * GROUP_W, "mm_in") for v in xb]

        prm = [_row(jnp.pad(rw_mu[l], (0, pad_sh))), _row(rw_w0[l]), _row(rw_a0[l]),
               _rows_at(rw_w2[l], 0, 2 * LANES), _rows_at(rw_a2[l], 32, 2 * LANES), _rows_at(rw_g2[l], 64, 2 * LANES),
               _row(rw_kk[l]), _row(rw_ka[l]), _row(rw_rk[l]), _row(rw_ln_w[l]), _row(rw_ln_b[l]), g64, g64m]
        o_rw_p, sh_p, big["wkv"][0] = _rwkv(P[0], 0, Bp, Tp, jnp.zeros((Bp, SUBLANES, 4 * GROUP_W), F32),
                                             jnp.zeros((Bp, RW_H, RW_HD, RW_HD), F32), prm,
                                             stacked=(depth, l, big["wkv"][0]))
        sh_in = jnp.pad(state_rw_shift[l], ((0, 0), (0, pad_sh)))[:, None, :]
        o_rw_s, sh_s, big["wkv"][1] = _rwkv(P[1], 0, Bs, Ts, sh_in, state_rw_wkv, prm, s0_layer=l,
                                             stacked=(depth, l, big["wkv"][1]))
        outs["sh"][0].append(sh_p[:, SUBLANES - 1, :RW_COLS])
        outs["sh"][1].append(sh_s[:, 0, :RW_COLS])

        bias_sm = jnp.zeros((LANES,), F32).at[SM_MLI - LANES:SM_MLI - LANES + HEADS].set(ml_bi[l])
        bias_sm = bias_sm.at[SM_MLF - LANES:SM_MLF - LANES + HEADS].set(ml_bf[l])

        def ml_consts(L):
            return [_tri(L), jnp.ones((L, LANES), BF16), ml_conv_w[l], _row(ml_conv_b[l]), _row(bias_sm), _row(ml_gn_w[l])]

        def ml_states(C, n, m, conv):
            b = n.shape[0]
            return [C, n[:, :, None, :], jnp.broadcast_to(m[:, :, None, None], (b, HEADS, 1, LANES)),
                    jnp.pad(conv, ((0, 0), (SUBLANES - conv.shape[1], 0), (0, 0)))]

        zp = lambda *s: jnp.zeros((Bp,) + s, F32)
        o_ml_p, st_p = _chunk_call("ml", P[0], 1, 4 * GROUP_W, True, 0, Bp, Tp,
                                   ml_states(zp(HEADS, HEAD_D, HEAD_D), zp(HEADS, HEAD_D), zp(HEADS), zp(3, 2 * GROUP_W)),
                                   ml_consts(CHUNK), [False] * 6, "mlstm_prompt", stacked=(depth, l, big["C"][0]))
        o_ml_s, st_s = _chunk_call("ml", P[1], 1, 4 * GROUP_W, True, 0, Bs, Ts,
                                   ml_states((state_ml_C, l), state_ml_n[l], state_ml_m[l], state_ml_conv[l]),
                                   ml_consts(Ts), [False] * 6, "mlstm_sample", stacked=(depth, l, big["C"][1]))
        for g, st in enumerate((st_p, st_s)):
            big["C"][g] = st[0]
            outs["n"][g].append(st[1][:, :, 0, :])
            outs["m"][g].append(st[2][:, :, 0, 0])
            outs["conv"][g].append(st[3][:, SUBLANES - 3:, :])

        c_p, ch_p = _ret_consts(0.0, Tp, CHUNK, rt_gn_w[l])
        c_s, ch_s = _ret_consts(float(PAST_LEN), Ts, Ts, rt_gn_w[l])
        o_rt_p, (big["rt"][0],) = _chunk_call("ret", P[0], 2, 4 * GROUP_W, False, 0, Bp, Tp, [zp(HEADS, HEAD_D, HEAD_D)],
                                              c_p, ch_p, "ret_prompt", stacked=(depth, l, big["rt"][0]))
        o_rt_s, (big["rt"][1],) = _chunk_call("ret", P[1], 2, 4 * GROUP_W, False, 0, Bs, Ts, [(state_rt_S, l)],
                                              c_s, ch_s, "ret_sample", stacked=(depth, l, big["rt"][1]))

        def gl_consts(L):
            return [_tri(L), _rows_at(gl_a2[l], SM_GLA - LANES, LANES), _row(gl_ab[l]), _row(gl_gn_w[l])]

        o_gl_p, (big["gl"][0],) = _chunk_call("gl", P[0], 4, 3 * GROUP_W, True, 0, Bp, Tp, [zp(HEADS, GL_DK, HEAD_D)],
                                              gl_consts(CHUNK), [False] * 4, "gla_prompt",
                                              stacked=(depth, l, big["gl"][0]))
        o_gl_s, (big["gl"][1],) = _chunk_call("gl", P[1], 4, 3 * GROUP_W, True, 0, Bs, Ts, [(state_gl_S, l)],
                                              gl_consts(Ts), [False] * 4, "gla_sample",
                                              stacked=(depth, l, big["gl"][1]))

        mixed = ((o_rw_p, o_ml_p, o_rt_p, o_gl_p), (o_rw_s, o_ml_s, o_rt_s, o_gl_s))
        for g in range(2):
            x[g], xb[g] = _mm_ln(list(mixed[g]), w_out_b, l, x[g], ln1_w[l], ln1_b[l])

        if l % 2 == 0:
            w1, w3, w2 = (w[l // 2][None].astype(BF16) for w in (ffn_w1, ffn_w3, ffn_w2))
            for g in range(2):
                rows = x[g].shape[0]
                tm = _pick(rows, (512, 256, 128))
                y = _ffn(xb[g], jnp.zeros((rows // tm,), jnp.int32), jnp.full((1,), rows // tm, jnp.int32), w1, w3, w2, tm)
                x[g], xb[g] = _add_ln(x[g], [y], None, ln2_w[l], ln2_b[l])
        else:
            tm = _pick(M, (512, 256, 128))
            n_tiles = 2 * M // tm + N_EXPERTS
            router_pad = jnp.pad(moe_router[l // 2], ((0, 0), (0, LANES - N_EXPERTS)))
            routed = [_router(v, router_pad) for v in x]
            gates = [r[0] for r in routed]
            ids = jnp.concatenate([r[1][:, :2] for r in routed], axis=0)
            row_src, dest, tile_expert, n_used = _moe_plan(ids, tm, n_tiles)
            ysorted = _ffn(jnp.concatenate(x, axis=0), tile_expert, n_used, moe_w1[l // 2].astype(BF16),
                           moe_w3[l // 2].astype(BF16), moe_w2[l // 2].astype(BF16), tm, row_src=row_src)
            for g, (r0, r1) in enumerate(((0, Mp), (Mp, M))):
                ys = [jnp.take(ysorted, dest[r0:r1, k], axis=0, mode="clip") for k in range(2)]
                x[g], xb[g] = _add_ln(x[g], ys, [gates[g][:, 0:1], gates[g][:, 1:2]], ln2_w[l], ln2_b[l])

    res = [x[0].reshape(Bp, Tp, D_MODEL), x[1].reshape(Bs, Ts, D_MODEL)]
    for key in ("sh", "wkv", "conv", "C", "n", "m", "rt", "gl"):
        res += list(big[key]) if key in big else [jnp.stack(outs[key][0]), jnp.stack(outs[key][1])]
    return tuple(res)
```

```python
import functools
import math

import jax
import jax.numpy as jnp
from jax import lax
from jax.experimental import pallas as pl
from jax.experimental.pallas import tpu as pltpu

F32 = jnp.float32
BF16 = jnp.bfloat16

D_MODEL = 2048
GROUP_W = 512
RW_COLS = 1696
RW_H, RW_HD = 8, 64
HEADS, HEAD_D = 4, 128
GL_DK = 64
CHUNK = 64
ALPHA = 4.0 ** 0.25
LN_EPS = 1e-5
GN_EPS = 1e-5
RW_GN_EPS = 64e-5
GL_TAU = 16.0
PAST_LEN = 16384
N_EXPERTS = 8

LANES = 128
SUBLANES = 8
VMEM_LIMIT = 48 * 1024 * 1024

N_PACKED = 15 * GROUP_W
SM_MLI, SM_MLF, SM_GLA = 160, 164, 168


def _nn(a, b):
    return jnp.dot(a, b, preferred_element_type=F32)


def _nt(a, b):
    return lax.dot_general(a, b, (((1,), (1,)), ((), ())), preferred_element_type=F32)


def _tn(a, b):
    return lax.dot_general(a, b, (((0,), (0,)), ((), ())), preferred_element_type=F32)


def _bf(x):
    return x.astype(BF16)


def _split3(x):
    hi = x.astype(BF16)
    r1 = x - hi.astype(F32)
    mid = r1.astype(BF16)
    lo = (r1 - mid.astype(F32)).astype(BF16)
    return hi, mid, lo


def _dot3(x, m, kind="nn"):
    f = {"nn": _nn, "nt": _nt, "tn": _tn}[kind]
    hi, mid, lo = _split3(x)
    return f(hi, m) + f(mid, m) + f(lo, m)


def _dot3r(m, x, kind="nn"):
    f = {"nn": _nn, "nt": _nt, "tn": _tn}[kind]
    hi, mid, lo = _split3(x)
    return f(m, hi) + f(m, mid) + f(m, lo)


def _sigmoid(x):
    return 1.0 / (1.0 + jnp.exp(-x))


def _softplus(x):
    return jnp.maximum(x, 0.0) + jnp.log1p(jnp.exp(-jnp.abs(x)))


def _log_sigmoid(x):
    return -_softplus(-x)


def _silu(x):
    return x * _sigmoid(x)


def _iota(shape, axis):
    return lax.broadcasted_iota(jnp.int32, shape, axis)


def _shift_rows(x, prev8, j):
    rows = x.shape[0]
    xr = pltpu.roll(x, j, 0)
    pr = pltpu.roll(prev8, j, 0)
    first = jnp.where(_iota((SUBLANES, x.shape[1]), 0) < j, pr, xr[0:SUBLANES])
    if rows == SUBLANES:
        return first
    return jnp.concatenate([first, xr[SUBLANES:]], axis=0)


def _pick(n, cands):
    return next(c for c in cands if n % c == 0)


def _cparams(n_axes):
    return pltpu.CompilerParams(dimension_semantics=("arbitrary",) * n_axes, vmem_limit_bytes=VMEM_LIMIT)


def _mm_kernel(n_in, *refs):
    xs, ws, o_ref = refs[:n_in], refs[n_in:2 * n_in], refs[2 * n_in]
    acc = _nn(xs[0][...], ws[0][...])
    for x_ref, w_ref in zip(xs[1:], ws[1:]):
        acc = acc + _nn(x_ref[...], w_ref[...])
    o_ref[...] = acc


def _mm(xs, w, layer, tn, name):
    M, N = xs[0].shape[0], w.shape[2]
    n_in = len(xs)
    tm = _pick(M, (1024, 512, 256, 128))
    in_specs = ([pl.BlockSpec((tm, x.shape[1]), lambda j, i: (i, 0)) for x in xs]
                + [pl.BlockSpec((None, x.shape[1], tn), functools.partial(lambda g, j, i: (layer, g, j), g))
                   for g, x in enumerate(xs)])
    return pl.pallas_call(
        functools.partial(_mm_kernel, n_in),
        grid=(N // tn, M // tm),
        in_specs=in_specs,
        out_specs=pl.BlockSpec((tm, tn), lambda j, i: (i, j)),
        out_shape=jax.ShapeDtypeStruct((M, N), F32),
        compiler_params=_cparams(2),
        name=name,
    )(*xs, *([w] * n_in))


def _mm_pair_kernel(n_a, xa_ref, xb_ref, w_ref, oa_ref, ob_ref):
    i = pl.program_id(1)

    @pl.when(i < n_a)
    def _():
        oa_ref[...] = _nn(xa_ref[...], w_ref[...])

    @pl.when(i >= n_a)
    def _():
        ob_ref[...] = _nn(xb_ref[...], w_ref[...])


def _mm_pair(xa, xb, w, layer, tn, name):
    K, N = xa.shape[1], w.shape[2]
    tm = _pick(math.gcd(xa.shape[0], xb.shape[0]), (1024, 512, 256, 128))
    n_a, n_b = xa.shape[0] // tm, xb.shape[0] // tm

    def ia(i):
        return jnp.minimum(i, n_a - 1)

    def ib(i):
        return jnp.maximum(i - n_a, 0)

    return pl.pallas_call(
        functools.partial(_mm_pair_kernel, n_a),
        grid=(N // tn, n_a + n_b),
        in_specs=[pl.BlockSpec((tm, K), lambda j, i: (ia(i), 0)),
                  pl.BlockSpec((tm, K), lambda j, i: (ib(i), 0)),
                  pl.BlockSpec((None, K, tn), lambda j, i: (layer, 0, j))],
        out_specs=[pl.BlockSpec((tm, tn), lambda j, i: (ia(i), j)),
                   pl.BlockSpec((tm, tn), lambda j, i: (ib(i), j))],
        out_shape=[jax.ShapeDtypeStruct((xa.shape[0], N), F32), jax.ShapeDtypeStruct((xb.shape[0], N), F32)],
        compiler_params=_cparams(2),
        name=name,
    )(xa, xb, w)


def _mm_ln_kernel(n_in, *refs):
    xs, ws = refs[:n_in], refs[n_in:2 * n_in]
    r_ref, w_ref, b_ref, o_ref, ob_ref = refs[2 * n_in:]
    acc = _nn(xs[0][...], ws[0][...])
    for x_ref, wt_ref in zip(xs[1:], ws[1:]):
        acc = acc + _nn(x_ref[...], wt_ref[...])
    out = _layer_norm(ALPHA * r_ref[...] + acc, w_ref[...], b_ref[...])
    o_ref[...] = out
    ob_ref[...] = out.astype(BF16)


def _mm_ln(xs, w, layer, resid, ln_w, ln_b):
    M, N = resid.shape
    n_in = len(xs)
    tm = _pick(M, (512, 256, 128))
    row = pl.BlockSpec((tm, N), lambda i: (i, 0))
    in_specs = ([pl.BlockSpec((tm, x.shape[1]), lambda i: (i, 0)) for x in xs]
                + [pl.BlockSpec((None, x.shape[1], N), functools.partial(lambda g, i: (layer, g, 0), g))
                   for g, x in enumerate(xs)]
                + [row] + [pl.BlockSpec((1, N), lambda i: (0, 0))] * 2)
    return pl.pallas_call(
        functools.partial(_mm_ln_kernel, n_in),
        grid=(M // tm,),
        in_specs=in_specs,
        out_specs=[row, row],
        out_shape=[jax.ShapeDtypeStruct((M, N), F32), jax.ShapeDtypeStruct((M, N), BF16)],
        compiler_params=_cparams(1),
        name="mm_out_ln",
    )(*xs, *([w] * n_in), resid, ln_w.reshape(1, N), ln_b.reshape(1, N))


def _layer_norm(z, w, b):
    mu = jnp.mean(z, axis=-1, keepdims=True)
    zc = z - mu
    var = jnp.mean(zc * zc, axis=-1, keepdims=True)
    return zc * lax.rsqrt(var + LN_EPS) * w + b


def _add_ln_kernel(n_y, gated, x_ref, *refs):
    ys = refs[:n_y]
    gs = refs[n_y:2 * n_y] if gated else ()
    w_ref, b_ref, o_ref, ob_ref = refs[-4:]
    z = ALPHA * x_ref[...]
    for i, y_ref in enumerate(ys):
        y = y_ref[...]
        if gated:
            y = gs[i][...] * y
        z = z + y
    out = _layer_norm(z, w_ref[...], b_ref[...])
    o_ref[...] = out
    ob_ref[...] = out.astype(BF16)


def _add_ln(x, ys, gates, w, b):
    M = x.shape[0]
    tm = _pick(M, (512, 256, 128))
    gated = gates is not None
    row = pl.BlockSpec((tm, D_MODEL), lambda i: (i, 0))
    in_specs = [row] + [row] * len(ys)
    args = [x] + list(ys)
    if gated:
        in_specs += [pl.BlockSpec((tm, 1), lambda i: (i, 0))] * len(ys)
        args += list(gates)
    in_specs += [pl.BlockSpec((1, D_MODEL), lambda i: (0, 0))] * 2
    args += [w.reshape(1, D_MODEL), b.reshape(1, D_MODEL)]
    return pl.pallas_call(
        functools.partial(_add_ln_kernel, len(ys), gated),
        grid=(M // tm,),
        in_specs=in_specs,
        out_specs=[row, row],
        out_shape=[jax.ShapeDtypeStruct((M, D_MODEL), F32), jax.ShapeDtypeStruct((M, D_MODEL), BF16)],
        compiler_params=_cparams(1),
        name="add_ln",
    )(*args)


def _ffn_up_kernel(te_ref, nu_ref, x_ref, w1_ref, w3_ref, h_ref):
    @pl.when(pl.program_id(0) < nu_ref[0])
    def _():
        x = x_ref[...]
        h_ref[...] = (_silu(_nn(x, w1_ref[...])) * _nn(x, w3_ref[...])).astype(BF16)

    @pl.when(pl.program_id(0) >= nu_ref[0])
    def _():
        h_ref[...] = jnp.zeros(h_ref.shape, BF16)


def _ffn_up_gather_kernel(te_ref, nu_ref, src_ref, x_hbm, w1_ref, w3_ref, h_ref, xbuf, xb16, sem):
    i, f = pl.program_id(0), pl.program_id(1)
    tm = xbuf.shape[0]
    n_used = nu_ref[0]

    def start_tile(tile):
        def body(r, c):
            src = x_hbm.at[pl.ds(src_ref[tile * tm + r], 1)]
            pltpu.make_async_copy(src, xbuf.at[pl.ds(r, 1)], sem.at[0]).start()
            return c
        lax.fori_loop(0, tm, body, 0, unroll=8)

    @pl.when((f == 0) & (i == 0) & (n_used > 0))
    def _():
        start_tile(0)

    @pl.when((f == 0) & (i < n_used))
    def _():
        pltpu.make_async_copy(x_hbm.at[pl.ds(0, tm)], xbuf, sem.at[0]).wait()
        xb16[...] = xbuf[...].astype(BF16)

        @pl.when(i + 1 < n_used)
        def _():
            start_tile(i + 1)

    @pl.when(i < n_used)
    def _():
        x = xb16[...]
        h_ref[...] = (_silu(_nn(x, w1_ref[...])) * _nn(x, w3_ref[...])).astype(BF16)

    @pl.when(i >= n_used)
    def _():
        h_ref[...] = jnp.zeros(h_ref.shape, BF16)


def _ffn_down_kernel(te_ref, nu_ref, h_ref, w2_ref, o_ref):
    @pl.when(pl.program_id(0) < nu_ref[0])
    def _():
        o_ref[...] = _nn(h_ref[...], w2_ref[...])

    @pl.when(pl.program_id(0) >= nu_ref[0])
    def _():
        o_ref[...] = jnp.zeros(o_ref.shape, F32)


FFN_TF = 1408
FFN_TN = 512
FFN_DOWN_TILE_ELEMS = 2816 * 2048


def _ffn(xs, tile_expert, n_used, w1, w3, w2, tm, row_src=None):
    R = xs.shape[0] if row_src is None else row_src.shape[0]
    F = w1.shape[2]
    tn = FFN_TN * (FFN_DOWN_TILE_ELEMS // (F * FFN_TN))
    nf, nn = F // FFN_TF, D_MODEL // tn

    def last(i, j, nu, n):
        return jnp.where(i < nu[0], j, n - 1)

    if row_src is None:
        up_spec = pltpu.PrefetchScalarGridSpec(
            num_scalar_prefetch=2,
            grid=(R // tm, nf),
            in_specs=[
                pl.BlockSpec((tm, D_MODEL), lambda i, f, te, nu: (i, 0)),
                pl.BlockSpec((None, D_MODEL, FFN_TF), lambda i, f, te, nu: (te[i], 0, last(i, f, nu, nf))),
                pl.BlockSpec((None, D_MODEL, FFN_TF), lambda i, f, te, nu: (te[i], 0, last(i, f, nu, nf))),
            ],
            out_specs=pl.BlockSpec((tm, FFN_TF), lambda i, f, te, nu: (i, f)),
        )
        h = pl.pallas_call(
            _ffn_up_kernel,
            grid_spec=up_spec,
            out_shape=jax.ShapeDtypeStruct((R, F), BF16),
            compiler_params=_cparams(2),
            name="ffn_up",
        )(tile_expert, n_used, xs, w1, w3)
    else:
        up_spec = pltpu.PrefetchScalarGridSpec(
            num_scalar_prefetch=3,
            grid=(R // tm, nf),
            in_specs=[
                pl.BlockSpec(memory_space=pl.ANY),
                pl.BlockSpec((None, D_MODEL, FFN_TF), lambda i, f, te, nu, src: (te[i], 0, last(i, f, nu, nf))),
                pl.BlockSpec((None, D_MODEL, FFN_TF), lambda i, f, te, nu, src: (te[i], 0, last(i, f, nu, nf))),
            ],
            out_specs=pl.BlockSpec((tm, FFN_TF), lambda i, f, te, nu, src: (i, f)),
            scratch_shapes=[pltpu.VMEM((tm, D_MODEL), F32), pltpu.VMEM((tm, D_MODEL), BF16),
                            pltpu.SemaphoreType.DMA((1,))],
        )
        h = pl.pallas_call(
            _ffn_up_gather_kernel,
            grid_spec=up_spec,
            out_shape=jax.ShapeDtypeStruct((R, F), BF16),
            compiler_params=_cparams(2),
            name="ffn_up_gather",
        )(tile_expert, n_used, row_src, xs, w1, w3)
    down_spec = pltpu.PrefetchScalarGridSpec(
        num_scalar_prefetch=2,
        grid=(R // tm, nn),
        in_specs=[
            pl.BlockSpec((tm, F), lambda i, n, te, nu: (i, 0)),
            pl.BlockSpec((None, F, tn), lambda i, n, te, nu: (te[i], 0, last(i, n, nu, nn))),
        ],
        out_specs=pl.BlockSpec((tm, tn), lambda i, n, te, nu: (i, n)),
    )
    return pl.pallas_call(
        _ffn_down_kernel,
        grid_spec=down_spec,
        out_shape=jax.ShapeDtypeStruct((R, D_MODEL), F32),
        compiler_params=_cparams(2),
        name="ffn_down",
    )(tile_expert, n_used, h, w2)


def _router_kernel(x_ref, r_ref, g_ref, i_ref):
    xh, xm, xl = _split3(x_ref[...])
    rh, rm, rl = _split3(r_ref[...])
    logits = (_nn(xh, rh) + _nn(xh, rm) + _nn(xm, rh)) + (_nn(xh, rl) + _nn(xl, rh) + _nn(xm, rm))
    shape = logits.shape
    lane = _iota(shape, 1).astype(F32)
    neg = jnp.float32(-jnp.inf)
    l1 = jnp.where(lane < N_EXPERTS, logits, neg)
    m1 = jnp.max(l1, axis=1, keepdims=True)
    i1 = jnp.min(jnp.where(l1 == m1, lane, float(LANES)), axis=1, keepdims=True)
    l2 = jnp.where(lane == i1, neg, l1)
    m2 = jnp.max(l2, axis=1, keepdims=True)
    i2 = jnp.min(jnp.where(l2 == m2, lane, float(LANES)), axis=1, keepdims=True)
    e = jnp.exp(m2 - m1)
    g1 = 1.0 / (1.0 + e)
    g2 = e / (1.0 + e)
    g_ref[...] = jnp.where(lane == 0.0, g1, jnp.where(lane == 1.0, g2, 0.0))
    i_ref[...] = jnp.where(lane == 0.0, i1, jnp.where(lane == 1.0, i2, 0.0)).astype(jnp.int32)


def _router(x, router_pad):
    M = x.shape[0]
    tm = _pick(M, (512, 256, 128))
    return pl.pallas_call(
        _router_kernel,
        grid=(M // tm,),
        in_specs=[pl.BlockSpec((tm, D_MODEL), lambda i: (i, 0)),
                  pl.BlockSpec((D_MODEL, LANES), lambda i: (0, 0))],
        out_specs=[pl.BlockSpec((tm, LANES), lambda i: (i, 0))] * 2,
        out_shape=[jax.ShapeDtypeStruct((M, LANES), F32), jax.ShapeDtypeStruct((M, LANES), jnp.int32)],
        compiler_params=_cparams(1),
        name="router",
    )(x, router_pad)


def _rw_prep(pm, prm):
    (w0, a0, w2p, a2p, g2p, k_k, k_a, _, _, _, g64, _) = prm
    r = pm[:, 0:512]
    k = pm[:, 512:1024]
    v = pm[:, 1024:1536]
    sm = pm[:, 1536:1792]
    logw = -_softplus(-(w0 + _nn(_bf(jnp.tanh(sm)), w2p))) - 0.5
    lw = -jnp.exp(logw)
    a = _sigmoid(a0 + _nn(_bf(sm), a2p))
    g = _nn(_bf(_sigmoid(sm)), g2p)
    kk = k * k_k
    ss = jnp.concatenate([_dot3((kk * kk)[:, LANES * p:LANES * (p + 1)], g64) for p in range(4)], axis=1)
    kk = kk / jnp.maximum(jnp.sqrt(ss), 1e-12)
    kmod = k * (1.0 + (a - 1.0) * k_a)
    return r, lw, kmod, kk, kk * a, v, g


def _rw_post(y, r, kmod, v, g, prm):
    (_, _, _, _, _, _, _, r_k, ln_w, ln_b, g64, g64m) = prm

    def per_head(x, m):
        return jnp.concatenate([_dot3(x[:, LANES * p:LANES * (p + 1)], m) for p in range(4)], axis=1)

    yc = y - per_head(y, g64m)
    var = per_head(yc * yc, g64m)
    yn = yc * lax.rsqrt(var + RW_GN_EPS) * ln_w + ln_b
    bonus = per_head(r * kmod * r_k, g64) * v
    return (yn + bonus) * g


def _rw_chunks(L, seqs, scr, s_ref):
    q_s, rr_s, kt_s, bt_s, kp_s, bp_s, v_s, wl_s, y_s = scr
    units = [(si, r0, h) for (si, r0) in seqs for h in range(RW_H)]

    def fetch(ref, r0, h):
        return ref[pl.ds(r0, L), RW_HD * h:RW_HD * (h + 1)]

    row = _iota((2 * L, 2 * L), 0)
    col = _iota((2 * L, 2 * L), 1)
    colm = jnp.where(col >= L, col - L, col)
    mask4 = colm <= jnp.where(row < L, row - 1, row - L)
    qf = [fetch(q_s, r0, h) for (_, r0, h) in units]
    rf = [fetch(rr_s, r0, h) for (_, r0, h) in units]
    vb = [_bf(fetch(v_s, r0, h)) for (_, r0, h) in units]
    g4 = [jnp.where(mask4, _nt(_bf(jnp.concatenate([qf[i], rf[i]], axis=0)),
                               _bf(jnp.concatenate([fetch(kt_s, r0, h), fetch(bt_s, r0, h)], axis=0))), 0.0)
          for i, (_, r0, h) in enumerate(units)]
    av = [_nn(_bf(g4[i][:, :L]), vb[i]) for i in range(len(units))]
    x = [jnp.concatenate([qf[i], av[i][:L]], axis=1) for i in range(len(units))]
    pw = [g4[i][:L, L:] for i in range(len(units))]
    x = [x[i] - _nn(_bf(pw[i]), _bf(x[i])) for i in range(len(units))]
    for _ in range(int(math.log2(L)) - 1):
        pw = [_nn(_bf(m), _bf(m)) for m in pw]
        x = [x[i] + _nn(_bf(pw[i]), _bf(x[i])) for i in range(len(units))]
    ry = [jnp.concatenate([rf[i], av[i][L:]], axis=1) - _nn(_bf(g4[i][L:, L:]), _bf(x[i])) for i in range(len(units))]
    gh = [_tn(_bf(x[i]), _bf(fetch(bp_s, r0, h))) for i, (_, r0, h) in enumerate(units)]
    hk = [_tn(vb[i], _bf(fetch(kp_s, r0, h))) for i, (_, r0, h) in enumerate(units)]
    ys = []
    for i, (si, r0, h) in enumerate(units):
        S = s_ref[si, h]
        Sb = _bf(S)
        ys.append(_nt(_bf(ry[i][:, :RW_HD]), Sb) + ry[i][:, RW_HD:])
        w_last = fetch(wl_s, r0, h)[0:1, :]
        s_ref[si, h] = S * w_last - _nn(Sb, _bf(gh[i][:RW_HD])) + hk[i] - gh[i][RW_HD:]
    for i in range(0, len(units), 2):
        _, r0, h = units[i]
        y_s[pl.ds(r0, L), RW_HD * h:RW_HD * (h + 2)] = jnp.concatenate([ys[i], ys[i + 1]], axis=1)


N_RW_PRM = 13


def _rwkv_kernel(is_prompt, nsq, L, grp, n_alias, own_layer, *refs):
    n_p = nsq if is_prompt else 1
    p_refs = refs[:n_p]
    sh0_ref, s0_ref = refs[n_p:n_p + 2]
    pos = n_p + 2
    prm_refs = refs[pos:pos + N_RW_PRM]
    pos += N_RW_PRM
    tri_ref = refs[pos]
    pos += 1 + n_alias
    o_ref, sh_out_ref, s_out_ref = refs[pos:pos + 3]
    scr = refs[pos + 3:]
    g_s, r_s, k_s = scr[9:12]
    mu = prm_refs[0][...]
    prm = tuple(r[...] for r in prm_refs[1:])

    stacked_ref = s_out_ref if own_layer is not None else None
    if stacked_ref is not None:
        s_out_ref = stacked_ref.at[own_layer]

    def zero_other_layers():
        if stacked_ref is not None:
            for d in range(stacked_ref.shape[0]):
                if d != own_layer:
                    stacked_ref[d] = jnp.zeros(stacked_ref.shape[1:], F32)

    if is_prompt:
        prev_scr = scr[12]

        @pl.when(pl.program_id(0) == 0)
        def _():
            s_out_ref[...] = s0_ref[...]
            prev_scr[...] = sh0_ref[...]
            zero_other_layers()

        ps, pps = [], []
        for s in range(nsq):
            ps.append(p_refs[s][...])
            pps.append(_shift_rows(ps[s], prev_scr[s], 1))
            prev_scr[s] = ps[s][L - SUBLANES:L]
        p = jnp.concatenate(ps, axis=0)
        pp = jnp.concatenate(pps, axis=0)
        sh_out_ref[...] = prev_scr[...]
    else:
        p = p_refs[0][...]
        p3 = p.reshape(nsq, L, p.shape[1])
        pp = jnp.where(_iota(p3.shape, 1) == 0, sh0_ref[...], pltpu.roll(p3, 1, 1)).reshape(p.shape)
        sh_out_ref[...] = p3[:, L - 1:L, :]
        s_out_ref[...] = s0_ref[...]
        zero_other_layers()

    r, lw, kmod, kk, bb, v, g = _rw_prep(p + (pp - p) * mu, prm)
    cum = _dot3r(tri_ref[...], lw)
    cum3 = cum.reshape(nsq, L, cum.shape[1])
    tot = jnp.broadcast_to(cum3[:, L - 1:L, :], cum3.shape).reshape(cum.shape)
    e_neg = jnp.exp(-cum)
    e_rem = jnp.exp(tot - cum)
    q_s, rr_s, kt_s, bt_s, kp_s, bp_s, v_s, wl_s, y_s = scr[:9]
    q_s[...] = kk * jnp.exp(cum - lw)
    rr_s[...] = r * jnp.exp(cum)
    kt_s[...] = kmod * e_neg
    bt_s[...] = bb * e_neg
    kp_s[...] = kmod * e_rem
    bp_s[...] = bb * e_rem
    v_s[...] = v
    wl_s[...] = jnp.exp(tot)
    g_s[...], r_s[...], k_s[...] = g, r, kmod

    if is_prompt:
        _rw_chunks(L, [(s, s * L) for s in range(nsq)], scr[:9], s_out_ref)
    else:
        def body(i, c):
            _rw_chunks(L, [(i * grp + j, pl.multiple_of((i * grp + j) * L, SUBLANES)) for j in range(grp)],
                       scr[:9], s_out_ref)
            return c
        lax.fori_loop(0, nsq // grp, body, 0)

    out = _rw_post(y_s[...], r_s[...], k_s[...], v_s[...], g_s[...], prm).astype(BF16)
    if is_prompt:
        for s in range(nsq):
            o_ref[s] = out[s * L:(s + 1) * L]
    else:
        o_ref[...] = out


def _rwkv(P, row0, B, T, sh0, s0, prm_list, s0_layer=None, stacked=None):
    is_prompt = T % CHUNK == 0
    wide = 4 * GROUP_W
    s_shape = (B, RW_H, RW_HD, RW_HD)
    if is_prompt:
        L, nsq, grid = CHUNK, B, (T // CHUNK,)
        rows = nsq * L
        in_specs = [pl.BlockSpec((L, wide), functools.partial(lambda s, c: ((row0 + s * T) // L + c, 0), s))
                    for s in range(nsq)]
        in_specs += [pl.BlockSpec(sh0.shape, lambda c: (0, 0, 0)), pl.BlockSpec(s0.shape, lambda c: (0, 0, 0, 0))]
        out_specs = [pl.BlockSpec((nsq, L, GROUP_W), lambda c: (0, c, 0)),
                     pl.BlockSpec(sh0.shape, lambda c: (0, 0, 0)), pl.BlockSpec(s0.shape, lambda c: (0, 0, 0, 0))]
        out_shape = [jax.ShapeDtypeStruct((nsq, T, GROUP_W), BF16)]
        args = [P] * nsq
    else:
        L, nsq, grid = T, 16, (B // 16,)
        rows = nsq * L
        s_spec = (pl.BlockSpec((nsq, RW_H, RW_HD, RW_HD), lambda i: (i, 0, 0, 0)) if s0_layer is None else
                  pl.BlockSpec((None, nsq, RW_H, RW_HD, RW_HD), lambda i: (s0_layer, i, 0, 0, 0)))
        in_specs = [pl.BlockSpec((rows, wide), lambda i: (row0 // rows + i, 0)),
                    pl.BlockSpec((nsq, 1, wide), lambda i: (i, 0, 0)), s_spec]
        out_specs = [pl.BlockSpec((rows, GROUP_W), lambda i: (i, 0)),
                     pl.BlockSpec((nsq, 1, wide), lambda i: (i, 0, 0)),
                     pl.BlockSpec((nsq, RW_H, RW_HD, RW_HD), lambda i: (i, 0, 0, 0))]
        out_shape = [jax.ShapeDtypeStruct((B * T, GROUP_W), BF16)]
        args = [P]
    seq_of_row = jnp.arange(rows) // L
    same = seq_of_row[:, None] == seq_of_row[None, :]
    tri = (same & (jnp.arange(rows)[:, None] >= jnp.arange(rows)[None, :])).astype(BF16)
    consts = list(prm_list) + [tri]
    in_specs += [pl.BlockSpec(x.shape, lambda *a: (0, 0)) for x in consts]
    out_shape += [jax.ShapeDtypeStruct(sh0.shape, F32), jax.ShapeDtypeStruct(s_shape, F32)]
    args += [sh0, s0] + consts
    aliases = {}
    own_layer = None
    if stacked is not None:
        depth, layer, prev = stacked
        blk = tuple(out_specs[2].block_shape)
        out_shape[2] = jax.ShapeDtypeStruct((depth,) + s_shape, F32)
        if prev is None:
            own_layer = layer
            out_specs[2] = pl.BlockSpec((depth,) + blk, (lambda c: (0, 0, 0, 0, 0)) if is_prompt
                                        else (lambda i: (0, i, 0, 0, 0)))
        else:
            out_specs[2] = pl.BlockSpec((None,) + blk, (lambda c: (layer, 0, 0, 0, 0)) if is_prompt
                                        else (lambda i: (layer, i, 0, 0, 0)))
            aliases[len(args)] = 2
            in_specs.append(pl.BlockSpec(memory_space=pl.ANY))
            args.append(prev)
    scratch = [pltpu.VMEM((rows, GROUP_W), F32)] * 12
    if is_prompt:
        scratch.append(pltpu.VMEM(sh0.shape, F32))
    outs = pl.pallas_call(
        functools.partial(_rwkv_kernel, is_prompt, nsq, L, RW_SEQ_GROUP, len(aliases), own_layer),
        grid=grid,
        in_specs=in_specs,
        out_specs=out_specs,
        out_shape=out_shape,
        input_output_aliases=aliases,
        scratch_shapes=scratch,
        compiler_params=_cparams(1),
        name="rwkv_prompt" if is_prompt else "rwkv_sample",
    )(*args)
    return outs[0].reshape(B * T, GROUP_W), outs[1], outs[2]


SEQ_GROUP = 4
RW_SEQ_GROUP = 8
GL_SAFE_SPREAD = 40.0


def _causal(L):
    return _iota((L, L), 0) >= _iota((L, L), 1)


def _heads(units):
    return [(u, h) for u in range(len(units)) for h in range(HEADS)]


def _hs(h):
    return slice(h * HEAD_D, (h + 1) * HEAD_D)


def _ret_units(L, units, S_ref, consts):
    cosv, sinv, dm_ref, qd_ref, kd_ref, gn, c_dec = consts
    uh = _heads(units)

    def rot(x):
        return x * cosv + pltpu.roll(x, HEAD_D // 2, 1) * sinv

    qr = [rot(units[u][0](h * HEAD_D, HEAD_D)) for u, h in uh]
    kr = [rot(units[u][0](GROUP_W + h * HEAD_D, HEAD_D)) * (HEAD_D ** -0.5) for u, h in uh]
    vb = [_bf(units[u][0](2 * GROUP_W + h * HEAD_D, HEAD_D)) for u, h in uh]
    S = [S_ref[units[u][3], h] for u, h in uh]
    A = [_nt(_bf(qr[i]), _bf(kr[i])) * dm_ref[h] for i, (u, h) in enumerate(uh)]
    o = [_nn(_bf(A[i]), vb[i]) + _nn(_bf(qr[i] * qd_ref[h]), _bf(S[i])) for i, (u, h) in enumerate(uh)]
    for i, (u, h) in enumerate(uh):
        S_ref[units[u][3], h] = c_dec[h] * S[i] + _tn(_bf(kr[i] * kd_ref[h]), vb[i])
    for i, (u, h) in enumerate(uh):
        oc = o[i] - jnp.mean(o[i], axis=1, keepdims=True)
        on = oc * lax.rsqrt(jnp.mean(oc * oc, axis=1, keepdims=True) + GN_EPS) * gn[:, _hs(h)]
        g = units[u][0](3 * GROUP_W + h * HEAD_D, HEAD_D)
        units[u][2](h * HEAD_D, _bf(_silu(g) * on))


def _ml_units(L, units, st_refs, consts):
    C_ref, n_ref, m_ref, cv_ref = st_refs
    ltri, ones_l, conv_w, conv_b, bias_sm, gn = consts
    uh = _heads(units)
    conv, xs, bc_all = [], [], []
    for get, get_sm, _, sidx in units:
        x = get(0, 2 * GROUP_W)
        prev8 = cv_ref[sidx]
        c = (conv_b + _shift_rows(x, prev8, 3) * conv_w[0:1] + _shift_rows(x, prev8, 2) * conv_w[1:2]
             + _shift_rows(x, prev8, 1) * conv_w[2:3] + x * conv_w[3:4])
        cv_ref[sidx] = x[L - SUBLANES:L]
        conv.append(_silu(c))
        xs.append(get_sm(LANES, LANES) + bias_sm)
        bc_all.append(_dot3r(ltri, _log_sigmoid(xs[-1])))
    causal = _causal(L)
    lane0 = _iota((L, LANES), 1) == 0
    q = [conv[u][:, _hs(h)] for u, h in uh]
    k = [conv[u][:, GROUP_W + h * HEAD_D:GROUP_W + (h + 1) * HEAD_D] * (HEAD_D ** -0.5) for u, h in uh]
    vb = [_bf(units[u][0](2 * GROUP_W + h * HEAD_D, HEAD_D)) for u, h in uh]
    ig = [xs[u][:, SM_MLI - LANES + h:SM_MLI - LANES + h + 1] for u, h in uh]
    bc = [bc_all[u][:, SM_MLF - LANES + h:SM_MLF - LANES + h + 1] for u, h in uh]
    m_prev = [m_ref[units[u][3], h][:, 0:1] for u, h in uh]
    row = [_dot3r(ones_l, jnp.where(lane0, ig[i] - bc[i], 0.0), "nt") for i in range(len(uh))]
    dlog = [jnp.where(causal, bc[i] + row[i], -jnp.inf) for i in range(len(uh))]
    inter = [bc[i] + m_prev[i] for i in range(len(uh))]
    m_t = [jnp.maximum(inter[i], jnp.max(dlog[i], axis=1, keepdims=True)) for i in range(len(uh))]
    dmat = [jnp.exp(dlog[i] - m_t[i]) for i in range(len(uh))]
    s_in = [jnp.exp(inter[i] - m_t[i]) for i in range(len(uh))]
    C = [C_ref[units[u][3], h] for u, h in uh]
    n = [n_ref[units[u][3], h] for u, h in uh]
    qb = [_bf(x) for x in q]
    A = [_nt(qb[i], _bf(k[i])) * dmat[i] for i in range(len(uh))]
    num = [_nn(_bf(A[i]), vb[i]) + s_in[i] * _nt(qb[i], _bf(C[i])) for i in range(len(uh))]
    den = [jnp.sum(A[i], axis=1, keepdims=True) + s_in[i] * jnp.sum(q[i] * n[i], axis=1, keepdims=True)
           for i in range(len(uh))]
    hh = [num[i] / jnp.maximum(jnp.abs(den[i]), jnp.exp(-m_t[i])) for i in range(len(uh))]
    for i, (u, h) in enumerate(uh):
        sidx = units[u][3]
        m_new = m_t[i][L - 1:L, :]
        b_last = bc[i][L - 1:L, :]
        carry = jnp.exp(b_last + m_prev[i] - m_new)
        wk = jnp.exp(b_last - bc[i] + ig[i] - m_new) * k[i]
        C_ref[sidx, h] = carry * C[i] + _tn(vb[i], _bf(wk))
        n_ref[sidx, h] = carry * n[i] + jnp.sum(wk, axis=0, keepdims=True)
        m_ref[sidx, h] = jnp.broadcast_to(m_new, (1, LANES))
    for i, (u, h) in enumerate(uh):
        hc = hh[i] - jnp.mean(hh[i], axis=1, keepdims=True)
        hn = hc * lax.rsqrt(jnp.mean(hc * hc, axis=1, keepdims=True) + GN_EPS) * gn[:, _hs(h)]
        o_pre = units[u][0](3 * GROUP_W + h * HEAD_D, HEAD_D)
        units[u][2](h * HEAD_D, _bf(_sigmoid(o_pre) * hn))


def _gl_units(L, units, S_ref, consts):
    ltri, a2p, ab, gn = consts
    uh = _heads(units)
    mid = L // 2 - 1
    qs, kall, bcs, q_st, k_st, e_last = [], [], [], [], [], []
    spread = jnp.float32(0.0)
    for get, get_sm, _, _ in units:
        la = _log_sigmoid(_nn(_bf(get_sm(LANES, LANES)), a2p) + ab) / GL_TAU
        bc = _dot3r(ltri, la)
        b_last = bc[L - 1:L, :]
        qs.append(get(0, GL_DK * HEADS) * (GL_DK ** -0.5))
        kall.append(get(GL_DK * HEADS, GL_DK * HEADS))
        bcs.append(bc)
        spread = jnp.maximum(spread, jnp.max(jnp.abs(bc - bc[mid:mid + 1, :])))
        q_st.append(_bf(qs[-1] * jnp.exp(bc)))
        k_st.append(_bf(kall[-1] * jnp.exp(b_last - bc)))
        e_last.append(jnp.exp(b_last))
    causal = _causal(L)
    eye = _iota((GL_DK, GL_DK), 0) == _iota((GL_DK, GL_DK), 1)

    def ks(h):
        return slice(h * GL_DK, (h + 1) * GL_DK)

    def scores_factored():
        q_in = [_bf(qs[u] * jnp.exp(bcs[u] - bcs[u][mid:mid + 1, :])) for u in range(len(units))]
        k_in = [_bf(kall[u] * jnp.exp(bcs[u][mid:mid + 1, :] - bcs[u])) for u in range(len(units))]
        return [jnp.where(causal, _nt(q_in[u][:, ks(h)], k_in[u][:, ks(h)]), 0.0) for u, h in uh]

    def scores_direct():
        rows = _iota((L, GL_DK * HEADS), 0)
        cols = _iota((L, L), 1)

        def body(s, acc):
            out = []
            for u in range(len(units)):
                pick = rows == s
                k_s = jnp.sum(jnp.where(pick, kall[u], 0.0), axis=0, keepdims=True)
                b_s = jnp.sum(jnp.where(pick, bcs[u], 0.0), axis=0, keepdims=True)
                t = qs[u] * k_s * jnp.exp(jnp.where(rows >= s, bcs[u] - b_s, -jnp.inf))
                for h in range(HEADS):
                    col = jnp.sum(t[:, ks(h)], axis=1, keepdims=True)
                    out.append(jnp.where(cols == s, col, acc[u * HEADS + h]))
            return out

        return lax.fori_loop(0, L, body, [jnp.zeros((L, L), F32) for _ in uh])

    def rest(scores):
        A = scores()
        vb = [_bf(units[u][0](GROUP_W + h * HEAD_D, HEAD_D)) for u, h in uh]
        S = [S_ref[units[u][3], h] for u, h in uh]
        o = [_nn(_bf(A[i]), vb[i]) + _nn(q_st[u][:, ks(h)], _bf(S[i])) for i, (u, h) in enumerate(uh)]
        for i, (u, h) in enumerate(uh):
            e_col = jnp.sum(jnp.where(eye, e_last[u][:, ks(h)], 0.0), axis=1, keepdims=True)
            S_ref[units[u][3], h] = e_col * S[i] + _tn(k_st[u][:, ks(h)], vb[i])
        for i, (u, h) in enumerate(uh):
            on = o[i] * lax.rsqrt(jnp.mean(o[i] * o[i], axis=1, keepdims=True) + GN_EPS) * gn[:, _hs(h)]
            g = units[u][0](2 * GROUP_W + h * HEAD_D, HEAD_D)
            units[u][2](h * HEAD_D, _bf(_silu(g) * on))

    lax.cond(spread < GL_SAFE_SPREAD, lambda: rest(scores_factored), lambda: rest(scores_direct))


def _chunk_kernel(kind, nslab, nunit, L, n_state, n_const, n_alias, has_sm, own_layer, *refs):
    pos = 0
    p_refs = refs[pos:pos + nslab]; pos += nslab
    sm_refs = refs[pos:pos + (nslab if has_sm else 0)]; pos += (nslab if has_sm else 0)
    st_in = refs[pos:pos + n_state]; pos += n_state
    c_refs = refs[pos:pos + n_const]; pos += n_const + n_alias
    o_ref = refs[pos]; pos += 1
    st_out = refs[pos:pos + n_state]; pos += n_state
    c = pl.program_id(1)
    if own_layer is not None:
        stacked_ref = st_out[0]
        st_out = (stacked_ref.at[own_layer],) + tuple(st_out[1:])

    @pl.when(c == 0)
    def _():
        for a, b in zip(st_in, st_out):
            b[...] = a[...]
        if own_layer is not None:
            for d in range(stacked_ref.shape[0]):
                if d != own_layer:
                    stacked_ref[d] = jnp.zeros(stacked_ref.shape[1:], F32)

    def unit(slab, seq, sidx):
        r0 = seq * L if isinstance(seq, int) else pl.multiple_of(seq * L, SUBLANES)

        def get(col, width):
            return p_refs[slab][pl.ds(r0, L), col:col + width]

        def get_sm(col, width):
            return sm_refs[slab][pl.ds(r0, L), col:col + width]

        def put(col, val):
            o_ref[slab, pl.ds(r0, L), col:col + val.shape[1]] = val

        return (get, get_sm, put, sidx)

    def run(units):
        if kind == "ret":
            cosv, sinv = c_refs[0][...], c_refs[1][...]
            consts = (cosv, sinv, c_refs[2], c_refs[3], c_refs[4], c_refs[5][...], _RT_CDEC[L])
            _ret_units(L, units, st_out[0], consts)
        elif kind == "ml":
            _ml_units(L, units, st_out, tuple(r[...] for r in c_refs))
        else:
            _gl_units(L, units, st_out[0], tuple(r[...] for r in c_refs))

    if nunit == 1:
        run([unit(s, 0, s) for s in range(nslab)])
    else:
        def body(i, carry):
            run([unit(0, i * SEQ_GROUP + j, i * SEQ_GROUP + j) for j in range(SEQ_GROUP)])
            return carry
        lax.fori_loop(0, nunit // SEQ_GROUP, body, 0)


def _rt_log_gamma():
    return [math.log1p(-(2.0 ** (-5.0 - h))) for h in range(HEADS)]


_RT_CDEC = {L: [math.exp(L * lg) for lg in _rt_log_gamma()] for L in (8, CHUNK)}


def _chunk_call(kind, P, col_blk, col_w, sm, row0, nseq, T, states, consts, const_chunked, name, stacked=None):
    has_sm = sm
    if T % CHUNK == 0:
        L, nslab, nunit = CHUNK, nseq, 1
        grid = (1, T // L)
        rows = L
        def pmap(s, i, c):
            return ((row0 + s * T) // L + c, col_blk)
        def smap(s, i, c):
            return ((row0 + s * T) // L + c, 3)
        def omap(i, c):
            return (0, c, 0)
        sblk = nseq
    else:
        L, nslab, nunit = T, 1, 16
        grid = (nseq // nunit, 1)
        rows = L * nunit
        def pmap(s, i, c):
            return (row0 // rows + i, col_blk)
        def smap(s, i, c):
            return (row0 // rows + i, 3)
        def omap(i, c):
            return (0, i, 0)
        sblk = nunit
    in_specs = [pl.BlockSpec((rows, col_w), functools.partial(pmap, s)) for s in range(nslab)]
    args = [P] * nslab
    if has_sm:
        in_specs += [pl.BlockSpec((rows, GROUP_W), functools.partial(smap, s)) for s in range(nslab)]
        args += [P] * nslab
    st_specs, st_shapes = [], []
    for st in states:
        arr, layer = st if isinstance(st, tuple) else (st, None)
        tail = arr.shape[1:] if layer is None else arr.shape[2:]
        nd = len(tail)
        st_specs.append(pl.BlockSpec((sblk,) + tail, lambda i, c, nd=nd: (i,) + (0,) * nd))
        st_shapes.append(jax.ShapeDtypeStruct((nseq,) + tail, F32))
        if layer is None:
            in_specs.append(st_specs[-1])
        else:
            in_specs.append(pl.BlockSpec((None, sblk) + tail, lambda i, c, nd=nd, layer=layer: (layer, i) + (0,) * nd))
        args.append(arr)
    for cst, chunked in zip(consts, const_chunked):
        if chunked:
            in_specs.append(pl.BlockSpec((None,) + cst.shape[1:], lambda i, c: (c, 0, 0)))
        else:
            in_specs.append(pl.BlockSpec(cst.shape, lambda i, c, nd=cst.ndim: (0,) * nd))
        args.append(cst)
    aliases = {}
    own_layer = None
    if stacked is not None:
        depth, layer, prev = stacked
        tail = st_shapes[0].shape[1:]
        st_shapes[0] = jax.ShapeDtypeStruct((depth, nseq) + tail, F32)
        if prev is None:
            own_layer = layer
            st_specs[0] = pl.BlockSpec((depth, sblk) + tail, lambda i, c, nd=len(tail): (0, i) + (0,) * nd)
        else:
            st_specs[0] = pl.BlockSpec((None, sblk) + tail, lambda i, c, nd=len(tail): (layer, i) + (0,) * nd)
            aliases[len(args)] = 1
            in_specs.append(pl.BlockSpec(memory_space=pl.ANY))
            args.append(prev)
    out_specs = [pl.BlockSpec((nslab, rows, GROUP_W), omap)] + st_specs
    out_shape = [jax.ShapeDtypeStruct((nslab, nseq * T // nslab, GROUP_W), BF16)] + st_shapes
    outs = pl.pallas_call(
        functools.partial(_chunk_kernel, kind, nslab, nunit, L, len(states), len(consts), len(aliases), has_sm,
                          own_layer),
        grid=grid,
        in_specs=in_specs,
        out_specs=out_specs,
        out_shape=out_shape,
        input_output_aliases=aliases,
        compiler_params=_cparams(2),
        name=name,
    )(*args)
    return outs[0].reshape(nseq * T, GROUP_W), outs[1:]


def _pack_w_in(w_in):
    w_in = w_in.astype(BF16)
    z = jnp.zeros(w_in.shape[:2] + (GROUP_W - 184,), BF16)
    parts = [w_in[..., 0:1696], w_in[..., 3744:3752], w_in[..., 6824:6840], z,
             w_in[..., 1696:3744], w_in[..., 3752:5800], w_in[..., 5800:6824], w_in[..., 6840:7352]]
    return jnp.concatenate(parts, axis=-1)


def _row(v):
    return v.reshape(1, -1).astype(F32)


def _rows_at(mat, r0, rows_total):
    z = jnp.zeros((rows_total, mat.shape[1]), F32).at[r0:r0 + mat.shape[0]].set(mat)
    return z.astype(BF16)


def _ret_consts(pos0, T, L, gn):
    pos = pos0 + jnp.arange(T, dtype=F32)
    inv = 1.0 / (10000.0 ** jnp.linspace(0.0, 1.0, HEAD_D // 2, dtype=F32))
    ang = pos[:, None] * inv[None, :]
    cos, sin = jnp.cos(ang), jnp.sin(ang)
    cos2 = jnp.concatenate([cos, cos], axis=1).reshape(T // L, L, HEAD_D)
    sin2 = jnp.concatenate([-sin, sin], axis=1).reshape(T // L, L, HEAD_D)
    lg = jnp.log1p(-jnp.exp2(-5.0 - jnp.arange(HEADS, dtype=F32)))
    idx = jnp.arange(L, dtype=F32)
    rel = idx[:, None] - idx[None, :]
    dmat = jnp.where(rel >= 0, jnp.exp(jnp.maximum(rel, 0.0) * lg[:, None, None]), 0.0)
    q_dec = jnp.broadcast_to(jnp.exp((idx + 1.0) * lg[:, None])[..., None], (HEADS, L, HEAD_D))
    k_dec = jnp.broadcast_to(jnp.exp((L - 1.0 - idx) * lg[:, None])[..., None], (HEADS, L, HEAD_D))
    return [cos2, sin2, dmat, q_dec, k_dec, _row(gn)], [True, True, False, False, False, False]


def _tri(L):
    return jnp.tril(jnp.ones((L, L), F32)).astype(BF16)


def _moe_plan(ids, tm, n_tiles):
    e = ids.reshape(-1)
    n = e.shape[0]
    onehot = (e[:, None] == jnp.arange(N_EXPERTS, dtype=jnp.int32)[None, :]).astype(jnp.int32)
    rank = jnp.sum((jnp.cumsum(onehot, axis=0) - 1) * onehot, axis=1)
    counts = jnp.sum(onehot, axis=0)
    padded = ((counts + tm - 1) // tm) * tm
    ends = jnp.cumsum(padded)
    starts = ends - padded
    dest = jnp.sum(starts[None, :] * onehot, axis=1) + rank
    row_src = jnp.zeros((n_tiles * tm,), jnp.int32).at[dest].set(jnp.arange(n, dtype=jnp.int32) // 2,
                                                                 unique_indices=True)
    tile_start = jnp.arange(n_tiles, dtype=jnp.int32) * tm
    tile_expert = jnp.minimum(jnp.sum((tile_start[:, None] >= ends[None, :]).astype(jnp.int32), axis=1),
                              N_EXPERTS - 1)
    return row_src, dest.reshape(-1, 2), tile_expert, (ends[-1:] // tm).astype(jnp.int32)


def kernel(x_prompt, x_sample, state_rw_shift, state_rw_wkv, state_ml_conv, state_ml_C, state_ml_n, state_ml_m, state_rt_S, state_gl_S, w_in, rw_mu, rw_w0, rw_w2, rw_a0, rw_a2, rw_g2, rw_kk, rw_ka, rw_rk, rw_ln_w, rw_ln_b, ml_conv_w, ml_conv_b, ml_bi, ml_bf, ml_gn_w, rt_gn_w, gl_a2, gl_ab, gl_gn_w, w_out, ln1_w, ln1_b, ln2_w, ln2_b, ffn_w1, ffn_w3, ffn_w2, moe_router, moe_w1, moe_w3, moe_w2):
    Bp, Tp = x_prompt.shape[:2]
    Bs, Ts = x_sample.shape[:2]
    Mp, Ms = Bp * Tp, Bs * Ts
    M = Mp + Ms
    depth = w_in.shape[0]
    x = [x_prompt.reshape(Mp, D_MODEL), x_sample.reshape(Ms, D_MODEL)]
    xb = [v.astype(BF16) for v in x]
    w_in_p = _pack_w_in(w_in)
    w_out_b = w_out.astype(BF16)
    blockdiag = (jnp.arange(LANES)[:, None] // RW_HD) == (jnp.arange(LANES)[None, :] // RW_HD)
    g64 = blockdiag.astype(BF16)
    g64m = (blockdiag.astype(F32) / RW_HD).astype(BF16)
    pad_sh = 4 * GROUP_W - RW_COLS
    outs = {k: [[], []] for k in ("sh", "conv", "n", "m")}
    big = {k: [None, None] for k in ("wkv", "C", "rt", "gl")}

    for l in range(depth):
        P = _mm_pair(xb[0], xb[1], w_in_p, l, 3 * GROUP_W, "mm_in")

        prm = [_row(jnp.pad(rw_mu[l], (0, pad_sh))), _row(rw_w0[l]), _row(rw_a0[l]),
               _rows_at(rw_w2[l], 0, 2 * LANES), _rows_at(rw_a2[l], 32, 2 * LANES), _rows_at(rw_g2[l], 64, 2 * LANES),
               _row(rw_kk[l]), _row(rw_ka[l]), _row(rw_rk[l]), _row(rw_ln_w[l]), _row(rw_ln_b[l]), g64, g64m]
        o_rw_p, sh_p, big["wkv"][0] = _rwkv(P[0], 0, Bp, Tp, jnp.zeros((Bp, SUBLANES, 4 * GROUP_W), F32),
                                             jnp.zeros((Bp, RW_H, RW_HD, RW_HD), F32), prm,
                                             stacked=(depth, l, big["wkv"][0]))
        sh_in = jnp.pad(state_rw_shift[l], ((0, 0), (0, pad_sh)))[:, None, :]
        o_rw_s, sh_s, big["wkv"][1] = _rwkv(P[1], 0, Bs, Ts, sh_in, state_rw_wkv, prm, s0_layer=l,
                                             stacked=(depth, l, big["wkv"][1]))
        outs["sh"][0].append(sh_p[:, SUBLANES - 1, :RW_COLS])
        outs["sh"][1].append(sh_s[:, 0, :RW_COLS])

        bias_sm = jnp.zeros((LANES,), F32).at[SM_MLI - LANES:SM_MLI - LANES + HEADS].set(ml_bi[l])
        bias_sm = bias_sm.at[SM_MLF - LANES:SM_MLF - LANES + HEADS].set(ml_bf[l])

        def ml_consts(L):
            return [_tri(L), jnp.ones((L, LANES), BF16), ml_conv_w[l], _row(ml_conv_b[l]), _row(bias_sm), _row(ml_gn_w[l])]

        def ml_states(C, n, m, conv):
            b = n.shape[0]
            return [C, n[:, :, None, :], jnp.broadcast_to(m[:, :, None, None], (b, HEADS, 1, LANES)),
                    jnp.pad(conv, ((0, 0), (SUBLANES - conv.shape[1], 0), (0, 0)))]

        zp = lambda *s: jnp.zeros((Bp,) + s, F32)
        o_ml_p, st_p = _chunk_call("ml", P[0], 1, 4 * GROUP_W, True, 0, Bp, Tp,
                                   ml_states(zp(HEADS, HEAD_D, HEAD_D), zp(HEADS, HEAD_D), zp(HEADS), zp(3, 2 * GROUP_W)),
                                   ml_consts(CHUNK), [False] * 6, "mlstm_prompt", stacked=(depth, l, big["C"][0]))
        o_ml_s, st_s = _chunk_call("ml", P[1], 1, 4 * GROUP_W, True, 0, Bs, Ts,
                                   ml_states((state_ml_C, l), state_ml_n[l], state_ml_m[l], state_ml_conv[l]),
                                   ml_consts(Ts), [False] * 6, "mlstm_sample", stacked=(depth, l, big["C"][1]))
        for g, st in enumerate((st_p, st_s)):
            big["C"][g] = st[0]
            outs["n"][g].append(st[1][:, :, 0, :])
            outs["m"][g].append(st[2][:, :, 0, 0])
            outs["conv"][g].append(st[3][:, SUBLANES - 3:, :])

        c_p, ch_p = _ret_consts(0.0, Tp, CHUNK, rt_gn_w[l])
        c_s, ch_s = _ret_consts(float(PAST_LEN), Ts, Ts, rt_gn_w[l])
        o_rt_p, (big["rt"][0],) = _chunk_call("ret", P[0], 2, 4 * GROUP_W, False, 0, Bp, Tp, [zp(HEADS, HEAD_D, HEAD_D)],
                                              c_p, ch_p, "ret_prompt", stacked=(depth, l, big["rt"][0]))
        o_rt_s, (big["rt"][1],) = _chunk_call("ret", P[1], 2, 4 * GROUP_W, False, 0, Bs, Ts, [(state_rt_S, l)],
                                              c_s, ch_s, "ret_sample", stacked=(depth, l, big["rt"][1]))

        def gl_consts(L):
            return [_tri(L), _rows_at(gl_a2[l], SM_GLA - LANES, LANES), _row(gl_ab[l]), _row(gl_gn_w[l])]

        o_gl_p, (big["gl"][0],) = _chunk_call("gl", P[0], 4, 3 * GROUP_W, True, 0, Bp, Tp, [zp(HEADS, GL_DK, HEAD_D)],
                                              gl_consts(CHUNK), [False] * 4, "gla_prompt",
                                              stacked=(depth, l, big["gl"][0]))
        o_gl_s, (big["gl"][1],) = _chunk_call("gl", P[1], 4, 3 * GROUP_W, True, 0, Bs, Ts, [(state_gl_S, l)],
                                              gl_consts(Ts), [False] * 4, "gla_sample",
                                              stacked=(depth, l, big["gl"][1]))

        mixed = ((o_rw_p, o_ml_p, o_rt_p, o_gl_p), (o_rw_s, o_ml_s, o_rt_s, o_gl_s))
        for g in range(2):
            x[g], xb[g] = _mm_ln(list(mixed[g]), w_out_b, l, x[g], ln1_w[l], ln1_b[l])

        if l % 2 == 0:
            w1, w3, w2 = (w[l // 2][None].astype(BF16) for w in (ffn_w1, ffn_w3, ffn_w2))
            for g in range(2):
                rows = x[g].shape[0]
                tm = _pick(rows, (512, 256, 128))
                y = _ffn(xb[g], jnp.zeros((rows // tm,), jnp.int32), jnp.full((1,), rows // tm, jnp.int32), w1, w3, w2, tm)
                x[g], xb[g] = _add_ln(x[g], [y], None, ln2_w[l], ln2_b[l])
        else:
            tm = _pick(M, (512, 256, 128))
            n_tiles = 2 * M // tm + N_EXPERTS
            router_pad = jnp.pad(moe_router[l // 2], ((0, 0), (0, LANES - N_EXPERTS)))
            routed = [_router(v, router_pad) for v in x]
            gates = [r[0] for r in routed]
            ids = jnp.concatenate([r[1][:, :2] for r in routed], axis=0)
            row_src, dest, tile_expert, n_used = _moe_plan(ids, tm, n_tiles)
            ysorted = _ffn(jnp.concatenate(x, axis=0), tile_expert, n_used, moe_w1[l // 2].astype(BF16),
                           moe_w3[l // 2].astype(BF16), moe_w2[l // 2].astype(BF16), tm, row_src=row_src)
            for g, (r0, r1) in enumerate(((0, Mp), (Mp, M))):
                ys = [jnp.take(ysorted, dest[r0:r1, k], axis=0, mode="clip") for k in range(2)]
                x[g], xb[g] = _add_ln(x[g], ys, [gates[g][:, 0:1], gates[g][:, 1:2]], ln2_w[l], ln2_b[l])

    res = [x[0].reshape(Bp, Tp, D_MODEL), x[1].reshape(Bs, Ts, D_MODEL)]
    for key in ("sh", "wkv", "conv", "C", "n", "m", "rt", "gl"):
        res += list(big[key]) if key in big else [jnp.stack(outs[key][0]), jnp.stack(outs[key][1])]
    return tuple(res)
```

```python
import functools
import math

import jax
import jax.numpy as jnp
from jax import lax
from jax.experimental import pallas as pl
from jax.experimental.pallas import tpu as pltpu

F32 = jnp.float32
BF16 = jnp.bfloat16

D_MODEL = 2048
GROUP_W = 512
RW_COLS = 1696
RW_H, RW_HD = 8, 64
HEADS, HEAD_D = 4, 128
GL_DK = 64
CHUNK = 64
ALPHA = 4.0 ** 0.25
LN_EPS = 1e-5
GN_EPS = 1e-5
RW_GN_EPS = 64e-5
GL_TAU = 16.0
PAST_LEN = 16384
N_EXPERTS = 8

LANES = 128
SUBLANES = 8
VMEM_LIMIT = 48 * 1024 * 1024

N_PACKED = 15 * GROUP_W
SM_MLI, SM_MLF, SM_GLA = 160, 164, 168


def _nn(a, b):
    return jnp.dot(a, b, preferred_element_type=F32)


def _nt(a, b):
    return lax.dot_general(a, b, (((1,), (1,)), ((), ())), preferred_element_type=F32)


def _tn(a, b):
    return lax.dot_general(a, b, (((0,), (0,)), ((), ())), preferred_element_type=F32)


def _bf(x):
    return x.astype(BF16)


def _split3(x):
    hi = x.astype(BF16)
    r1 = x - hi.astype(F32)
    mid = r1.astype(BF16)
    lo = (r1 - mid.astype(F32)).astype(BF16)
    return hi, mid, lo


def _dot3(x, m, kind="nn"):
    f = {"nn": _nn, "nt": _nt, "tn": _tn}[kind]
    hi, mid, lo = _split3(x)
    return f(hi, m) + f(mid, m) + f(lo, m)


def _dot3r(m, x, kind="nn"):
    f = {"nn": _nn, "nt": _nt, "tn": _tn}[kind]
    hi, mid, lo = _split3(x)
    return f(m, hi) + f(m, mid) + f(m, lo)


def _sigmoid(x):
    return 1.0 / (1.0 + jnp.exp(-x))


def _softplus(x):
    return jnp.maximum(x, 0.0) + jnp.log1p(jnp.exp(-jnp.abs(x)))


def _log_sigmoid(x):
    return -_softplus(-x)


def _silu(x):
    return x * _sigmoid(x)


def _iota(shape, axis):
    return lax.broadcasted_iota(jnp.int32, shape, axis)


def _shift_rows(x, prev8, j):
    rows = x.shape[0]
    xr = pltpu.roll(x, j, 0)
    pr = pltpu.roll(prev8, j, 0)
    first = jnp.where(_iota((SUBLANES, x.shape[1]), 0) < j, pr, xr[0:SUBLANES])
    if rows == SUBLANES:
        return first
    return jnp.concatenate([first, xr[SUBLANES:]], axis=0)


def _pick(n, cands):
    return next(c for c in cands if n % c == 0)


def _cparams(n_axes):
    return pltpu.CompilerParams(dimension_semantics=("arbitrary",) * n_axes, vmem_limit_bytes=VMEM_LIMIT)


def _mm_kernel(n_in, *refs):
    xs, ws, o_ref = refs[:n_in], refs[n_in:2 * n_in], refs[2 * n_in]
    acc = _nn(xs[0][...], ws[0][...])
    for x_ref, w_ref in zip(xs[1:], ws[1:]):
        acc = acc + _nn(x_ref[...], w_ref[...])
    o_ref[...] = acc


def _mm(xs, w, layer, tn, name):
    M, N = xs[0].shape[0], w.shape[2]
    n_in = len(xs)
    tm = _pick(M, (1024, 512, 256, 128))
    in_specs = ([pl.BlockSpec((tm, x.shape[1]), lambda j, i: (i, 0)) for x in xs]
                + [pl.BlockSpec((None, x.shape[1], tn), functools.partial(lambda g, j, i: (layer, g, j), g))
                   for g, x in enumerate(xs)])
    return pl.pallas_call(
        functools.partial(_mm_kernel, n_in),
        grid=(N // tn, M // tm),
        in_specs=in_specs,
        out_specs=pl.BlockSpec((tm, tn), lambda j, i: (i, j)),
        out_shape=jax.ShapeDtypeStruct((M, N), F32),
        compiler_params=_cparams(2),
        name=name,
    )(*xs, *([w] * n_in))


def _mm_pair_kernel(n_a, xa_ref, xb_ref, w_ref, oa_ref, ob_ref):
    i = pl.program_id(1)

    @pl.when(i < n_a)
    def _():
        oa_ref[...] = _nn(xa_ref[...], w_ref[...])

    @pl.when(i >= n_a)
    def _():
        ob_ref[...] = _nn(xb_ref[...], w_ref[...])


def _mm_pair(xa, xb, w, layer, tn, name):
    K, N = xa.shape[1], w.shape[2]
    tm = _pick(math.gcd(xa.shape[0], xb.shape[0]), (1024, 512, 256, 128))
    n_a, n_b = xa.shape[0] // tm, xb.shape[0] // tm

    def ia(i):
        return jnp.minimum(i, n_a - 1)

    def ib(i):
        return jnp.maximum(i - n_a, 0)

    return pl.pallas_call(
        functools.partial(_mm_pair_kernel, n_a),
        grid=(N // tn, n_a + n_b),
        in_specs=[pl.BlockSpec((tm, K), lambda j, i: (ia(i), 0)),
                  pl.BlockSpec((tm, K), lambda j, i: (ib(i), 0)),
                  pl.BlockSpec((None, K, tn), lambda j, i: (layer, 0, j))],
        out_specs=[pl.BlockSpec((tm, tn), lambda j, i: (ia(i), j)),
                   pl.BlockSpec((tm, tn), lambda j, i: (ib(i), j))],
        out_shape=[jax.ShapeDtypeStruct((xa.shape[0], N), F32), jax.ShapeDtypeStruct((xb.shape[0], N), F32)],
        compiler_params=_cparams(2),
        name=name,
    )(xa, xb, w)


def _mm_ln_kernel(n_in, *refs):
    xs, ws = refs[:n_in], refs[n_in:2 * n_in]
    r_ref, w_ref, b_ref, o_ref, ob_ref = refs[2 * n_in:]
    acc = _nn(xs[0][...], ws[0][...])
    for x_ref, wt_ref in zip(xs[1:], ws[1:]):
        acc = acc + _nn(x_ref[...], wt_ref[...])
    out = _layer_norm(ALPHA * r_ref[...] + acc, w_ref[...], b_ref[...])
    o_ref[...] = out
    ob_ref[...] = out.astype(BF16)


def _mm_ln(xs, w, layer, resid, ln_w, ln_b):
    M, N = resid.shape
    n_in = len(xs)
    tm = _pick(M, (512, 256, 128))
    row = pl.BlockSpec((tm, N), lambda i: (i, 0))
    in_specs = ([pl.BlockSpec((tm, x.shape[1]), lambda i: (i, 0)) for x in xs]
                + [pl.BlockSpec((None, x.shape[1], N), functools.partial(lambda g, i: (layer, g, 0), g))
                   for g, x in enumerate(xs)]
                + [row] + [pl.BlockSpec((1, N), lambda i: (0, 0))] * 2)
    return pl.pallas_call(
        functools.partial(_mm_ln_kernel, n_in),
        grid=(M // tm,),
        in_specs=in_specs,
        out_specs=[row, row],
        out_shape=[jax.ShapeDtypeStruct((M, N), F32), jax.ShapeDtypeStruct((M, N), BF16)],
        compiler_params=_cparams(1),
        name="mm_out_ln",
    )(*xs, *([w] * n_in), resid, ln_w.reshape(1, N), ln_b.reshape(1, N))


def _layer_norm(z, w, b):
    mu = jnp.mean(z, axis=-1, keepdims=True)
    zc = z - mu
    var = jnp.mean(zc * zc, axis=-1, keepdims=True)
    return zc * lax.rsqrt(var + LN_EPS) * w + b


def _add_ln_kernel(n_y, gated, x_ref, *refs):
    ys = refs[:n_y]
    gs = refs[n_y:2 * n_y] if gated else ()
    w_ref, b_ref, o_ref, ob_ref = refs[-4:]
    z = ALPHA * x_ref[...]
    for i, y_ref in enumerate(ys):
        y = y_ref[...]
        if gated:
            y = gs[i][...] * y
        z = z + y
    out = _layer_norm(z, w_ref[...], b_ref[...])
    o_ref[...] = out
    ob_ref[...] = out.astype(BF16)


def _add_ln(x, ys, gates, w, b):
    M = x.shape[0]
    tm = _pick(M, (512, 256, 128))
    gated = gates is not None
    row = pl.BlockSpec((tm, D_MODEL), lambda i: (i, 0))
    in_specs = [row] + [row] * len(ys)
    args = [x] + list(ys)
    if gated:
        in_specs += [pl.BlockSpec((tm, 1), lambda i: (i, 0))] * len(ys)
        args += list(gates)
    in_specs += [pl.BlockSpec((1, D_MODEL), lambda i: (0, 0))] * 2
    args += [w.reshape(1, D_MODEL), b.reshape(1, D_MODEL)]
    return pl.pallas_call(
        functools.partial(_add_ln_kernel, len(ys), gated),
        grid=(M // tm,),
        in_specs=in_specs,
        out_specs=[row, row],
        out_shape=[jax.ShapeDtypeStruct((M, D_MODEL), F32), jax.ShapeDtypeStruct((M, D_MODEL), BF16)],
        compiler_params=_cparams(1),
        name="add_ln",
    )(*args)


def _ffn_up_kernel(te_ref, nu_ref, x_ref, w1_ref, w3_ref, h_ref):
    @pl.when(pl.program_id(0) < nu_ref[0])
    def _():
        x = x_ref[...]
        h_ref[...] = (_silu(_nn(x, w1_ref[...])) * _nn(x, w3_ref[...])).astype(BF16)

    @pl.when(pl.program_id(0) >= nu_ref[0])
    def _():
        h_ref[...] = jnp.zeros(h_ref.shape, BF16)


def _ffn_up_gather_kernel(te_ref, nu_ref, src_ref, x_hbm, w1_ref, w3_ref, h_ref, xbuf, xb16, sem):
    i, f = pl.program_id(0), pl.program_id(1)
    tm = xbuf.shape[0]
    n_used = nu_ref[0]

    def start_tile(tile):
        def body(r2, c):
            for prio in range(2):
                r = 2 * r2 + prio
                src = x_hbm.at[pl.ds(src_ref[tile * tm + r], 1)]
                pltpu.make_async_copy(src, xbuf.at[pl.ds(r, 1)], sem.at[0]).start(priority=prio)
            return c
        lax.fori_loop(0, tm // 2, body, 0, unroll=4)

    @pl.when((f == 0) & (i == 0) & (n_used > 0))
    def _():
        start_tile(0)

    @pl.when((f == 0) & (i < n_used))
    def _():
        pltpu.make_async_copy(x_hbm.at[pl.ds(0, tm)], xbuf, sem.at[0]).wait()
        xb16[...] = xbuf[...].astype(BF16)

        @pl.when(i + 1 < n_used)
        def _():
            start_tile(i + 1)

    @pl.when(i < n_used)
    def _():
        x = xb16[...]
        h_ref[...] = (_silu(_nn(x, w1_ref[...])) * _nn(x, w3_ref[...])).astype(BF16)

    @pl.when(i >= n_used)
    def _():
        h_ref[...] = jnp.zeros(h_ref.shape, BF16)


def _ffn_down_kernel(te_ref, nu_ref, h_ref, w2_ref, o_ref):
    @pl.when(pl.program_id(0) < nu_ref[0])
    def _():
        o_ref[...] = _nn(h_ref[...], w2_ref[...])

    @pl.when(pl.program_id(0) >= nu_ref[0])
    def _():
        o_ref[...] = jnp.zeros(o_ref.shape, F32)


FFN_TF = 1408
FFN_TN = 512
FFN_DOWN_TILE_ELEMS = 2816 * 2048


def _ffn(xs, tile_expert, n_used, w1, w3, w2, tm, row_src=None):
    R = xs.shape[0] if row_src is None else row_src.shape[0]
    F = w1.shape[2]
    tn = FFN_TN * (FFN_DOWN_TILE_ELEMS // (F * FFN_TN))
    nf, nn = F // FFN_TF, D_MODEL // tn

    def last(i, j, nu, n):
        return jnp.where(i < nu[0], j, n - 1)

    if row_src is None:
        up_spec = pltpu.PrefetchScalarGridSpec(
            num_scalar_prefetch=2,
            grid=(R // tm, nf),
            in_specs=[
                pl.BlockSpec((tm, D_MODEL), lambda i, f, te, nu: (i, 0)),
                pl.BlockSpec((None, D_MODEL, FFN_TF), lambda i, f, te, nu: (te[i], 0, last(i, f, nu, nf))),
                pl.BlockSpec((None, D_MODEL, FFN_TF), lambda i, f, te, nu: (te[i], 0, last(i, f, nu, nf))),
            ],
            out_specs=pl.BlockSpec((tm, FFN_TF), lambda i, f, te, nu: (i, f)),
        )
        h = pl.pallas_call(
            _ffn_up_kernel,
            grid_spec=up_spec,
            out_shape=jax.ShapeDtypeStruct((R, F), BF16),
            compiler_params=_cparams(2),
            name="ffn_up",
        )(tile_expert, n_used, xs, w1, w3)
    else:
        up_spec = pltpu.PrefetchScalarGridSpec(
            num_scalar_prefetch=3,
            grid=(R // tm, nf),
            in_specs=[
                pl.BlockSpec(memory_space=pl.ANY),
                pl.BlockSpec((None, D_MODEL, FFN_TF), lambda i, f, te, nu, src: (te[i], 0, last(i, f, nu, nf))),
                pl.BlockSpec((None, D_MODEL, FFN_TF), lambda i, f, te, nu, src: (te[i], 0, last(i, f, nu, nf))),
            ],
            out_specs=pl.BlockSpec((tm, FFN_TF), lambda i, f, te, nu, src: (i, f)),
            scratch_shapes=[pltpu.VMEM((tm, D_MODEL), F32), pltpu.VMEM((tm, D_MODEL), BF16),
                            pltpu.SemaphoreType.DMA((1,))],
        )
        h = pl.pallas_call(
            _ffn_up_gather_kernel,
            grid_spec=up_spec,
            out_shape=jax.ShapeDtypeStruct((R, F), BF16),
            compiler_params=_cparams(2),
            name="ffn_up_gather",
        )(tile_expert, n_used, row_src, xs, w1, w3)
    down_spec = pltpu.PrefetchScalarGridSpec(
        num_scalar_prefetch=2,
        grid=(R // tm, nn),
        in_specs=[
            pl.BlockSpec((tm, F), lambda i, n, te, nu: (i, 0)),
            pl.BlockSpec((None, F, tn), lambda i, n, te, nu: (te[i], 0, last(i, n, nu, nn))),
        ],
        out_specs=pl.BlockSpec((tm, tn), lambda i, n, te, nu: (i, n)),
    )
    return pl.pallas_call(
        _ffn_down_kernel,
        grid_spec=down_spec,
        out_shape=jax.ShapeDtypeStruct((R, D_MODEL), F32),
        compiler_params=_cparams(2),
        name="ffn_down",
    )(tile_expert, n_used, h, w2)


def _router_kernel(x_ref, r_ref, g_ref, i_ref):
    xh, xm, xl = _split3(x_ref[...])
    rh, rm, rl = _split3(r_ref[...])
    logits = (_nn(xh, rh) + _nn(xh, rm) + _nn(xm, rh)) + (_nn(xh, rl) + _nn(xl, rh) + _nn(xm, rm))
    shape = logits.shape
    lane = _iota(shape, 1).astype(F32)
    neg = jnp.float32(-jnp.inf)
    l1 = jnp.where(lane < N_EXPERTS, logits, neg)
    m1 = jnp.max(l1, axis=1, keepdims=True)
    i1 = jnp.min(jnp.where(l1 == m1, lane, float(LANES)), axis=1, keepdims=True)
    l2 = jnp.where(lane == i1, neg, l1)
    m2 = jnp.max(l2, axis=1, keepdims=True)
    i2 = jnp.min(jnp.where(l2 == m2, lane, float(LANES)), axis=1, keepdims=True)
    e = jnp.exp(m2 - m1)
    g1 = 1.0 / (1.0 + e)
    g2 = e / (1.0 + e)
    g_ref[...] = jnp.where(lane == 0.0, g1, jnp.where(lane == 1.0, g2, 0.0))
    i_ref[...] = jnp.where(lane == 0.0, i1, jnp.where(lane == 1.0, i2, 0.0)).astype(jnp.int32)


def _router(x, router_pad):
    M = x.shape[0]
    tm = _pick(M, (512, 256, 128))
    return pl.pallas_call(
        _router_kernel,
        grid=(M // tm,),
        in_specs=[pl.BlockSpec((tm, D_MODEL), lambda i: (i, 0)),
                  pl.BlockSpec((D_MODEL, LANES), lambda i: (0, 0))],
        out_specs=[pl.BlockSpec((tm, LANES), lambda i: (i, 0))] * 2,
        out_shape=[jax.ShapeDtypeStruct((M, LANES), F32), jax.ShapeDtypeStruct((M, LANES), jnp.int32)],
        compiler_params=_cparams(1),
        name="router",
    )(x, router_pad)


def _rw_prep(pm, prm):
    (w0, a0, w2p, a2p, g2p, k_k, k_a, _, _, _, g64, _) = prm
    r = pm[:, 0:512]
    k = pm[:, 512:1024]
    v = pm[:, 1024:1536]
    sm = pm[:, 1536:1792]
    logw = -_softplus(-(w0 + _nn(_bf(jnp.tanh(sm)), w2p))) - 0.5
    lw = -jnp.exp(logw)
    a = _sigmoid(a0 + _nn(_bf(sm), a2p))
    g = _nn(_bf(_sigmoid(sm)), g2p)
    kk = k * k_k
    ss = jnp.concatenate([_dot3((kk * kk)[:, LANES * p:LANES * (p + 1)], g64) for p in range(4)], axis=1)
    kk = kk / jnp.maximum(jnp.sqrt(ss), 1e-12)
    kmod = k * (1.0 + (a - 1.0) * k_a)
    return r, lw, kmod, kk, kk * a, v, g


def _rw_post(y, r, kmod, v, g, prm):
    (_, _, _, _, _, _, _, r_k, ln_w, ln_b, g64, g64m) = prm

    def per_head(x, m):
        return jnp.concatenate([_dot3(x[:, LANES * p:LANES * (p + 1)], m) for p in range(4)], axis=1)

    yc = y - per_head(y, g64m)
    var = per_head(yc * yc, g64m)
    yn = yc * lax.rsqrt(var + RW_GN_EPS) * ln_w + ln_b
    bonus = per_head(r * kmod * r_k, g64) * v
    return (yn + bonus) * g


def _rw_chunks(L, seqs, scr, s_ref):
    q_s, rr_s, kt_s, bt_s, kp_s, bp_s, v_s, wl_s, y_s = scr
    units = [(si, r0, h) for (si, r0) in seqs for h in range(RW_H)]

    def fetch(ref, r0, h):
        return ref[pl.ds(r0, L), RW_HD * h:RW_HD * (h + 1)]

    row = _iota((2 * L, 2 * L), 0)
    col = _iota((2 * L, 2 * L), 1)
    colm = jnp.where(col >= L, col - L, col)
    mask4 = colm <= jnp.where(row < L, row - 1, row - L)
    qf = [fetch(q_s, r0, h) for (_, r0, h) in units]
    rf = [fetch(rr_s, r0, h) for (_, r0, h) in units]
    vb = [_bf(fetch(v_s, r0, h)) for (_, r0, h) in units]
    g4 = [jnp.where(mask4, _nt(_bf(jnp.concatenate([qf[i], rf[i]], axis=0)),
                               _bf(jnp.concatenate([fetch(kt_s, r0, h), fetch(bt_s, r0, h)], axis=0))), 0.0)
          for i, (_, r0, h) in enumerate(units)]
    av = [_nn(_bf(g4[i][:, :L]), vb[i]) for i in range(len(units))]
    x = [jnp.concatenate([qf[i], av[i][:L]], axis=1) for i in range(len(units))]
    pw = [g4[i][:L, L:] for i in range(len(units))]
    x = [x[i] - _nn(_bf(pw[i]), _bf(x[i])) for i in range(len(units))]
    for _ in range(int(math.log2(L)) - 1):
        pw = [_nn(_bf(m), _bf(m)) for m in pw]
        x = [x[i] + _nn(_bf(pw[i]), _bf(x[i])) for i in range(len(units))]
    ry = [jnp.concatenate([rf[i], av[i][L:]], axis=1) - _nn(_bf(g4[i][L:, L:]), _bf(x[i])) for i in range(len(units))]
    gh = [_tn(_bf(x[i]), _bf(fetch(bp_s, r0, h))) for i, (_, r0, h) in enumerate(units)]
    hk = [_tn(vb[i], _bf(fetch(kp_s, r0, h))) for i, (_, r0, h) in enumerate(units)]
    ys = []
    for i, (si, r0, h) in enumerate(units):
        S = s_ref[si, h]
        Sb = _bf(S)
        ys.append(_nt(_bf(ry[i][:, :RW_HD]), Sb) + ry[i][:, RW_HD:])
        w_last = fetch(wl_s, r0, h)[0:1, :]
        s_ref[si, h] = S * w_last - _nn(Sb, _bf(gh[i][:RW_HD])) + hk[i] - gh[i][RW_HD:]
    for i in range(0, len(units), 2):
        _, r0, h = units[i]
        y_s[pl.ds(r0, L), RW_HD * h:RW_HD * (h + 2)] = jnp.concatenate([ys[i], ys[i + 1]], axis=1)


N_RW_PRM = 13


def _rwkv_kernel(is_prompt, nsq, L, grp, n_alias, own_layer, *refs):
    n_p = nsq if is_prompt else 1
    p_refs = refs[:n_p]
    sh0_ref, s0_ref = refs[n_p:n_p + 2]
    pos = n_p + 2
    prm_refs = refs[pos:pos + N_RW_PRM]
    pos += N_RW_PRM
    tri_ref = refs[pos]
    pos += 1 + n_alias
    o_ref, sh_out_ref, s_out_ref = refs[pos:pos + 3]
    scr = refs[pos + 3:]
    g_s, r_s, k_s = scr[9:12]
    mu = prm_refs[0][...]
    prm = tuple(r[...] for r in prm_refs[1:])

    stacked_ref = s_out_ref if own_layer is not None else None
    if stacked_ref is not None:
        s_out_ref = stacked_ref.at[own_layer]

    def zero_other_layers():
        if stacked_ref is not None:
            for d in range(stacked_ref.shape[0]):
                if d != own_layer:
                    stacked_ref[d] = jnp.zeros(stacked_ref.shape[1:], F32)

    if is_prompt:
        prev_scr = scr[12]

        @pl.when(pl.program_id(0) == 0)
        def _():
            s_out_ref[...] = s0_ref[...]
            prev_scr[...] = sh0_ref[...]
            zero_other_layers()

        ps, pps = [], []
        for s in range(nsq):
            ps.append(p_refs[s][...])
            pps.append(_shift_rows(ps[s], prev_scr[s], 1))
            prev_scr[s] = ps[s][L - SUBLANES:L]
        p = jnp.concatenate(ps, axis=0)
        pp = jnp.concatenate(pps, axis=0)
        sh_out_ref[...] = prev_scr[...]
    else:
        p = p_refs[0][...]
        p3 = p.reshape(nsq, L, p.shape[1])
        pp = jnp.where(_iota(p3.shape, 1) == 0, sh0_ref[...], pltpu.roll(p3, 1, 1)).reshape(p.shape)
        sh_out_ref[...] = p3[:, L - 1:L, :]
        s_out_ref[...] = s0_ref[...]
        zero_other_layers()

    r, lw, kmod, kk, bb, v, g = _rw_prep(p + (pp - p) * mu, prm)
    cum = _dot3r(tri_ref[...], lw)
    cum3 = cum.reshape(nsq, L, cum.shape[1])
    tot = jnp.broadcast_to(cum3[:, L - 1:L, :], cum3.shape).reshape(cum.shape)
    e_neg = jnp.exp(-cum)
    e_rem = jnp.exp(tot - cum)
    q_s, rr_s, kt_s, bt_s, kp_s, bp_s, v_s, wl_s, y_s = scr[:9]
    q_s[...] = kk * jnp.exp(cum - lw)
    rr_s[...] = r * jnp.exp(cum)
    kt_s[...] = kmod * e_neg
    bt_s[...] = bb * e_neg
    kp_s[...] = kmod * e_rem
    bp_s[...] = bb * e_rem
    v_s[...] = v
    wl_s[...] = jnp.exp(tot)
    g_s[...], r_s[...], k_s[...] = g, r, kmod

    if is_prompt:
        _rw_chunks(L, [(s, s * L) for s in range(nsq)], scr[:9], s_out_ref)
    else:
        def body(i, c):
            _rw_chunks(L, [(i * grp + j, pl.multiple_of((i * grp + j) * L, SUBLANES)) for j in range(grp)],
                       scr[:9], s_out_ref)
            return c
        lax.fori_loop(0, nsq // grp, body, 0)

    out = _rw_post(y_s[...], r_s[...], k_s[...], v_s[...], g_s[...], prm).astype(BF16)
    if is_prompt:
        for s in range(nsq):
            o_ref[s] = out[s * L:(s + 1) * L]
    else:
        o_ref[...] = out


def _rwkv(P, row0, B, T, sh0, s0, prm_list, s0_layer=None, stacked=None):
    is_prompt = T % CHUNK == 0
    wide = 4 * GROUP_W
    s_shape = (B, RW_H, RW_HD, RW_HD)
    if is_prompt:
        L, nsq, grid = CHUNK, B, (T // CHUNK,)
        rows = nsq * L
        in_specs = [pl.BlockSpec((L, wide), functools.partial(lambda s, c: ((row0 + s * T) // L + c, 0), s))
                    for s in range(nsq)]
        in_specs += [pl.BlockSpec(sh0.shape, lambda c: (0, 0, 0)), pl.BlockSpec(s0.shape, lambda c: (0, 0, 0, 0))]
        out_specs = [pl.BlockSpec((nsq, L, GROUP_W), lambda c: (0, c, 0)),
                     pl.BlockSpec(sh0.shape, lambda c: (0, 0, 0)), pl.BlockSpec(s0.shape, lambda c: (0, 0, 0, 0))]
        out_shape = [jax.ShapeDtypeStruct((nsq, T, GROUP_W), BF16)]
        args = [P] * nsq
    else:
        L, nsq, grid = T, 16, (B // 16,)
        rows = nsq * L
        s_spec = (pl.BlockSpec((nsq, RW_H, RW_HD, RW_HD), lambda i: (i, 0, 0, 0)) if s0_layer is None else
                  pl.BlockSpec((None, nsq, RW_H, RW_HD, RW_HD), lambda i: (s0_layer, i, 0, 0, 0)))
        in_specs = [pl.BlockSpec((rows, wide), lambda i: (row0 // rows + i, 0)),
                    pl.BlockSpec((nsq, 1, wide), lambda i: (i, 0, 0)), s_spec]
        out_specs = [pl.BlockSpec((rows, GROUP_W), lambda i: (i, 0)),
                     pl.BlockSpec((nsq, 1, wide), lambda i: (i, 0, 0)),
                     pl.BlockSpec((nsq, RW_H, RW_HD, RW_HD), lambda i: (i, 0, 0, 0))]
        out_shape = [jax.ShapeDtypeStruct((B * T, GROUP_W), BF16)]
        args = [P]
    seq_of_row = jnp.arange(rows) // L
    same = seq_of_row[:, None] == seq_of_row[None, :]
    tri = (same & (jnp.arange(rows)[:, None] >= jnp.arange(rows)[None, :])).astype(BF16)
    consts = list(prm_list) + [tri]
    in_specs += [pl.BlockSpec(x.shape, lambda *a: (0, 0)) for x in consts]
    out_shape += [jax.ShapeDtypeStruct(sh0.shape, F32), jax.ShapeDtypeStruct(s_shape, F32)]
    args += [sh0, s0] + consts
    aliases = {}
    own_layer = None
    if stacked is not None:
        depth, layer, prev = stacked
        blk = tuple(out_specs[2].block_shape)
        out_shape[2] = jax.ShapeDtypeStruct((depth,) + s_shape, F32)
        if prev is None:
            own_layer = layer
            out_specs[2] = pl.BlockSpec((depth,) + blk, (lambda c: (0, 0, 0, 0, 0)) if is_prompt
                                        else (lambda i: (0, i, 0, 0, 0)))
        else:
            out_specs[2] = pl.BlockSpec((None,) + blk, (lambda c: (layer, 0, 0, 0, 0)) if is_prompt
                                        else (lambda i: (layer, i, 0, 0, 0)))
            aliases[len(args)] = 2
            in_specs.append(pl.BlockSpec(memory_space=pl.ANY))
            args.append(prev)
    scratch = [pltpu.VMEM((rows, GROUP_W), F32)] * 12
    if is_prompt:
        scratch.append(pltpu.VMEM(sh0.shape, F32))
    outs = pl.pallas_call(
        functools.partial(_rwkv_kernel, is_prompt, nsq, L, RW_SEQ_GROUP, len(aliases), own_layer),
        grid=grid,
        in_specs=in_specs,
        out_specs=out_specs,
        out_shape=out_shape,
        input_output_aliases=aliases,
        scratch_shapes=scratch,
        compiler_params=_cparams(1),
        name="rwkv_prompt" if is_prompt else "rwkv_sample",
    )(*args)
    return outs[0].reshape(B * T, GROUP_W), outs[1], outs[2]


SEQ_GROUP = 4
RW_SEQ_GROUP = 8
GL_SAFE_SPREAD = 40.0


def _causal(L):
    return _iota((L, L), 0) >= _iota((L, L), 1)


def _heads(units):
    return [(u, h) for u in range(len(units)) for h in range(HEADS)]


def _hs(h):
    return slice(h * HEAD_D, (h + 1) * HEAD_D)


def _ret_units(L, units, S_ref, consts):
    cosv, sinv, dm_ref, qd_ref, kd_ref, gn, c_dec = consts
    uh = _heads(units)

    def rot(x):
        return x * cosv + pltpu.roll(x, HEAD_D // 2, 1) * sinv

    qr = [rot(units[u][0](h * HEAD_D, HEAD_D)) for u, h in uh]
    kr = [rot(units[u][0](GROUP_W + h * HEAD_D, HEAD_D)) * (HEAD_D ** -0.5) for u, h in uh]
    vb = [_bf(units[u][0](2 * GROUP_W + h * HEAD_D, HEAD_D)) for u, h in uh]
    S = [S_ref[units[u][3], h] for u, h in uh]
    A = [_nt(_bf(qr[i]), _bf(kr[i])) * dm_ref[h] for i, (u, h) in enumerate(uh)]
    o = [_nn(_bf(A[i]), vb[i]) + _nn(_bf(qr[i] * qd_ref[h]), _bf(S[i])) for i, (u, h) in enumerate(uh)]
    for i, (u, h) in enumerate(uh):
        S_ref[units[u][3], h] = c_dec[h] * S[i] + _tn(_bf(kr[i] * kd_ref[h]), vb[i])
    for i, (u, h) in enumerate(uh):
        oc = o[i] - jnp.mean(o[i], axis=1, keepdims=True)
        on = oc * lax.rsqrt(jnp.mean(oc * oc, axis=1, keepdims=True) + GN_EPS) * gn[:, _hs(h)]
        g = units[u][0](3 * GROUP_W + h * HEAD_D, HEAD_D)
        units[u][2](h * HEAD_D, _bf(_silu(g) * on))


def _ml_units(L, units, st_refs, consts):
    C_ref, n_ref, m_ref, cv_ref = st_refs
    ltri, ones_l, conv_w, conv_b, bias_sm, gn = consts
    uh = _heads(units)
    conv, xs, bc_all = [], [], []
    for get, get_sm, _, sidx in units:
        x = get(0, 2 * GROUP_W)
        prev8 = cv_ref[sidx]
        c = (conv_b + _shift_rows(x, prev8, 3) * conv_w[0:1] + _shift_rows(x, prev8, 2) * conv_w[1:2]
             + _shift_rows(x, prev8, 1) * conv_w[2:3] + x * conv_w[3:4])
        cv_ref[sidx] = x[L - SUBLANES:L]
        conv.append(_silu(c))
        xs.append(get_sm(LANES, LANES) + bias_sm)
        bc_all.append(_dot3r(ltri, _log_sigmoid(xs[-1])))
    causal = _causal(L)
    lane0 = _iota((L, LANES), 1) == 0
    q = [conv[u][:, _hs(h)] for u, h in uh]
    k = [conv[u][:, GROUP_W + h * HEAD_D:GROUP_W + (h + 1) * HEAD_D] * (HEAD_D ** -0.5) for u, h in uh]
    vb = [_bf(units[u][0](2 * GROUP_W + h * HEAD_D, HEAD_D)) for u, h in uh]
    ig = [xs[u][:, SM_MLI - LANES + h:SM_MLI - LANES + h + 1] for u, h in uh]
    bc = [bc_all[u][:, SM_MLF - LANES + h:SM_MLF - LANES + h + 1] for u, h in uh]
    m_prev = [m_ref[units[u][3], h][:, 0:1] for u, h in uh]
    row = [_dot3r(ones_l, jnp.where(lane0, ig[i] - bc[i], 0.0), "nt") for i in range(len(uh))]
    dlog = [jnp.where(causal, bc[i] + row[i], -jnp.inf) for i in range(len(uh))]
    inter = [bc[i] + m_prev[i] for i in range(len(uh))]
    m_t = [jnp.maximum(inter[i], jnp.max(dlog[i], axis=1, keepdims=True)) for i in range(len(uh))]
    dmat = [jnp.exp(dlog[i] - m_t[i]) for i in range(len(uh))]
    s_in = [jnp.exp(inter[i] - m_t[i]) for i in range(len(uh))]
    C = [C_ref[units[u][3], h] for u, h in uh]
    n = [n_ref[units[u][3], h] for u, h in uh]
    qb = [_bf(x) for x in q]
    A = [_nt(qb[i], _bf(k[i])) * dmat[i] for i in range(len(uh))]
    num = [_nn(_bf(A[i]), vb[i]) + s_in[i] * _nt(qb[i], _bf(C[i])) for i in range(len(uh))]
    den = [jnp.sum(A[i], axis=1, keepdims=True) + s_in[i] * jnp.sum(q[i] * n[i], axis=1, keepdims=True)
           for i in range(len(uh))]
    hh = [num[i] / jnp.maximum(jnp.abs(den[i]), jnp.exp(-m_t[i])) for i in range(len(uh))]
    for i, (u, h) in enumerate(uh):
        sidx = units[u][3]
        m_new = m_t[i][L - 1:L, :]
        b_last = bc[i][L - 1:L, :]
        carry = jnp.exp(b_last + m_prev[i] - m_new)
        wk = jnp.exp(b_last - bc[i] + ig[i] - m_new) * k[i]
        C_ref[sidx, h] = carry * C[i] + _tn(vb[i], _bf(wk))
        n_ref[sidx, h] = carry * n[i] + jnp.sum(wk, axis=0, keepdims=True)
        m_ref[sidx, h] = jnp.broadcast_to(m_new, (1, LANES))
    for i, (u, h) in enumerate(uh):
        hc = hh[i] - jnp.mean(hh[i], axis=1, keepdims=True)
        hn = hc * lax.rsqrt(jnp.mean(hc * hc, axis=1, keepdims=True) + GN_EPS) * gn[:, _hs(h)]
        o_pre = units[u][0](3 * GROUP_W + h * HEAD_D, HEAD_D)
        units[u][2](h * HEAD_D, _bf(_sigmoid(o_pre) * hn))


def _gl_units(L, units, S_ref, consts):
    ltri, a2p, ab, gn = consts
    uh = _heads(units)
    mid = L // 2 - 1
    qs, kall, bcs, q_st, k_st, e_last = [], [], [], [], [], []
    spread = jnp.float32(0.0)
    for get, get_sm, _, _ in units:
        la = _log_sigmoid(_nn(_bf(get_sm(LANES, LANES)), a2p) + ab) / GL_TAU
        bc = _dot3r(ltri, la)
        b_last = bc[L - 1:L, :]
        qs.append(get(0, GL_DK * HEADS) * (GL_DK ** -0.5))
        kall.append(get(GL_DK * HEADS, GL_DK * HEADS))
        bcs.append(bc)
        spread = jnp.maximum(spread, jnp.max(jnp.abs(bc - bc[mid:mid + 1, :])))
        q_st.append(_bf(qs[-1] * jnp.exp(bc)))
        k_st.append(_bf(kall[-1] * jnp.exp(b_last - bc)))
        e_last.append(jnp.exp(b_last))
    causal = _causal(L)
    eye = _iota((GL_DK, GL_DK), 0) == _iota((GL_DK, GL_DK), 1)

    def ks(h):
        return slice(h * GL_DK, (h + 1) * GL_DK)

    def scores_factored():
        q_in = [_bf(qs[u] * jnp.exp(bcs[u] - bcs[u][mid:mid + 1, :])) for u in range(len(units))]
        k_in = [_bf(kall[u] * jnp.exp(bcs[u][mid:mid + 1, :] - bcs[u])) for u in range(len(units))]
        return [jnp.where(causal, _nt(q_in[u][:, ks(h)], k_in[u][:, ks(h)]), 0.0) for u, h in uh]

    def scores_direct():
        rows = _iota((L, GL_DK * HEADS), 0)
        cols = _iota((L, L), 1)

        def body(s, acc):
            out = []
            for u in range(len(units)):
                pick = rows == s
                k_s = jnp.sum(jnp.where(pick, kall[u], 0.0), axis=0, keepdims=True)
                b_s = jnp.sum(jnp.where(pick, bcs[u], 0.0), axis=0, keepdims=True)
                t = qs[u] * k_s * jnp.exp(jnp.where(rows >= s, bcs[u] - b_s, -jnp.inf))
                for h in range(HEADS):
                    col = jnp.sum(t[:, ks(h)], axis=1, keepdims=True)
                    out.append(jnp.where(cols == s, col, acc[u * HEADS + h]))
            return out

        return lax.fori_loop(0, L, body, [jnp.zeros((L, L), F32) for _ in uh])

    def rest(scores):
        A = scores()
        vb = [_bf(units[u][0](GROUP_W + h * HEAD_D, HEAD_D)) for u, h in uh]
        S = [S_ref[units[u][3], h] for u, h in uh]
        o = [_nn(_bf(A[i]), vb[i]) + _nn(q_st[u][:, ks(h)], _bf(S[i])) for i, (u, h) in enumerate(uh)]
        for i, (u, h) in enumerate(uh):
            e_col = jnp.sum(jnp.where(eye, e_last[u][:, ks(h)], 0.0), axis=1, keepdims=True)
            S_ref[units[u][3], h] = e_col * S[i] + _tn(k_st[u][:, ks(h)], vb[i])
        for i, (u, h) in enumerate(uh):
            on = o[i] * lax.rsqrt(jnp.mean(o[i] * o[i], axis=1, keepdims=True) + GN_EPS) * gn[:, _hs(h)]
            g = units[u][0](2 * GROUP_W + h * HEAD_D, HEAD_D)
            units[u][2](h * HEAD_D, _bf(_silu(g) * on))

    lax.cond(spread < GL_SAFE_SPREAD, lambda: rest(scores_factored), lambda: rest(scores_direct))


def _chunk_kernel(kind, nslab, nunit, L, n_state, n_const, n_alias, has_sm, own_layer, *refs):
    pos = 0
    p_refs = refs[pos:pos + nslab]; pos += nslab
    sm_refs = refs[pos:pos + (nslab if has_sm else 0)]; pos += (nslab if has_sm else 0)
    st_in = refs[pos:pos + n_state]; pos += n_state
    c_refs = refs[pos:pos + n_const]; pos += n_const + n_alias
    o_ref = refs[pos]; pos += 1
    st_out = refs[pos:pos + n_state]; pos += n_state
    c = pl.program_id(1)
    if own_layer is not None:
        stacked_ref = st_out[0]
        st_out = (stacked_ref.at[own_layer],) + tuple(st_out[1:])

    @pl.when(c == 0)
    def _():
        for a, b in zip(st_in, st_out):
            b[...] = a[...]
        if own_layer is not None:
            for d in range(stacked_ref.shape[0]):
                if d != own_layer:
                    stacked_ref[d] = jnp.zeros(stacked_ref.shape[1:], F32)

    def unit(slab, seq, sidx):
        r0 = seq * L if isinstance(seq, int) else pl.multiple_of(seq * L, SUBLANES)

        def get(col, width):
            return p_refs[slab][pl.ds(r0, L), col:col + width]

        def get_sm(col, width):
            return sm_refs[slab][pl.ds(r0, L), col:col + width]

        def put(col, val):
            o_ref[slab, pl.ds(r0, L), col:col + val.shape[1]] = val

        return (get, get_sm, put, sidx)

    def run(units):
        if kind == "ret":
            cosv, sinv = c_refs[0][...], c_refs[1][...]
            consts = (cosv, sinv, c_refs[2], c_refs[3], c_refs[4], c_refs[5][...], _RT_CDEC[L])
            _ret_units(L, units, st_out[0], consts)
        elif kind == "ml":
            _ml_units(L, units, st_out, tuple(r[...] for r in c_refs))
        else:
            _gl_units(L, units, st_out[0], tuple(r[...] for r in c_refs))

    if nunit == 1:
        run([unit(s, 0, s) for s in range(nslab)])
    else:
        def body(i, carry):
            run([unit(0, i * SEQ_GROUP + j, i * SEQ_GROUP + j) for j in range(SEQ_GROUP)])
            return carry
        lax.fori_loop(0, nunit // SEQ_GROUP, body, 0)


def _rt_log_gamma():
    return [math.log1p(-(2.0 ** (-5.0 - h))) for h in range(HEADS)]


_RT_CDEC = {L: [math.exp(L * lg) for lg in _rt_log_gamma()] for L in (8, CHUNK)}


def _chunk_call(kind, P, col_blk, col_w, sm, row0, nseq, T, states, consts, const_chunked, name, stacked=None):
    has_sm = sm
    if T % CHUNK == 0:
        L, nslab, nunit = CHUNK, nseq, 1
        grid = (1, T // L)
        rows = L
        def pmap(s, i, c):
            return ((row0 + s * T) // L + c, col_blk)
        def smap(s, i, c):
            return ((row0 + s * T) // L + c, 3)
        def omap(i, c):
            return (0, c, 0)
        sblk = nseq
    else:
        L, nslab, nunit = T, 1, 16
        grid = (nseq // nunit, 1)
        rows = L * nunit
        def pmap(s, i, c):
            return (row0 // rows + i, col_blk)
        def smap(s, i, c):
            return (row0 // rows + i, 3)
        def omap(i, c):
            return (0, i, 0)
        sblk = nunit
    in_specs = [pl.BlockSpec((rows, col_w), functools.partial(pmap, s)) for s in range(nslab)]
    args = [P] * nslab
    if has_sm:
        in_specs += [pl.BlockSpec((rows, GROUP_W), functools.partial(smap, s)) for s in range(nslab)]
        args += [P] * nslab
    st_specs, st_shapes = [], []
    for st in states:
        arr, layer = st if isinstance(st, tuple) else (st, None)
        tail = arr.shape[1:] if layer is None else arr.shape[2:]
        nd = len(tail)
        st_specs.append(pl.BlockSpec((sblk,) + tail, lambda i, c, nd=nd: (i,) + (0,) * nd))
        st_shapes.append(jax.ShapeDtypeStruct((nseq,) + tail, F32))
        if layer is None:
            in_specs.append(st_specs[-1])
        else:
            in_specs.append(pl.BlockSpec((None, sblk) + tail, lambda i, c, nd=nd, layer=layer: (layer, i) + (0,) * nd))
        args.append(arr)
    for cst, chunked in zip(consts, const_chunked):
        if chunked:
            in_specs.append(pl.BlockSpec((None,) + cst.shape[1:], lambda i, c: (c, 0, 0)))
        else:
            in_specs.append(pl.BlockSpec(cst.shape, lambda i, c, nd=cst.ndim: (0,) * nd))
        args.append(cst)
    aliases = {}
    own_layer = None
    if stacked is not None:
        depth, layer, prev = stacked
        tail = st_shapes[0].shape[1:]
        st_shapes[0] = jax.ShapeDtypeStruct((depth, nseq) + tail, F32)
        if prev is None:
            own_layer = layer
            st_specs[0] = pl.BlockSpec((depth, sblk) + tail, lambda i, c, nd=len(tail): (0, i) + (0,) * nd)
        else:
            st_specs[0] = pl.BlockSpec((None, sblk) + tail, lambda i, c, nd=len(tail): (layer, i) + (0,) * nd)
            aliases[len(args)] = 1
            in_specs.append(pl.BlockSpec(memory_space=pl.ANY))
            args.append(prev)
    out_specs = [pl.BlockSpec((nslab, rows, GROUP_W), omap)] + st_specs
    out_shape = [jax.ShapeDtypeStruct((nslab, nseq * T // nslab, GROUP_W), BF16)] + st_shapes
    outs = pl.pallas_call(
        functools.partial(_chunk_kernel, kind, nslab, nunit, L, len(states), len(consts), len(aliases), has_sm,
                          own_layer),
        grid=grid,
        in_specs=in_specs,
        out_specs=out_specs,
        out_shape=out_shape,
        input_output_aliases=aliases,
        compiler_params=_cparams(2),
        name=name,
    )(*args)
    return outs[0].reshape(nseq * T, GROUP_W), outs[1:]


def _pack_w_in(w_in):
    w_in = w_in.astype(BF16)
    z = jnp.zeros(w_in.shape[:2] + (GROUP_W - 184,), BF16)
    parts = [w_in[..., 0:1696], w_in[..., 3744:3752], w_in[..., 6824:6840], z,
             w_in[..., 1696:3744], w_in[..., 3752:5800], w_in[..., 5800:6824], w_in[..., 6840:7352]]
    return jnp.concatenate(parts, axis=-1)


def _row(v):
    return v.reshape(1, -1).astype(F32)


def _rows_at(mat, r0, rows_total):
    z = jnp.zeros((rows_total, mat.shape[1]), F32).at[r0:r0 + mat.shape[0]].set(mat)
    return z.astype(BF16)


def _ret_consts(pos0, T, L, gn):
    pos = pos0 + jnp.arange(T, dtype=F32)
    inv = 1.0 / (10000.0 ** jnp.linspace(0.0, 1.0, HEAD_D // 2, dtype=F32))
    ang = pos[:, None] * inv[None, :]
    cos, sin = jnp.cos(ang), jnp.sin(ang)
    cos2 = jnp.concatenate([cos, cos], axis=1).reshape(T // L, L, HEAD_D)
    sin2 = jnp.concatenate([-sin, sin], axis=1).reshape(T // L, L, HEAD_D)
    lg = jnp.log1p(-jnp.exp2(-5.0 - jnp.arange(HEADS, dtype=F32)))
    idx = jnp.arange(L, dtype=F32)
    rel = idx[:, None] - idx[None, :]
    dmat = jnp.where(rel >= 0, jnp.exp(jnp.maximum(rel, 0.0) * lg[:, None, None]), 0.0)
    q_dec = jnp.broadcast_to(jnp.exp((idx + 1.0) * lg[:, None])[..., None], (HEADS, L, HEAD_D))
    k_dec = jnp.broadcast_to(jnp.exp((L - 1.0 - idx) * lg[:, None])[..., None], (HEADS, L, HEAD_D))
    return [cos2, sin2, dmat, q_dec, k_dec, _row(gn)], [True, True, False, False, False, False]


def _tri(L):
    return jnp.tril(jnp.ones((L, L), F32)).astype(BF16)


def _moe_plan(ids, tm, n_tiles):
    e = ids.reshape(-1)
    n = e.shape[0]
    onehot = (e[:, None] == jnp.arange(N_EXPERTS, dtype=jnp.int32)[None, :]).astype(jnp.int32)
    rank = jnp.sum((jnp.cumsum(onehot, axis=0) - 1) * onehot, axis=1)
    counts = jnp.sum(onehot, axis=0)
    padded = ((counts + tm - 1) // tm) * tm
    ends = jnp.cumsum(padded)
    starts = ends - padded
    dest = jnp.sum(starts[None, :] * onehot, axis=1) + rank
    row_src = jnp.zeros((n_tiles * tm,), jnp.int32).at[dest].set(jnp.arange(n, dtype=jnp.int32) // 2,
                                                                 unique_indices=True)
    tile_start = jnp.arange(n_tiles, dtype=jnp.int32) * tm
    tile_expert = jnp.minimum(jnp.sum((tile_start[:, None] >= ends[None, :]).astype(jnp.int32), axis=1),
                              N_EXPERTS - 1)
    return row_src, dest.reshape(-1, 2), tile_expert, (ends[-1:] // tm).astype(jnp.int32)


def kernel(x_prompt, x_sample, state_rw_shift, state_rw_wkv, state_ml_conv, state_ml_C, state_ml_n, state_ml_m, state_rt_S, state_gl_S, w_in, rw_mu, rw_w0, rw_w2, rw_a0, rw_a2, rw_g2, rw_kk, rw_ka, rw_rk, rw_ln_w, rw_ln_b, ml_conv_w, ml_conv_b, ml_bi, ml_bf, ml_gn_w, rt_gn_w, gl_a2, gl_ab, gl_gn_w, w_out, ln1_w, ln1_b, ln2_w, ln2_b, ffn_w1, ffn_w3, ffn_w2, moe_router, moe_w1, moe_w3, moe_w2):
    Bp, Tp = x_prompt.shape[:2]
    Bs, Ts = x_sample.shape[:2]
    Mp, Ms = Bp * Tp, Bs * Ts
    M = Mp + Ms
    depth = w_in.shape[0]
    x = [x_prompt.reshape(Mp, D_MODEL), x_sample.reshape(Ms, D_MODEL)]
    xb = [v.astype(BF16) for v in x]
    w_in_p = _pack_w_in(w_in)
    w_out_b = w_out.astype(BF16)
    blockdiag = (jnp.arange(LANES)[:, None] // RW_HD) == (jnp.arange(LANES)[None, :] // RW_HD)
    g64 = blockdiag.astype(BF16)
    g64m = (blockdiag.astype(F32) / RW_HD).astype(BF16)
    pad_sh = 4 * GROUP_W - RW_COLS
    outs = {k: [[], []] for k in ("sh", "conv", "n", "m")}
    big = {k: [None, None] for k in ("wkv", "C", "rt", "gl")}

    for l in range(depth):
        P = _mm_pair(xb[0], xb[1], w_in_p, l, 3 * GROUP_W, "mm_in")

        prm = [_row(jnp.pad(rw_mu[l], (0, pad_sh))), _row(rw_w0[l]), _row(rw_a0[l]),
               _rows_at(rw_w2[l], 0, 2 * LANES), _rows_at(rw_a2[l], 32, 2 * LANES), _rows_at(rw_g2[l], 64, 2 * LANES),
               _row(rw_kk[l]), _row(rw_ka[l]), _row(rw_rk[l]), _row(rw_ln_w[l]), _row(rw_ln_b[l]), g64, g64m]
        o_rw_p, sh_p, big["wkv"][0] = _rwkv(P[0], 0, Bp, Tp, jnp.zeros((Bp, SUBLANES, 4 * GROUP_W), F32),
                                             jnp.zeros((Bp, RW_H, RW_HD, RW_HD), F32), prm,
                                             stacked=(depth, l, big["wkv"][0]))
        sh_in = jnp.pad(state_rw_shift[l], ((0, 0), (0, pad_sh)))[:, None, :]
        o_rw_s, sh_s, big["wkv"][1] = _rwkv(P[1], 0, Bs, Ts, sh_in, state_rw_wkv, prm, s0_layer=l,
                                             stacked=(depth, l, big["wkv"][1]))
        outs["sh"][0].append(sh_p[:, SUBLANES - 1, :RW_COLS])
        outs["sh"][1].append(sh_s[:, 0, :RW_COLS])

        bias_sm = jnp.zeros((LANES,), F32).at[SM_MLI - LANES:SM_MLI - LANES + HEADS].set(ml_bi[l])
        bias_sm = bias_sm.at[SM_MLF - LANES:SM_MLF - LANES + HEADS].set(ml_bf[l])

        def ml_consts(L):
            return [_tri(L), jnp.ones((L, LANES), BF16), ml_conv_w[l], _row(ml_conv_b[l]), _row(bias_sm), _row(ml_gn_w[l])]

        def ml_states(C, n, m, conv):
            b = n.shape[0]
            return [C, n[:, :, None, :], jnp.broadcast_to(m[:, :, None, None], (b, HEADS, 1, LANES)),
                    jnp.pad(conv, ((0, 0), (SUBLANES - conv.shape[1], 0), (0, 0)))]

        zp = lambda *s: jnp.zeros((Bp,) + s, F32)
        o_ml_p, st_p = _chunk_call("ml", P[0], 1, 4 * GROUP_W, True, 0, Bp, Tp,
                                   ml_states(zp(HEADS, HEAD_D, HEAD_D), zp(HEADS, HEAD_D), zp(HEADS), zp(3, 2 * GROUP_W)),
                                   ml_consts(CHUNK), [False] * 6, "mlstm_prompt", stacked=(depth, l, big["C"][0]))
        o_ml_s, st_s = _chunk_call("ml", P[1], 1, 4 * GROUP_W, True, 0, Bs, Ts,
                                   ml_states((state_ml_C, l), state_ml_n[l], state_ml_m[l], state_ml_conv[l]),
                                   ml_consts(Ts), [False] * 6, "mlstm_sample", stacked=(depth, l, big["C"][1]))
        for g, st in enumerate((st_p, st_s)):
            big["C"][g] = st[0]
            outs["n"][g].append(st[1][:, :, 0, :])
            outs["m"][g].append(st[2][:, :, 0, 0])
            outs["conv"][g].append(st[3][:, SUBLANES - 3:, :])

        c_p, ch_p = _ret_consts(0.0, Tp, CHUNK, rt_gn_w[l])
        c_s, ch_s = _ret_consts(float(PAST_LEN), Ts, Ts, rt_gn_w[l])
        o_rt_p, (big["rt"][0],) = _chunk_call("ret", P[0], 2, 4 * GROUP_W, False, 0, Bp, Tp, [zp(HEADS, HEAD_D, HEAD_D)],
                                              c_p, ch_p, "ret_prompt", stacked=(depth, l, big["rt"][0]))
        o_rt_s, (big["rt"][1],) = _chunk_call("ret", P[1], 2, 4 * GROUP_W, False, 0, Bs, Ts, [(state_rt_S, l)],
                                              c_s, ch_s, "ret_sample", stacked=(depth, l, big["rt"][1]))

        def gl_consts(L):
            return [_tri(L), _rows_at(gl_a2[l], SM_GLA - LANES, LANES), _row(gl_ab[l]), _row(gl_gn_w[l])]

        o_gl_p, (big["gl"][0],) = _chunk_call("gl", P[0], 4, 3 * GROUP_W, True, 0, Bp, Tp, [zp(HEADS, GL_DK, HEAD_D)],
                                              gl_consts(CHUNK), [False] * 4, "gla_prompt",
                                              stacked=(depth, l, big["gl"][0]))
        o_gl_s, (big["gl"][1],) = _chunk_call("gl", P[1], 4, 3 * GROUP_W, True, 0, Bs, Ts, [(state_gl_S, l)],
                                              gl_consts(Ts), [False] * 4, "gla_sample",
                                              stacked=(depth, l, big["gl"][1]))

        mixed = ((o_rw_p, o_ml_p, o_rt_p, o_gl_p), (o_rw_s, o_ml_s, o_rt_s, o_gl_s))
        for g in range(2):
            x[g], xb[g] = _mm_ln(list(mixed[g]), w_out_b, l, x[g], ln1_w[l], ln1_b[l])

        if l % 2 == 0:
            w1, w3, w2 = (w[l // 2][None].astype(BF16) for w in (ffn_w1, ffn_w3, ffn_w2))
            for g in range(2):
                rows = x[g].shape[0]
                tm = _pick(rows, (512, 256, 128))
                y = _ffn(xb[g], jnp.zeros((rows // tm,), jnp.int32), jnp.full((1,), rows // tm, jnp.int32), w1, w3, w2, tm)
                x[g], xb[g] = _add_ln(x[g], [y], None, ln2_w[l], ln2_b[l])
        else:
            tm = _pick(M, (512, 256, 128))
            n_tiles = 2 * M // tm + N_EXPERTS
            router_pad = jnp.pad(moe_router[l // 2], ((0, 0), (0, LANES - N_EXPERTS)))
            routed = [_router(v, router_pad) for v in x]
            gates = [r[0] for r in routed]
            ids = jnp.concatenate([r[1][:, :2] for r in routed], axis=0)
            row_src, dest, tile_expert, n_used = _moe_plan(ids, tm, n_tiles)
            ysorted = _ffn(jnp.concatenate(x, axis=0), tile_expert, n_used, moe_w1[l // 2].astype(BF16),
                           moe_w3[l // 2].astype(BF16), moe_w2[l // 2].astype(BF16), tm, row_src=row_src)
            for g, (r0, r1) in enumerate(((0, Mp), (Mp, M))):
                ys = [jnp.take(ysorted, dest[r0:r1, k], axis=0, mode="clip") for k in range(2)]
                x[g], xb[g] = _add_ln(x[g], ys, [gates[g][:, 0:1], gates[g][:, 1:2]], ln2_w[l], ln2_b[l])

    res = [x[0].reshape(Bp, Tp, D_MODEL), x[1].reshape(Bs, Ts, D_MODEL)]
    for key in ("sh", "wkv", "conv", "C", "n", "m", "rt", "gl"):
        res += list(big[key]) if key in big else [jnp.stack(outs[key][0]), jnp.stack(outs[key][1])]
    return tuple(res)
```
